```python
import math
import jax, jax.numpy as jnp
from jax import lax
import numpy as np

D_MODEL = 1024
BATCH = 8
SEQ = 8192
DEPTH = 2

SSM_WIDTH = 512
SSM_GROUP = 16
SSM_GROUPS = SSM_WIDTH // SSM_GROUP
SSM_STATE = 64
DT_MIN = 1e-3
DT_MAX = 1e-1
HEAD_DIM = 128
N_HEADS = D_MODEL // HEAD_DIM
N_KV_HEADS = 2
ATTN_WIDTH = N_HEADS * HEAD_DIM
KV_WIDTH = N_KV_HEADS * HEAD_DIM
IDX_HEADS = 8
IDX_DIM = 64
IDX_SCALE = (IDX_HEADS * IDX_DIM) ** -0.5
TOPK_MAX = 256
Q_BLOCK = 128
ROPE_THETA = 10000.0
D_FF = -(-8 * D_MODEL // (3 * 256)) * 256
DEEPNORM_ALPHA = (2 * DEPTH) ** 0.25
DEEPNORM_BETA = (8 * DEPTH) ** -0.25
LN_EPS = 1e-5
IN_SIZES = (SSM_WIDTH, ATTN_WIDTH, KV_WIDTH, KV_WIDTH, IDX_HEADS * IDX_DIM, IDX_DIM, IDX_HEADS, D_MODEL, D_MODEL)
IN_OFFSETS = [int(o) for o in np.cumsum(IN_SIZES)[:-1]]
D_IN = int(sum(IN_SIZES))

kernel_name = 'hybrid_s5_dsa_deepnorm_block'


def layer_norm(x, g, b):
    xf = x.astype(jnp.float32)
    mu = jnp.mean(xf, axis=-1, keepdims=True)
    var = jnp.mean(jnp.square(xf - mu), axis=-1, keepdims=True)
    y = (xf - mu) * lax.rsqrt(var + LN_EPS) * g.astype(jnp.float32) + b.astype(jnp.float32)
    return y.astype(x.dtype)


def rope(x, pos):
    half = x.shape[-1] // 2
    inv = ROPE_THETA ** (-jnp.arange(half, dtype=jnp.float32) / half)
    ang = pos.astype(jnp.float32)[:, None] * inv[None, :]
    cos = jnp.cos(ang)[:, None, :]
    sin = jnp.sin(ang)[:, None, :]
    xf = x.astype(jnp.float32)
    x1, x2 = xf[..., :half], xf[..., half:]
    out = jnp.concatenate([x1 * cos - x2 * sin, x2 * cos + x1 * sin], axis=-1)
    return out.astype(x.dtype)


def s5_branch(u, lam_re, lam_im, log_dt, b_re, b_im, c_re, c_im, d_skip, w_glu, b_glu):
    bsz, L, _ = u.shape
    f32 = jnp.float32
    uf = u.astype(f32).reshape(bsz, L, SSM_GROUPS, SSM_GROUP)
    dt = jnp.exp(log_dt.astype(f32))[:, None]
    lr, li = lam_re.astype(f32), lam_im.astype(f32)
    mag = jnp.exp(lr * dt)
    ar = mag * jnp.cos(li * dt)
    ai = mag * jnp.sin(li * dt)
    den = lr * lr + li * li
    nr = ar - 1.0
    fr = (nr * lr + ai * li) / den
    fi = (ai * lr - nr * li) / den
    br, bi = b_re.astype(f32), b_im.astype(f32)
    bbr = fr[..., None] * br - fi[..., None] * bi
    bbi = fr[..., None] * bi + fi[..., None] * br
    xr = jnp.einsum('bsgi,gpi->bsgp', uf, bbr)
    xi = jnp.einsum('bsgi,gpi->bsgp', uf, bbi)
    a_r = jnp.broadcast_to(ar, (1, L) + ar.shape)
    a_i = jnp.broadcast_to(ai, (1, L) + ai.shape)

    def combine(e1, e2):
        a1r, a1i, b1r, b1i = e1
        a2r, a2i, b2r, b2i = e2
        return (a1r * a2r - a1i * a2i,
                a1r * a2i + a1i * a2r,
                a2r * b1r - a2i * b1i + b2r,
                a2r * b1i + a2i * b1r + b2i)

    _, _, hr, hi = lax.associative_scan(combine, (a_r, a_i, xr, xi), axis=1)
    y = (jnp.einsum('bsgp,gip->bsgi', hr, c_re.astype(f32))
         - jnp.einsum('bsgp,gip->bsgi', hi, c_im.astype(f32))
         + d_skip.astype(f32).reshape(SSM_GROUPS, SSM_GROUP) * uf)
    y = jax.nn.gelu(y.reshape(bsz, L, SSM_WIDTH))
    y = y * jax.nn.sigmoid(y @ w_glu.astype(f32) + b_glu.astype(f32))
    return y.astype(u.dtype)


def dsa_branch(q, k, v, q_idx, k_idx, w_idx):
    bsz, L = q.shape[0], q.shape[1]
    n_sel = min(TOPK_MAX, L // 4)
    nblk = L // Q_BLOCK
    grp = N_HEADS // N_KV_HEADS
    key_pos = jnp.arange(L)
    bidx = jnp.arange(bsz)[:, None, None]

    def to_blocks(t):
        return jnp.swapaxes(t.reshape((bsz, nblk, Q_BLOCK) + t.shape[2:]), 0, 1)

    def block_fn(args):
        qb, qib, wb, start = args
        q_pos = start + jnp.arange(Q_BLOCK)
        causal = key_pos[None, :] <= q_pos[:, None]
        rel = jax.nn.relu(jnp.einsum('bqhd,bsd->bqhs', qib, k_idx).astype(jnp.float32))
        score = jnp.einsum('bqhs,bqh->bqs', rel, wb.astype(jnp.float32) * IDX_SCALE)
        score = jnp.where(causal[None], score, -jnp.inf)
        _, sel = lax.top_k(score, n_sel)
        valid = sel <= q_pos[None, :, None]
        ks = k[bidx, sel]
        vs = v[bidx, sel]
        qg = qb.reshape(bsz, Q_BLOCK, N_KV_HEADS, grp, HEAD_DIM)
        logits = jnp.einsum('bqhgd,bqnhd->bqhgn', qg, ks).astype(jnp.float32) * (HEAD_DIM ** -0.5)
        logits = jnp.where(valid[:, :, None, None, :], logits, -jnp.inf)
        p = jax.nn.softmax(logits, axis=-1).astype(vs.dtype)
        o = jnp.einsum('bqhgn,bqnhd->bqhgd', p, vs)
        return o.reshape(bsz, Q_BLOCK, ATTN_WIDTH)

    starts = jnp.arange(nblk) * Q_BLOCK
    out = lax.map(block_fn, (to_blocks(q), to_blocks(q_idx), to_blocks(w_idx), starts))
    return jnp.swapaxes(out, 0, 1).reshape(bsz, L, ATTN_WIDTH)


def hybrid_mixer(u, pos, w_in, lam_re, lam_im, log_dt, b_re, b_im, c_re, c_im, d_skip,
                 w_glu, b_glu, p_ssm, p_attn, w_out):
    bsz, L, _ = u.shape
    z = u @ w_in
    u_ssm, q, k, v, q_idx, k_idx, w_idx, g_ssm, g_attn = jnp.split(z, IN_OFFSETS, axis=-1)
    y_ssm = s5_branch(u_ssm, lam_re, lam_im, log_dt, b_re, b_im, c_re, c_im, d_skip, w_glu, b_glu)
    q = rope(q.reshape(bsz, L, N_HEADS, HEAD_DIM), pos)
    k = rope(k.reshape(bsz, L, N_KV_HEADS, HEAD_DIM), pos)
    v = v.reshape(bsz, L, N_KV_HEADS, HEAD_DIM)
    q_idx = rope(q_idx.reshape(bsz, L, IDX_HEADS, IDX_DIM), pos)
    k_idx = rope(k_idx.reshape(bsz, L, 1, IDX_DIM), pos)[:, :, 0]
    y_attn = dsa_branch(q, k, v, q_idx, k_idx, w_idx)
    merged = jax.nn.sigmoid(g_ssm) * (y_ssm @ p_ssm) + jax.nn.sigmoid(g_attn) * (y_attn @ p_attn)
    return merged @ w_out


def swiglu_ffn(u, w_gate_up, w_down):
    a, b = jnp.split(u @ w_gate_up, 2, axis=-1)
    return (jax.nn.silu(a) * b) @ w_down


def setup_inputs(seed: int = 0) -> dict:
    key = jax.random.key(seed)
    ks = jax.random.split(key, 24)
    f32 = jnp.float32

    def nrm(k, shape, std):
        return std * jax.random.normal(k, shape, f32)

    n = jnp.arange(SSM_STATE, dtype=f32)
    return {
        'x': nrm(ks[0], (BATCH, SEQ, D_MODEL), 1.0),
        'c': nrm(ks[1], (BATCH, D_MODEL), 1.0),
        'w_cond': nrm(ks[2], (DEPTH, D_MODEL, 6 * D_MODEL), 0.5 * D_MODEL ** -0.5),
        'b_cond': nrm(ks[3], (DEPTH, 6 * D_MODEL), 0.02),
        'w_in': nrm(ks[4], (DEPTH, D_MODEL, D_IN), D_MODEL ** -0.5),
        'ssm_lam_re': -0.5 + nrm(ks[5], (DEPTH, SSM_GROUPS, SSM_STATE), 0.01),
        'ssm_lam_im': math.pi * n + nrm(ks[6], (DEPTH, SSM_GROUPS, SSM_STATE), 0.01),
        'ssm_log_dt': jax.random.uniform(ks[7], (DEPTH, SSM_GROUPS), f32, math.log(DT_MIN), math.log(DT_MAX)),
        'ssm_b_re': nrm(ks[8], (DEPTH, SSM_GROUPS, SSM_STATE, SSM_GROUP), (2 * SSM_GROUP) ** -0.5),
        'ssm_b_im': nrm(ks[9], (DEPTH, SSM_GROUPS, SSM_STATE, SSM_GROUP), (2 * SSM_GROUP) ** -0.5),
        'ssm_c_re': nrm(ks[10], (DEPTH, SSM_GROUPS, SSM_GROUP, SSM_STATE), SSM_STATE ** -0.5),
        'ssm_c_im': nrm(ks[11], (DEPTH, SSM_GROUPS, SSM_GROUP, SSM_STATE), SSM_STATE ** -0.5),
        'ssm_d': nrm(ks[12], (DEPTH, SSM_WIDTH), 1.0),
        'ssm_w_glu': nrm(ks[13], (DEPTH, SSM_WIDTH, SSM_WIDTH), SSM_WIDTH ** -0.5),
        'ssm_b_glu': nrm(ks[14], (DEPTH, SSM_WIDTH), 0.02),
        'p_ssm': nrm(ks[15], (DEPTH, SSM_WIDTH, D_MODEL), SSM_WIDTH ** -0.5),
        'p_attn': nrm(ks[16], (DEPTH, ATTN_WIDTH, D_MODEL), ATTN_WIDTH ** -0.5),
        'w_out': nrm(ks[17], (DEPTH, D_MODEL, D_MODEL), DEEPNORM_BETA * D_MODEL ** -0.5),
        'ln1_g': 1.0 + nrm(ks[18], (DEPTH, D_MODEL), 0.02),
        'ln1_b': nrm(ks[19], (DEPTH, D_MODEL), 0.02),
        'w_gate_up': nrm(ks[20], (DEPTH, D_MODEL, 2 * D_FF), D_MODEL ** -0.5),
        'w_down': nrm(ks[21], (DEPTH, D_FF, D_MODEL), DEEPNORM_BETA * D_FF ** -0.5),
        'ln2_g': 1.0 + nrm(ks[22], (DEPTH, D_MODEL), 0.02),
        'ln2_b': nrm(ks[23], (DEPTH, D_MODEL), 0.02),
    }


def reference(x, c, w_cond, b_cond, w_in, ssm_lam_re, ssm_lam_im, ssm_log_dt, ssm_b_re, ssm_b_im,
              ssm_c_re, ssm_c_im, ssm_d, ssm_w_glu, ssm_b_glu, p_ssm, p_attn, w_out,
              ln1_g, ln1_b, w_gate_up, w_down, ln2_g, ln2_b):
    L = x.shape[1]
    pos = jnp.arange(L)
    cond_in = jax.nn.silu(c)
    for l in range(DEPTH):
        mod = cond_in @ w_cond[l] + b_cond[l]
        sh1, sc1, gt1, sh2, sc2, gt2 = [m[:, None, :] for m in jnp.split(mod, 6, axis=-1)]
        u = x * (1.0 + sc1) + sh1
        h = hybrid_mixer(u, pos, w_in[l], ssm_lam_re[l], ssm_lam_im[l], ssm_log_dt[l], ssm_b_re[l],
                         ssm_b_im[l], ssm_c_re[l], ssm_c_im[l], ssm_d[l], ssm_w_glu[l], ssm_b_glu[l],
                         p_ssm[l], p_attn[l], w_out[l])
        x = layer_norm(DEEPNORM_ALPHA * x + (1.0 + gt1) * h, ln1_g[l], ln1_b[l])
        u = x * (1.0 + sc2) + sh2
        f = swiglu_ffn(u, w_gate_up[l], w_down[l])
        x = layer_norm(DEEPNORM_ALPHA * x + (1.0 + gt2) * f, ln2_g[l], ln2_b[l])
    return x
```

```python
import functools
import math

import jax
import jax.numpy as jnp
import numpy as np
from jax import lax
from jax.experimental import pallas as pl
from jax.experimental.pallas import tpu as pltpu

F32 = jnp.float32
BF16 = jnp.bfloat16
I32 = jnp.int32
HIGHEST = lax.Precision.HIGHEST

D_MODEL = 1024
DEPTH = 2
SSM_WIDTH = 512
SSM_GROUP = 16
SSM_GROUPS = SSM_WIDTH // SSM_GROUP
SSM_STATE = 64
HEAD_DIM = 128
N_HEADS = D_MODEL // HEAD_DIM
N_KV_HEADS = 2
KV_GROUP = N_HEADS // N_KV_HEADS
ATTN_WIDTH = N_HEADS * HEAD_DIM
KV_WIDTH = N_KV_HEADS * HEAD_DIM
IDX_HEADS = 8
IDX_DIM = 64
IDX_WIDTH = IDX_HEADS * IDX_DIM
IDX_SCALE = (IDX_HEADS * IDX_DIM) ** -0.5
TOPK_MAX = 256
ROPE_THETA = 10000.0
D_FF = -(-8 * D_MODEL // (3 * 256)) * 256
DEEPNORM_ALPHA = (2 * DEPTH) ** 0.25
LN_EPS = 1e-5
IN_SIZES = (SSM_WIDTH, ATTN_WIDTH, KV_WIDTH, KV_WIDTH, IDX_WIDTH, IDX_DIM, IDX_HEADS, D_MODEL, D_MODEL)
IN_OFFSETS = [0] + [int(o) for o in np.cumsum(IN_SIZES)]

LANES = 128
SUBLANES = 8
VMEM_LIMIT_BYTES = 56 * 1024 * 1024

S5_CHUNK = LANES
Q_TILE = 256
K_TILE = 256
FF_CHUNK = 256
INT_MIN = -(2 ** 31)
NEG_BIG = -1e30
Q_SCALE = HEAD_DIM ** -0.5 * math.log2(math.e)


def _cparams(*sem):
    return pltpu.CompilerParams(dimension_semantics=sem, vmem_limit_bytes=VMEM_LIMIT_BYTES)


def _cond_kernel(c_ref, w_ref, b_ref, o_ref):
    c = c_ref[...]
    s = c * jax.nn.sigmoid(c)
    o_ref[0] = jnp.dot(s, w_ref[0], preferred_element_type=F32, precision=HIGHEST) + b_ref[0]


def _cond(c, w_cond, b_cond):
    depth, d, n = w_cond.shape
    bsz = c.shape[0]
    tn = 1536
    return pl.pallas_call(
        _cond_kernel,
        out_shape=jax.ShapeDtypeStruct((depth, bsz, n), F32),
        grid=(depth, n // tn),
        in_specs=[
            pl.BlockSpec((bsz, d), lambda l, j: (0, 0)),
            pl.BlockSpec((1, d, tn), lambda l, j: (l, 0, j)),
            pl.BlockSpec((1, 1, tn), lambda l, j: (l, 0, j)),
        ],
        out_specs=pl.BlockSpec((1, bsz, tn), lambda l, j: (l, 0, j)),
        compiler_params=_cparams("arbitrary", "arbitrary"),
        name="cond",
    )(c, w_cond, b_cond.reshape(depth, 1, n))


NAT_SSM, NAT_K, NAT_KI, NAT_G = 0, 512, 768, 896
NAT_WIDTH = 2944
TR_Q, TR_V, TR_QI, TR_W = 0, 1024, 1280, 1792
TR_ROWS = 1808


def _inproj_kernel(x_ref, sc_ref, sh_ref, wn_ref, wt_ref, cn_ref, sn_ref, cki_ref, ski_ref,
                   ct_ref, st_ref, ct32_ref, st32_ref,
                   ssm_ref, k_ref, ki_ref, sg_ref, qt_ref, vt_ref, qit_ref, wtt_ref):
    u = (x_ref[0] * (1.0 + sc_ref[0]) + sh_ref[0]).astype(BF16)
    zn = jnp.dot(u, wn_ref[...], preferred_element_type=F32)
    zt = lax.dot_general(wt_ref[...], u, (((1,), (1,)), ((), ())),
                         preferred_element_type=F32)

    ssm_ref[0] = zn[:, NAT_SSM:NAT_SSM + SSM_WIDTH].astype(BF16)
    sg_ref[0] = jax.nn.sigmoid(zn[:, NAT_G:NAT_G + 2 * D_MODEL]).astype(BF16)

    cn, sn = cn_ref[...], sn_ref[...]
    for h in range(N_KV_HEADS):
        xh = zn[:, NAT_K + h * HEAD_DIM:NAT_K + (h + 1) * HEAD_DIM]
        rot = pltpu.roll(xh, HEAD_DIM // 2, axis=1)
        k_ref[0, :, h * HEAD_DIM:(h + 1) * HEAD_DIM] = (xh * cn + rot * sn).astype(BF16)

    xk = zn[:, NAT_KI:NAT_KI + LANES]
    lane = lax.broadcasted_iota(I32, xk.shape, 1)
    rot = jnp.where((lane & (IDX_DIM // 2)) == 0,
                    pltpu.roll(xk, LANES - IDX_DIM // 2, axis=1),
                    pltpu.roll(xk, IDX_DIM // 2, axis=1))
    ki_ref[0] = (xk * cki_ref[...] + rot * ski_ref[...])[:, :IDX_DIM].astype(BF16)

    ct, st = ct_ref[...], st_ref[...]
    half = HEAD_DIM // 2
    for h in range(N_HEADS):
        r0 = TR_Q + h * HEAD_DIM
        x1, x2 = zt[r0:r0 + half], zt[r0 + half:r0 + HEAD_DIM]
        qt_ref[0, h * HEAD_DIM:h * HEAD_DIM + half, :] = ((x1 * ct - x2 * st) * Q_SCALE).astype(BF16)
        qt_ref[0, h * HEAD_DIM + half:(h + 1) * HEAD_DIM, :] = ((x2 * ct + x1 * st) * Q_SCALE).astype(BF16)
    ct32, st32 = ct32_ref[...], st32_ref[...]
    half = IDX_DIM // 2
    for h in range(IDX_HEADS):
        r0 = TR_QI + h * IDX_DIM
        x1, x2 = zt[r0:r0 + half], zt[r0 + half:r0 + IDX_DIM]
        qit_ref[0, h * IDX_DIM:h * IDX_DIM + half, :] = (x1 * ct32 - x2 * st32).astype(BF16)
        qit_ref[0, h * IDX_DIM + half:(h + 1) * IDX_DIM, :] = (x2 * ct32 + x1 * st32).astype(BF16)

    vt_ref[0] = zt[TR_V:TR_V + KV_WIDTH].astype(BF16)
    wtt_ref[0] = zt[TR_W:TR_W + IDX_HEADS] * IDX_SCALE


def _rope_tables(seq):
    pos = jnp.arange(seq, dtype=F32)

    def cs(half):
        inv = ROPE_THETA ** (-jnp.arange(half, dtype=F32) / half)
        ang = pos[:, None] * inv[None, :]
        return jnp.cos(ang), jnp.sin(ang)

    c64, s64 = cs(HEAD_DIM // 2)
    c32, s32 = cs(IDX_DIM // 2)
    z = jnp.zeros((seq, LANES - IDX_DIM), F32)
    return dict(
        cn=jnp.concatenate([c64, c64], -1), sn=jnp.concatenate([-s64, s64], -1),
        cki=jnp.concatenate([c32, c32, z], -1), ski=jnp.concatenate([-s32, s32, z], -1),
        ct=c64.T, st=s64.T, ct32=c32.T, st32=s32.T)


def _split_w_in(w_in):
    o = IN_OFFSETS
    ssm, q, k, v, qi, ki, wi, gs, ga = [w_in[:, o[i]:o[i + 1]] for i in range(9)]
    zpad = jnp.zeros((D_MODEL, LANES - IDX_DIM), w_in.dtype)
    wn = jnp.concatenate([ssm, k, ki, zpad, gs, ga], axis=1).astype(BF16)
    wt = jnp.concatenate([q, v, qi, wi, jnp.zeros((D_MODEL, TR_ROWS - TR_W - IDX_HEADS), w_in.dtype)],
                         axis=1).T.astype(BF16)
    return wn, wt


def _inproj(x, sc, sh, wn, wt, tabs, tm):
    bsz, seq, d = x.shape
    grid = (bsz, seq // tm)
    tok = lambda w: pl.BlockSpec((1, tm, w), lambda b, i: (b, i, 0))
    trn = lambda r: pl.BlockSpec((1, r, tm), lambda b, i: (b, 0, i))
    vec = pl.BlockSpec((1, 1, d), lambda b, i: (b, 0, 0))
    const = lambda a: pl.BlockSpec(a.shape, lambda b, i: (0, 0))
    ntab = lambda: pl.BlockSpec((tm, LANES), lambda b, i: (i, 0))
    ttab = lambda r: pl.BlockSpec((r, tm), lambda b, i: (0, i))
    out_shape = (
        jax.ShapeDtypeStruct((bsz, seq, SSM_WIDTH), BF16),
        jax.ShapeDtypeStruct((bsz, seq, KV_WIDTH), BF16),
        jax.ShapeDtypeStruct((bsz, seq, IDX_DIM), BF16),
        jax.ShapeDtypeStruct((bsz, seq, 2 * D_MODEL), BF16),
        jax.ShapeDtypeStruct((bsz, ATTN_WIDTH, seq), BF16),
        jax.ShapeDtypeStruct((bsz, KV_WIDTH, seq), BF16),
        jax.ShapeDtypeStruct((bsz, IDX_WIDTH, seq), BF16),
        jax.ShapeDtypeStruct((bsz, IDX_HEADS, seq), F32),
    )
    return pl.pallas_call(
        _inproj_kernel,
        out_shape=out_shape,
        grid=grid,
        in_specs=[tok(d), vec, vec, const(wn), const(wt), ntab(), ntab(), ntab(), ntab(),
                  ttab(HEAD_DIM // 2), ttab(HEAD_DIM // 2), ttab(IDX_DIM // 2), ttab(IDX_DIM // 2)],
        out_specs=(tok(SSM_WIDTH), tok(KV_WIDTH), tok(IDX_DIM), tok(2 * D_MODEL),
                   trn(ATTN_WIDTH), trn(KV_WIDTH), trn(IDX_WIDTH), trn(IDX_HEADS)),
        compiler_params=_cparams("arbitrary", "arbitrary"),
        name="in_proj",
    )(x, sc, sh, wn, wt, tabs["cn"], tabs["sn"], tabs["cki"], tabs["ski"],
      tabs["ct"], tabs["st"], tabs["ct32"], tabs["st32"])


def _s5_kernel(u_ref, lrr_ref, lir_ref, ldr_ref, lrc_ref, lic_ref, ldc_ref,
               brt_ref, bit_ref, cr_ref, ci_ref, crt_ref, cit_ref, dv_ref,
               y_ref, w_scr, kv_scr, *, bsz):
    t = S5_CHUNK
    p = SSM_STATE
    gsz = SSM_GROUP
    lrr, lir = lrr_ref[0], lir_ref[0]
    dtr = jnp.exp(ldr_ref[0])
    lrc, lic = lrc_ref[0], lic_ref[0]
    dtc = jnp.exp(ldc_ref[0])

    mag = jnp.exp(lrr * dtr)
    ar, ai = mag * jnp.cos(lir * dtr), mag * jnp.sin(lir * dtr)
    den = lrr * lrr + lir * lir
    nr = ar - 1.0
    fr, fi = (nr * lrr + ai * lir) / den, (ai * lrr - nr * lir) / den
    brt, bit = brt_ref[0], bit_ref[0]
    bbr, bbi = fr * brt - fi * bit, fr * bit + fi * brt

    cr, ci = cr_ref[0], ci_ref[0]
    bbr_h, bbi_h = bbr[:, :p], bbi[:, :p]
    cbr = jnp.concatenate([cr[i:i + 1] * bbr_h - ci[i:i + 1] * bbi_h for i in range(gsz)], axis=0)
    cbi = jnp.concatenate([cr[i:i + 1] * bbi_h + ci[i:i + 1] * bbr_h for i in range(gsz)], axis=0)
    lag = lax.broadcasted_iota(I32, (p, t), 1).astype(F32)
    lrc_h, lic_h, dtc_h = lrc[:p], lic[:p], dtc[:p]
    pmag = jnp.exp(lrc_h * dtc_h * lag)
    pth = lic_h * dtc_h * lag
    kv_scr[...] = (jnp.dot(cbr, pmag * jnp.cos(pth), preferred_element_type=F32, precision=HIGHEST)
                   - jnp.dot(cbi, pmag * jnp.sin(pth), preferred_element_type=F32, precision=HIGHEST))

    row = lax.broadcasted_iota(I32, (t, t), 0)
    col = lax.broadcasted_iota(I32, (t, t), 1)
    causal = col >= row

    def toeplitz_cols(i, carry):
        c0 = pl.multiple_of(i * t, t)
        for k in range(gsz):
            vec = kv_scr[pl.ds(i * gsz + k, 1), :]
            blk = pltpu.roll(jnp.broadcast_to(vec, (t, t)), 0, 1, stride=1, stride_axis=0)
            w_scr[k * t:(k + 1) * t, pl.ds(c0, t)] = jnp.where(causal, blk, 0.0).astype(BF16)
        return carry

    lax.fori_loop(0, gsz, toeplitz_cols, 0)

    lo = lax.broadcasted_iota(I32, (t, 2 * p), 1) < p
    rem = (t - 1 - lax.broadcasted_iota(I32, (t, 2 * p), 0)).astype(F32)
    emag = jnp.exp(lrr * dtr * rem)
    eth = lir * dtr * rem
    ecs, esn = emag * jnp.cos(eth), emag * jnp.sin(eth)
    e_same, e_swap = jnp.where(lo, ecs, esn), jnp.where(lo, esn, ecs)
    sgn = jnp.where(lo[:1], -1.0, 1.0)
    wb = jnp.concatenate([e_same * bbr[k:k + 1] + e_swap * (bbi[k:k + 1] * sgn) for k in range(gsz)],
                         axis=0).astype(BF16)

    top = lax.broadcasted_iota(I32, (2 * p, t), 0) < p
    step = (lax.broadcasted_iota(I32, (2 * p, t), 1) + 1).astype(F32)
    cmag = jnp.exp(lrc * dtc * step)
    cth = lic * dtc * step
    mc, ms = cmag * jnp.cos(cth), cmag * jnp.sin(cth)
    crt, cit = crt_ref[0], cit_ref[0]
    wc = jnp.concatenate(
        [jnp.where(top, crt[:, i:i + 1] * mc - cit[:, i:i + 1] * ms,
                   -(crt[:, i:i + 1] * ms + cit[:, i:i + 1] * mc)) for i in range(gsz)],
        axis=1).astype(BF16)

    u = u_ref[0]
    y = jnp.dot(u, w_scr[...], preferred_element_type=F32)
    g = jnp.dot(u, wb, preferred_element_type=F32)
    gr, gi = g[:, :p], g[:, p:]

    m_t = jnp.exp(lrr[:, :p] * dtr[:, :p] * float(t))
    at_r = m_t * jnp.cos(lir[:, :p] * dtr[:, :p] * float(t))
    at_i = m_t * jnp.sin(lir[:, :p] * dtr[:, :p] * float(t))
    nc = u.shape[0] // bsz
    hr = jnp.zeros((bsz, p), F32)
    hi = jnp.zeros((bsz, p), F32)
    hrs, his = [hr], [hi]
    for c in range(1, nc):
        sl = slice((c - 1) * bsz, c * bsz)
        hr, hi = at_r * hr - at_i * hi + gr[sl], at_r * hi + at_i * hr + gi[sl]
        hrs.append(hr)
        his.append(hi)
    hcat = jnp.concatenate([jnp.concatenate(hrs, axis=0), jnp.concatenate(his, axis=0)], axis=1)
    y = y + jnp.dot(hcat.astype(BF16), wc, preferred_element_type=F32)
    y_ref[0] = (y + dv_ref[0] * u.astype(F32)).astype(BF16)


def _s5(u_ssm, lam_re, lam_im, log_dt, b_re, b_im, c_re, c_im, d_skip):
    bsz, seq, _ = u_ssm.shape
    t, g, gs, p = S5_CHUNK, SSM_GROUPS, SSM_GROUP, SSM_STATE
    nc = seq // t
    rows = nc * bsz
    ug = u_ssm.reshape(bsz, nc, t, g, gs).transpose(3, 1, 0, 4, 2).reshape(g, rows, gs * t)

    dup_r = lambda a: jnp.concatenate([a, a], axis=-1)[:, None, :]
    dup_c = lambda a: jnp.concatenate([a, a], axis=-1)[:, :, None]
    ld = jnp.broadcast_to(log_dt[:, None], (g, p))
    brt = jnp.swapaxes(b_re, 1, 2)
    bit = jnp.swapaxes(b_im, 1, 2)
    crt = jnp.swapaxes(c_re, 1, 2)
    cit = jnp.swapaxes(c_im, 1, 2)
    args = (
        ug,
        dup_r(lam_re), dup_r(lam_im), dup_r(ld), dup_c(lam_re), dup_c(lam_im), dup_c(ld),
        jnp.concatenate([brt, brt], -1), jnp.concatenate([bit, bit], -1),
        c_re, c_im, jnp.concatenate([crt, crt], 1), jnp.concatenate([cit, cit], 1),
        jnp.repeat(d_skip.reshape(g, gs), t, axis=1)[:, None, :],
    )
    per_g = lambda a: pl.BlockSpec((1,) + a.shape[1:], lambda i: (i,) + (0,) * (a.ndim - 1))
    yg = pl.pallas_call(
        functools.partial(_s5_kernel, bsz=bsz),
        out_shape=jax.ShapeDtypeStruct((g, rows, gs * t), BF16),
        grid=(g,),
        in_specs=[per_g(a) for a in args],
        out_specs=pl.BlockSpec((1, rows, gs * t), lambda i: (i, 0, 0)),
        scratch_shapes=[pltpu.VMEM((gs * t, gs * t), BF16), pltpu.VMEM((gs * gs, t), F32)],
        compiler_params=_cparams("arbitrary"),
        name="s5",
    )(*args)
    return yg.reshape(g, nc, bsz, gs, t).transpose(2, 1, 4, 0, 3).reshape(bsz, seq, g * gs)


def _dsa_kernel(qt_ref, qit_ref, wt_ref, k_ref, vt_ref, ki_ref, y_ref,
                key_scr, m_scr, l_scr, acc_scr, *, n_sel, idx_bits):
    tq, tk = Q_TILE, K_TILE
    qi = pl.program_id(1)
    nkt = qi + 1
    q0 = qi * tq
    diff = (lax.broadcasted_iota(I32, (tk, tq), 0) - lax.broadcasted_iota(I32, (tk, tq), 1))

    def score_tile(kt, carry):
        k0 = pl.multiple_of(kt * tk, tk)
        ki_t = ki_ref[0, pl.ds(k0, tk), :]
        s = jnp.zeros((tk, tq), F32)
        for h in range(IDX_HEADS):
            rel = jnp.dot(ki_t, qit_ref[0, h * IDX_DIM:(h + 1) * IDX_DIM, :], preferred_element_type=F32)
            s = s + jnp.maximum(rel, 0.0) * wt_ref[0, h:h + 1, :]
        bits = lax.bitcast_convert_type(s, I32)
        key = bits ^ ((bits >> 31) & 0x7FFFFFFF)
        key = jnp.where(s == 0.0, 0, key)
        key = jnp.where(diff <= q0 - k0, key, INT_MIN)
        key_scr[pl.ds(k0, tk), :] = key
        return carry

    lax.fori_loop(0, nkt, score_tile, 0)

    def count(pred):
        def body(kt, acc):
            k0 = pl.multiple_of(kt * tk, tk)
            hit = jnp.where(pred(key_scr[pl.ds(k0, tk), :], k0), 1, 0)
            return acc + jnp.sum(hit.reshape(tk // SUBLANES, SUBLANES, tq), axis=0)
        part = lax.fori_loop(0, nkt, body, jnp.zeros((SUBLANES, tq), I32))
        return jnp.sum(part, axis=0, keepdims=True)

    def bit_step(i, c_u):
        trial = c_u | (jnp.int32(1) << (31 - i))
        trial_s = trial ^ INT_MIN
        cnt = count(lambda kt_keys, k0: kt_keys >= trial_s)
        return jnp.where(cnt >= n_sel, trial, c_u)

    c_u = lax.fori_loop(0, 32, bit_step, jnp.zeros((1, tq), I32))
    thr = c_u ^ INT_MIN
    cnt_gt = count(lambda kt_keys, k0: kt_keys > thr)
    cnt_ge = count(lambda kt_keys, k0: kt_keys >= thr)

    @pl.when(jnp.max(cnt_ge) > n_sel)
    def _():
        need = n_sel - cnt_gt
        sub = lax.broadcasted_iota(I32, (tk, tq), 0)

        def idx_step(i, j_cut):
            trial = j_cut | (jnp.int32(1) << (idx_bits - 1 - i))
            cnt = count(lambda kt_keys, k0: (kt_keys == thr) & (sub + k0 < trial))
            return jnp.where(cnt <= need, trial, j_cut)

        j_cut = lax.fori_loop(0, idx_bits, idx_step, jnp.zeros((1, tq), I32))

        def drop(kt, carry):
            k0 = pl.multiple_of(kt * tk, tk)
            keys = key_scr[pl.ds(k0, tk), :]
            key_scr[pl.ds(k0, tk), :] = jnp.where((keys == thr) & (sub + k0 >= j_cut), INT_MIN, keys)
            return carry

        lax.fori_loop(0, nkt, drop, 0)

    thr_sel = jnp.maximum(thr, INT_MIN + 1)

    m_scr[...] = jnp.full(m_scr.shape, NEG_BIG, F32)
    l_scr[...] = jnp.zeros(l_scr.shape, F32)
    acc_scr[...] = jnp.zeros(acc_scr.shape, F32)

    def attend(kt, carry):
        k0 = pl.multiple_of(kt * tk, tk)
        sel = key_scr[pl.ds(k0, tk), :] >= thr_sel
        for g in range(N_KV_HEADS):
            k_t = k_ref[0, pl.ds(k0, tk), g * HEAD_DIM:(g + 1) * HEAD_DIM]
            vt_t = vt_ref[0, g * HEAD_DIM:(g + 1) * HEAD_DIM, pl.ds(k0, tk)]
            for hh in range(KV_GROUP):
                h = g * KV_GROUP + hh
                lg = jnp.dot(k_t, qt_ref[0, h * HEAD_DIM:(h + 1) * HEAD_DIM, :], preferred_element_type=F32)
                lg = jnp.where(sel, lg, NEG_BIG)
                m_old = m_scr[h]
                m_new = jnp.maximum(m_old, jnp.max(lg, axis=0, keepdims=True))
                pr = jnp.exp2(lg - m_new)
                alpha = jnp.exp2(m_old - m_new)
                l_scr[h] = alpha * l_scr[h] + jnp.sum(pr, axis=0, keepdims=True)
                acc_scr[h] = alpha * acc_scr[h] + jnp.dot(vt_t, pr.astype(BF16), preferred_element_type=F32)
                m_scr[h] = m_new
        return carry

    lax.fori_loop(0, nkt, attend, 0)

    for h in range(N_HEADS):
        o = acc_scr[h] / l_scr[h]
        y_ref[0, :, h * HEAD_DIM:(h + 1) * HEAD_DIM] = o.T.astype(BF16)


def _dsa(qt, qit, wtt, k, vt, ki):
    bsz, seq, _ = k.shape
    n_sel = min(TOPK_MAX, seq // 4)
    tq = Q_TILE
    kern = functools.partial(_dsa_kernel, n_sel=n_sel, idx_bits=seq.bit_length())
    return pl.pallas_call(
        kern,
        out_shape=jax.ShapeDtypeStruct((bsz, seq, ATTN_WIDTH), BF16),
        grid=(bsz, seq // tq),
        in_specs=[
            pl.BlockSpec((1, ATTN_WIDTH, tq), lambda b, i: (b, 0, i)),
            pl.BlockSpec((1, IDX_WIDTH, tq), lambda b, i: (b, 0, i)),
            pl.BlockSpec((1, IDX_HEADS, tq), lambda b, i: (b, 0, i)),
            pl.BlockSpec((1, seq, KV_WIDTH), lambda b, i: (b, 0, 0)),
            pl.BlockSpec((1, KV_WIDTH, seq), lambda b, i: (b, 0, 0)),
            pl.BlockSpec((1, seq, IDX_DIM), lambda b, i: (b, 0, 0)),
        ],
        out_specs=pl.BlockSpec((1, tq, ATTN_WIDTH), lambda b, i: (b, i, 0)),
        scratch_shapes=[
            pltpu.VMEM((seq, tq), I32),
            pltpu.VMEM((N_HEADS, 1, tq), F32),
            pltpu.VMEM((N_HEADS, 1, tq), F32),
            pltpu.VMEM((N_HEADS, HEAD_DIM, tq), F32),
        ],
        compiler_params=_cparams("arbitrary", "arbitrary"),
        name="dsa",
    )(qt, qit, wtt, k, vt, ki)


def _deepnorm_ln(x, h, gate, g, b):
    y = DEEPNORM_ALPHA * x + (1.0 + gate) * h
    mu = jnp.mean(y, axis=-1, keepdims=True)
    yc = y - mu
    var = jnp.mean(yc * yc, axis=-1, keepdims=True)
    return yc * lax.rsqrt(var + LN_EPS) * g + b


def _merge_kernel(x_ref, ys_ref, ya_ref, sg_ref, gt_ref, wglu_ref, bglu_ref, pssm_ref, pattn_ref,
                  wout_ref, lng_ref, lnb_ref, o_ref):
    y = jax.nn.gelu(ys_ref[0].astype(F32))
    glu = jnp.dot(y.astype(BF16), wglu_ref[...], preferred_element_type=F32) + bglu_ref[...]
    y = (y * jax.nn.sigmoid(glu)).astype(BF16)
    sg = sg_ref[0]
    merged = (sg[:, :D_MODEL].astype(F32) * jnp.dot(y, pssm_ref[...], preferred_element_type=F32)
              + sg[:, D_MODEL:].astype(F32) * jnp.dot(ya_ref[0], pattn_ref[...], preferred_element_type=F32))
    h = jnp.dot(merged.astype(BF16), wout_ref[...], preferred_element_type=F32)
    o_ref[0] = _deepnorm_ln(x_ref[0], h, gt_ref[0], lng_ref[...], lnb_ref[...])


def _merge(x, ys, ya, sg, gt, wglu, bglu, pssm, pattn, wout, lng, lnb, tm):
    bsz, seq, d = x.shape
    tok = lambda w: pl.BlockSpec((1, tm, w), lambda b, i: (b, i, 0))
    vec = pl.BlockSpec((1, 1, d), lambda b, i: (b, 0, 0))
    const = lambda a: pl.BlockSpec(a.shape, lambda b, i: (0, 0))
    return pl.pallas_call(
        _merge_kernel,
        out_shape=jax.ShapeDtypeStruct(x.shape, F32),
        grid=(bsz, seq // tm),
        in_specs=[tok(d), tok(SSM_WIDTH), tok(ATTN_WIDTH), tok(2 * D_MODEL), vec,
                  const(wglu), const(bglu), const(pssm), const(pattn), const(wout), const(lng), const(lnb)],
        out_specs=tok(d),
        compiler_params=_cparams("arbitrary", "arbitrary"),
        name="merge",
    )(x, ys, ya, sg, gt, wglu, bglu, pssm, pattn, wout, lng, lnb)


def _ffn_kernel(x_ref, sc_ref, sh_ref, gt_ref, wgu_ref, wd_ref, lng_ref, lnb_ref, o_ref, hid_scr):
    x = x_ref[0]
    u = (x * (1.0 + sc_ref[0]) + sh_ref[0]).astype(BF16)
    for j in range(D_FF // FF_CHUNK):
        a = jnp.dot(u, wgu_ref[:, j * FF_CHUNK:(j + 1) * FF_CHUNK], preferred_element_type=F32)
        b = jnp.dot(u, wgu_ref[:, D_FF + j * FF_CHUNK:D_FF + (j + 1) * FF_CHUNK], preferred_element_type=F32)
        hid_scr[:, j * FF_CHUNK:(j + 1) * FF_CHUNK] = (a * jax.nn.sigmoid(a) * b).astype(BF16)
    f = jnp.dot(hid_scr[...], wd_ref[...], preferred_element_type=F32)
    o_ref[0] = _deepnorm_ln(x, f, gt_ref[0], lng_ref[...], lnb_ref[...])


def _ffn(x, sc, sh, gt, wgu, wd, lng, lnb, tm):
    bsz, seq, d = x.shape
    tok = pl.BlockSpec((1, tm, d), lambda b, i: (b, i, 0))
    vec = pl.BlockSpec((1, 1, d), lambda b, i: (b, 0, 0))
    const = lambda a: pl.BlockSpec(a.shape, lambda b, i: (0, 0))
    return pl.pallas_call(
        _ffn_kernel,
        out_shape=jax.ShapeDtypeStruct(x.shape, F32),
        grid=(bsz, seq // tm),
        in_specs=[tok, vec, vec, vec, const(wgu), const(wd), const(lng), const(lnb)],
        out_specs=tok,
        scratch_shapes=[pltpu.VMEM((tm, D_FF), BF16)],
        compiler_params=_cparams("arbitrary", "arbitrary"),
        name="ffn",
    )(x, sc, sh, gt, wgu, wd, lng, lnb)


def kernel(x, c, w_cond, b_cond, w_in, ssm_lam_re, ssm_lam_im, ssm_log_dt, ssm_b_re, ssm_b_im,
           ssm_c_re, ssm_c_im, ssm_d, ssm_w_glu, ssm_b_glu, p_ssm, p_attn, w_out,
           ln1_g, ln1_b, w_gate_up, w_down, ln2_g, ln2_b):
    bsz, seq, d = x.shape
    assert d == D_MODEL and seq % Q_TILE == 0 and seq % S5_CHUNK == 0
    tm = min(512, seq)
    tabs = _rope_tables(seq)
    mod = _cond(c, w_cond, b_cond)
    row = lambda a: a.reshape(1, -1)
    for l in range(DEPTH):
        sh1, sc1, gt1, sh2, sc2, gt2 = [mod[l, :, i * d:(i + 1) * d][:, None, :] for i in range(6)]
        wn, wt = _split_w_in(w_in[l])
        u_ssm, k, ki, sg, qt, vt, qit, wtt = _inproj(x, sc1, sh1, wn, wt, tabs, tm)
        ys = _s5(u_ssm, ssm_lam_re[l], ssm_lam_im[l], ssm_log_dt[l], ssm_b_re[l], ssm_b_im[l],
                 ssm_c_re[l], ssm_c_im[l], ssm_d[l])
        ya = _dsa(qt, qit, wtt, k, vt, ki)
        x = _merge(x, ys, ya, sg, gt1, ssm_w_glu[l].astype(BF16), row(ssm_b_glu[l]),
                   p_ssm[l].astype(BF16), p_attn[l].astype(BF16), w_out[l].astype(BF16),
                   row(ln1_g[l]), row(ln1_b[l]), tm)
        x = _ffn(x, sc2, sh2, gt2, w_gate_up[l].astype(BF16), w_down[l].astype(BF16),
                 row(ln2_g[l]), row(ln2_b[l]), tm)
    return x
```

```python
import functools
import math

import jax
import jax.numpy as jnp
import numpy as np
from jax import lax
from jax.experimental import pallas as pl
from jax.experimental.pallas import tpu as pltpu

F32 = jnp.float32
BF16 = jnp.bfloat16
I32 = jnp.int32
HIGHEST = lax.Precision.HIGHEST

D_MODEL = 1024
DEPTH = 2
SSM_WIDTH = 512
SSM_GROUP = 16
SSM_GROUPS = SSM_WIDTH // SSM_GROUP
SSM_STATE = 64
HEAD_DIM = 128
N_HEADS = D_MODEL // HEAD_DIM
N_KV_HEADS = 2
KV_GROUP = N_HEADS // N_KV_HEADS
ATTN_WIDTH = N_HEADS * HEAD_DIM
KV_WIDTH = N_KV_HEADS * HEAD_DIM
IDX_HEADS = 8
IDX_DIM = 64
IDX_WIDTH = IDX_HEADS * IDX_DIM
IDX_SCALE = (IDX_HEADS * IDX_DIM) ** -0.5
TOPK_MAX = 256
ROPE_THETA = 10000.0
D_FF = -(-8 * D_MODEL // (3 * 256)) * 256
DEEPNORM_ALPHA = (2 * DEPTH) ** 0.25
LN_EPS = 1e-5
IN_SIZES = (SSM_WIDTH, ATTN_WIDTH, KV_WIDTH, KV_WIDTH, IDX_WIDTH, IDX_DIM, IDX_HEADS, D_MODEL, D_MODEL)
IN_OFFSETS = [0] + [int(o) for o in np.cumsum(IN_SIZES)]

LANES = 128
SUBLANES = 8
VMEM_LIMIT_BYTES = 56 * 1024 * 1024

S5_CHUNK = LANES
Q_TILE = 256
K_TILE = 256
FF_CHUNK = 256
INT_MIN = -(2 ** 31)
NEG_BIG = -1e30
Q_SCALE = HEAD_DIM ** -0.5 * math.log2(math.e)


def _cparams(*sem):
    return pltpu.CompilerParams(dimension_semantics=sem, vmem_limit_bytes=VMEM_LIMIT_BYTES)


def _cond_kernel(c_ref, w_ref, b_ref, o_ref):
    c = c_ref[...]
    s = c * jax.nn.sigmoid(c)
    o_ref[0] = jnp.dot(s, w_ref[0], preferred_element_type=F32, precision=HIGHEST) + b_ref[0]


def _cond(c, w_cond, b_cond):
    depth, d, n = w_cond.shape
    bsz = c.shape[0]
    tn = 1536
    return pl.pallas_call(
        _cond_kernel,
        out_shape=jax.ShapeDtypeStruct((depth, bsz, n), F32),
        grid=(depth, n // tn),
        in_specs=[
            pl.BlockSpec((bsz, d), lambda l, j: (0, 0)),
            pl.BlockSpec((1, d, tn), lambda l, j: (l, 0, j)),
            pl.BlockSpec((1, 1, tn), lambda l, j: (l, 0, j)),
        ],
        out_specs=pl.BlockSpec((1, bsz, tn), lambda l, j: (l, 0, j)),
        compiler_params=_cparams("arbitrary", "arbitrary"),
        name="cond",
    )(c, w_cond, b_cond.reshape(depth, 1, n))


NAT_SSM, NAT_K, NAT_KI, NAT_G = 0, 512, 768, 896
NAT_WIDTH = 2944
TR_Q, TR_V, TR_QI, TR_W = 0, 1024, 1280, 1792
TR_ROWS = 1808


def _inproj_kernel(x_ref, sc_ref, sh_ref, wn_ref, wt_ref, cn_ref, sn_ref, cki_ref, ski_ref,
                   ct_ref, st_ref, ct32_ref, st32_ref,
                   ssm_ref, k_ref, ki_ref, sg_ref, qt_ref, vt_ref, qit_ref, wtt_ref):
    u = (x_ref[0] * (1.0 + sc_ref[0]) + sh_ref[0]).astype(BF16)
    zn = jnp.dot(u, wn_ref[...], preferred_element_type=F32)
    zt = lax.dot_general(wt_ref[...], u, (((1,), (1,)), ((), ())),
                         preferred_element_type=F32)

    ssm_ref[0] = zn[:, NAT_SSM:NAT_SSM + SSM_WIDTH].astype(BF16)
    sg_ref[0] = jax.nn.sigmoid(zn[:, NAT_G:NAT_G + 2 * D_MODEL]).astype(BF16)

    cn, sn = cn_ref[...], sn_ref[...]
    for h in range(N_KV_HEADS):
        xh = zn[:, NAT_K + h * HEAD_DIM:NAT_K + (h + 1) * HEAD_DIM]
        rot = pltpu.roll(xh, HEAD_DIM // 2, axis=1)
        k_ref[0, :, h * HEAD_DIM:(h + 1) * HEAD_DIM] = (xh * cn + rot * sn).astype(BF16)

    xk = zn[:, NAT_KI:NAT_KI + LANES]
    lane = lax.broadcasted_iota(I32, xk.shape, 1)
    rot = jnp.where((lane & (IDX_DIM // 2)) == 0,
                    pltpu.roll(xk, LANES - IDX_DIM // 2, axis=1),
                    pltpu.roll(xk, IDX_DIM // 2, axis=1))
    ki_ref[0] = (xk * cki_ref[...] + rot * ski_ref[...])[:, :IDX_DIM].astype(BF16)

    ct, st = ct_ref[...], st_ref[...]
    half = HEAD_DIM // 2
    for h in range(N_HEADS):
        r0 = TR_Q + h * HEAD_DIM
        x1, x2 = zt[r0:r0 + half], zt[r0 + half:r0 + HEAD_DIM]
        qt_ref[0, h * HEAD_DIM:h * HEAD_DIM + half, :] = ((x1 * ct - x2 * st) * Q_SCALE).astype(BF16)
        qt_ref[0, h * HEAD_DIM + half:(h + 1) * HEAD_DIM, :] = ((x2 * ct + x1 * st) * Q_SCALE).astype(BF16)
    ct32, st32 = ct32_ref[...], st32_ref[...]
    half = IDX_DIM // 2
    for h in range(IDX_HEADS):
        r0 = TR_QI + h * IDX_DIM
        x1, x2 = zt[r0:r0 + half], zt[r0 + half:r0 + IDX_DIM]
        qit_ref[0, h * IDX_DIM:h * IDX_DIM + half, :] = (x1 * ct32 - x2 * st32).astype(BF16)
        qit_ref[0, h * IDX_DIM + half:(h + 1) * IDX_DIM, :] = (x2 * ct32 + x1 * st32).astype(BF16)

    vt_ref[0] = zt[TR_V:TR_V + KV_WIDTH].astype(BF16)
    wtt_ref[0] = zt[TR_W:TR_W + IDX_HEADS] * IDX_SCALE


def _rope_tables(seq):
    pos = jnp.arange(seq, dtype=F32)

    def cs(half):
        inv = ROPE_THETA ** (-jnp.arange(half, dtype=F32) / half)
        ang = pos[:, None] * inv[None, :]
        return jnp.cos(ang), jnp.sin(ang)

    c64, s64 = cs(HEAD_DIM // 2)
    c32, s32 = cs(IDX_DIM // 2)
    z = jnp.zeros((seq, LANES - IDX_DIM), F32)
    return dict(
        cn=jnp.concatenate([c64, c64], -1), sn=jnp.concatenate([-s64, s64], -1),
        cki=jnp.concatenate([c32, c32, z], -1), ski=jnp.concatenate([-s32, s32, z], -1),
        ct=c64.T, st=s64.T, ct32=c32.T, st32=s32.T)


def _split_w_in(w_in):
    o = IN_OFFSETS
    ssm, q, k, v, qi, ki, wi, gs, ga = [w_in[:, o[i]:o[i + 1]] for i in range(9)]
    zpad = jnp.zeros((D_MODEL, LANES - IDX_DIM), w_in.dtype)
    wn = jnp.concatenate([ssm, k, ki, zpad, gs, ga], axis=1).astype(BF16)
    wt = jnp.concatenate([q, v, qi, wi, jnp.zeros((D_MODEL, TR_ROWS - TR_W - IDX_HEADS), w_in.dtype)],
                         axis=1).T.astype(BF16)
    return wn, wt


def _inproj(x, sc, sh, wn, wt, tabs, tm):
    bsz, seq, d = x.shape
    grid = (bsz, seq // tm)
    tok = lambda w: pl.BlockSpec((1, tm, w), lambda b, i: (b, i, 0))
    trn = lambda r: pl.BlockSpec((1, r, tm), lambda b, i: (b, 0, i))
    vec = pl.BlockSpec((1, 1, d), lambda b, i: (b, 0, 0))
    const = lambda a: pl.BlockSpec(a.shape, lambda b, i: (0, 0))
    ntab = lambda: pl.BlockSpec((tm, LANES), lambda b, i: (i, 0))
    ttab = lambda r: pl.BlockSpec((r, tm), lambda b, i: (0, i))
    out_shape = (
        jax.ShapeDtypeStruct((bsz, seq, SSM_WIDTH), BF16),
        jax.ShapeDtypeStruct((bsz, seq, KV_WIDTH), BF16),
        jax.ShapeDtypeStruct((bsz, seq, IDX_DIM), BF16),
        jax.ShapeDtypeStruct((bsz, seq, 2 * D_MODEL), BF16),
        jax.ShapeDtypeStruct((bsz, ATTN_WIDTH, seq), BF16),
        jax.ShapeDtypeStruct((bsz, KV_WIDTH, seq), BF16),
        jax.ShapeDtypeStruct((bsz, IDX_WIDTH, seq), BF16),
        jax.ShapeDtypeStruct((bsz, IDX_HEADS, seq), F32),
    )
    return pl.pallas_call(
        _inproj_kernel,
        out_shape=out_shape,
        grid=grid,
        in_specs=[tok(d), vec, vec, const(wn), const(wt), ntab(), ntab(), ntab(), ntab(),
                  ttab(HEAD_DIM // 2), ttab(HEAD_DIM // 2), ttab(IDX_DIM // 2), ttab(IDX_DIM // 2)],
        out_specs=(tok(SSM_WIDTH), tok(KV_WIDTH), tok(IDX_DIM), tok(2 * D_MODEL),
                   trn(ATTN_WIDTH), trn(KV_WIDTH), trn(IDX_WIDTH), trn(IDX_HEADS)),
        compiler_params=_cparams("arbitrary", "arbitrary"),
        name="in_proj",
    )(x, sc, sh, wn, wt, tabs["cn"], tabs["sn"], tabs["cki"], tabs["ski"],
      tabs["ct"], tabs["st"], tabs["ct32"], tabs["st32"])


def _s5_kernel(u_ref, lrr_ref, lir_ref, ldr_ref, lrc_ref, lic_ref, ldc_ref,
               brt_ref, bit_ref, cr_ref, ci_ref, crt_ref, cit_ref, dv_ref,
               y_ref, w_scr, kv_scr, *, bsz):
    t = S5_CHUNK
    p = SSM_STATE
    gsz = SSM_GROUP
    lrr, lir = lrr_ref[0], lir_ref[0]
    dtr = jnp.exp(ldr_ref[0])
    lrc, lic = lrc_ref[0], lic_ref[0]
    dtc = jnp.exp(ldc_ref[0])

    mag = jnp.exp(lrr * dtr)
    ar, ai = mag * jnp.cos(lir * dtr), mag * jnp.sin(lir * dtr)
    den = lrr * lrr + lir * lir
    nr = ar - 1.0
    fr, fi = (nr * lrr + ai * lir) / den, (ai * lrr - nr * lir) / den
    brt, bit = brt_ref[0], bit_ref[0]
    bbr, bbi = fr * brt - fi * bit, fr * bit + fi * brt

    cr, ci = cr_ref[0], ci_ref[0]
    bbr_h, bbi_h = bbr[:, :p], bbi[:, :p]
    cbr = jnp.concatenate([cr[i:i + 1] * bbr_h - ci[i:i + 1] * bbi_h for i in range(gsz)], axis=0)
    cbi = jnp.concatenate([cr[i:i + 1] * bbi_h + ci[i:i + 1] * bbr_h for i in range(gsz)], axis=0)
    lag = lax.broadcasted_iota(I32, (p, t), 1).astype(F32)
    lrc_h, lic_h, dtc_h = lrc[:p], lic[:p], dtc[:p]
    pmag = jnp.exp(lrc_h * dtc_h * lag)
    pth = lic_h * dtc_h * lag
    kv_scr[...] = (jnp.dot(cbr, pmag * jnp.cos(pth), preferred_element_type=F32, precision=HIGHEST)
                   - jnp.dot(cbi, pmag * jnp.sin(pth), preferred_element_type=F32, precision=HIGHEST))

    row = lax.broadcasted_iota(I32, (t, t), 0)
    col = lax.broadcasted_iota(I32, (t, t), 1)
    causal = col >= row

    def toeplitz_cols(i, carry):
        c0 = pl.multiple_of(i * t, t)
        for k in range(gsz):
            vec = kv_scr[pl.ds(i * gsz + k, 1), :]
            blk = pltpu.roll(jnp.broadcast_to(vec, (t, t)), 0, 1, stride=1, stride_axis=0)
            w_scr[k * t:(k + 1) * t, pl.ds(c0, t)] = jnp.where(causal, blk, 0.0).astype(BF16)
        return carry

    lax.fori_loop(0, gsz, toeplitz_cols, 0)

    lo = lax.broadcasted_iota(I32, (t, 2 * p), 1) < p
    rem = (t - 1 - lax.broadcasted_iota(I32, (t, 2 * p), 0)).astype(F32)
    emag = jnp.exp(lrr * dtr * rem)
    eth = lir * dtr * rem
    ecs, esn = emag * jnp.cos(eth), emag * jnp.sin(eth)
    e_same, e_swap = jnp.where(lo, ecs, esn), jnp.where(lo, esn, ecs)
    sgn = jnp.where(lo[:1], -1.0, 1.0)
    wb = jnp.concatenate([e_same * bbr[k:k + 1] + e_swap * (bbi[k:k + 1] * sgn) for k in range(gsz)],
                         axis=0).astype(BF16)

    top = lax.broadcasted_iota(I32, (2 * p, t), 0) < p
    step = (lax.broadcasted_iota(I32, (2 * p, t), 1) + 1).astype(F32)
    cmag = jnp.exp(lrc * dtc * step)
    cth = lic * dtc * step
    mc, ms = cmag * jnp.cos(cth), cmag * jnp.sin(cth)
    crt, cit = crt_ref[0], cit_ref[0]
    wc = jnp.concatenate(
        [jnp.where(top, crt[:, i:i + 1] * mc - cit[:, i:i + 1] * ms,
                   -(crt[:, i:i + 1] * ms + cit[:, i:i + 1] * mc)) for i in range(gsz)],
        axis=1).astype(BF16)

    u = u_ref[0]
    y = jnp.dot(u, w_scr[...], preferred_element_type=F32)
    g = jnp.dot(u, wb, preferred_element_type=F32)
    gr, gi = g[:, :p], g[:, p:]

    m_t = jnp.exp(lrr[:, :p] * dtr[:, :p] * float(t))
    at_r = m_t * jnp.cos(lir[:, :p] * dtr[:, :p] * float(t))
    at_i = m_t * jnp.sin(lir[:, :p] * dtr[:, :p] * float(t))
    nc = u.shape[0] // bsz
    hr = jnp.zeros((bsz, p), F32)
    hi = jnp.zeros((bsz, p), F32)
    hrs, his = [hr], [hi]
    for c in range(1, nc):
        sl = slice((c - 1) * bsz, c * bsz)
        hr, hi = at_r * hr - at_i * hi + gr[sl], at_r * hi + at_i * hr + gi[sl]
        hrs.append(hr)
        his.append(hi)
    hcat = jnp.concatenate([jnp.concatenate(hrs, axis=0), jnp.concatenate(his, axis=0)], axis=1)
    y = y + jnp.dot(hcat.astype(BF16), wc, preferred_element_type=F32)
    y_ref[0] = (y + dv_ref[0] * u.astype(F32)).astype(BF16)


def _s5(u_ssm, lam_re, lam_im, log_dt, b_re, b_im, c_re, c_im, d_skip):
    bsz, seq, _ = u_ssm.shape
    t, g, gs, p = S5_CHUNK, SSM_GROUPS, SSM_GROUP, SSM_STATE
    nc = seq // t
    rows = nc * bsz
    ug = u_ssm.reshape(bsz, nc, t, g, gs).transpose(3, 1, 0, 4, 2).reshape(g, rows, gs * t)

    dup_r = lambda a: jnp.concatenate([a, a], axis=-1)[:, None, :]
    dup_c = lambda a: jnp.concatenate([a, a], axis=-1)[:, :, None]
    ld = jnp.broadcast_to(log_dt[:, None], (g, p))
    brt = jnp.swapaxes(b_re, 1, 2)
    bit = jnp.swapaxes(b_im, 1, 2)
    crt = jnp.swapaxes(c_re, 1, 2)
    cit = jnp.swapaxes(c_im, 1, 2)
    args = (
        ug,
        dup_r(lam_re), dup_r(lam_im), dup_r(ld), dup_c(lam_re), dup_c(lam_im), dup_c(ld),
        jnp.concatenate([brt, brt], -1), jnp.concatenate([bit, bit], -1),
        c_re, c_im, jnp.concatenate([crt, crt], 1), jnp.concatenate([cit, cit], 1),
        jnp.repeat(d_skip.reshape(g, gs), t, axis=1)[:, None, :],
    )
    per_g = lambda a: pl.BlockSpec((1,) + a.shape[1:], lambda i: (i,) + (0,) * (a.ndim - 1))
    yg = pl.pallas_call(
        functools.partial(_s5_kernel, bsz=bsz),
        out_shape=jax.ShapeDtypeStruct((g, rows, gs * t), BF16),
        grid=(g,),
        in_specs=[per_g(a) for a in args],
        out_specs=pl.BlockSpec((1, rows, gs * t), lambda i: (i, 0, 0)),
        scratch_shapes=[pltpu.VMEM((gs * t, gs * t), BF16), pltpu.VMEM((gs * gs, t), F32)],
        compiler_params=_cparams("arbitrary"),
        name="s5",
    )(*args)
    return yg.reshape(g, nc, bsz, gs, t).transpose(2, 1, 4, 0, 3).reshape(bsz, seq, g * gs)


def _dsa_kernel(qt_ref, qit_ref, wt_ref, k_ref, vt_ref, ki_ref, y_ref,
                key_scr, m_scr, l_scr, acc_scr, alpha_scr, cut_scr, lg_scr, p_scr, *, n_sel, idx_bits):
    tq, tk = Q_TILE, K_TILE
    qi = pl.program_id(1)
    nkt = qi + 1
    q0 = qi * tq
    diff = (lax.broadcasted_iota(I32, (tk, tq), 0) - lax.broadcasted_iota(I32, (tk, tq), 1))

    def score_tile(kt, carry):
        k0 = pl.multiple_of(kt * tk, tk)
        ki_t = ki_ref[0, pl.ds(k0, tk), :]
        s = jnp.zeros((tk, tq), F32)
        for h in range(IDX_HEADS):
            rel = jnp.dot(ki_t, qit_ref[0, h * IDX_DIM:(h + 1) * IDX_DIM, :], preferred_element_type=F32)
            s = s + jnp.maximum(rel, 0.0) * wt_ref[0, h:h + 1, :]
        bits = lax.bitcast_convert_type(s, I32)
        key = bits ^ ((bits >> 31) & 0x7FFFFFFF)
        key = jnp.where(s == 0.0, 0, key)
        key = jnp.where(diff <= q0 - k0, key, INT_MIN)
        key_scr[pl.ds(k0, tk), :] = key
        return carry

    lax.fori_loop(0, nkt, score_tile, 0)

    def count(pred):
        def body(kt, acc):
            k0 = pl.multiple_of(kt * tk, tk)
            hit = jnp.where(pred(key_scr[pl.ds(k0, tk), :], k0), 1, 0)
            return acc + jnp.sum(hit.reshape(tk // SUBLANES, SUBLANES, tq), axis=0)
        part = lax.fori_loop(0, nkt, body, jnp.zeros((SUBLANES, tq), I32))
        return jnp.sum(part, axis=0, keepdims=True)

    def bit_step(i, c_u):
        trial = c_u | (jnp.int32(1) << (31 - i))
        trial_s = trial ^ INT_MIN
        cnt = count(lambda kt_keys, k0: kt_keys >= trial_s)
        return jnp.where(cnt >= n_sel, trial, c_u)

    c_u = lax.fori_loop(0, 32, bit_step, jnp.zeros((1, tq), I32))
    thr = c_u ^ INT_MIN
    cnt_gt = count(lambda kt_keys, k0: kt_keys > thr)
    cnt_ge = count(lambda kt_keys, k0: kt_keys >= thr)

    sub = lax.broadcasted_iota(I32, (tk, tq), 0)
    cut_scr[...] = jnp.full(cut_scr.shape, 2 ** 31 - 1, I32)

    @pl.when(jnp.max(cnt_ge) > n_sel)
    def _():
        need = n_sel - cnt_gt

        def idx_step(i, j_cut):
            trial = j_cut | (jnp.int32(1) << (idx_bits - 1 - i))
            cnt = count(lambda kt_keys, k0: (kt_keys == thr) & (sub + k0 < trial))
            return jnp.where(cnt <= need, trial, j_cut)

        cut_scr[...] = lax.fori_loop(0, idx_bits, idx_step, jnp.zeros((1, tq), I32))

    thr_sel = jnp.maximum(thr, INT_MIN + 1)
    j_cut = cut_scr[...]

    def to_bias(kt, carry):
        k0 = pl.multiple_of(kt * tk, tk)
        keys = key_scr[pl.ds(k0, tk), :]
        sel = (keys > thr_sel) | ((keys == thr_sel) & (sub + k0 < j_cut))
        key_scr[pl.ds(k0, tk), :] = lax.bitcast_convert_type(jnp.where(sel, 0.0, NEG_BIG), I32)
        return carry

    lax.fori_loop(0, nkt, to_bias, 0)

    m_scr[...] = jnp.full(m_scr.shape, NEG_BIG, F32)
    l_scr[...] = jnp.zeros(l_scr.shape, F32)
    acc_scr[...] = jnp.zeros(acc_scr.shape, F32)

    def attend(kt, carry):
        k0 = pl.multiple_of(kt * tk, tk)
        for h in range(N_HEADS):
            g = h // KV_GROUP
            k_t = k_ref[0, pl.ds(k0, tk), g * HEAD_DIM:(g + 1) * HEAD_DIM]
            lg = (jnp.dot(k_t, qt_ref[0, h * HEAD_DIM:(h + 1) * HEAD_DIM, :], preferred_element_type=F32)
                  + lax.bitcast_convert_type(key_scr[pl.ds(k0, tk), :], F32))
            lg_scr[h] = lg
            m_old = m_scr[h]
            m_new = jnp.maximum(m_old, jnp.max(lg, axis=0, keepdims=True))
            alpha_scr[h] = jnp.exp2(m_old - m_new)
            m_scr[h] = m_new
        for h in range(N_HEADS):
            pr = jnp.exp2(lg_scr[h] - m_scr[h])
            l_scr[h] = alpha_scr[h] * l_scr[h] + jnp.sum(pr, axis=0, keepdims=True)
            p_scr[h] = pr.astype(BF16)
        for h in range(N_HEADS):
            g = h // KV_GROUP
            vt_t = vt_ref[0, g * HEAD_DIM:(g + 1) * HEAD_DIM, pl.ds(k0, tk)]
            acc_scr[h] = alpha_scr[h] * acc_scr[h] + jnp.dot(vt_t, p_scr[h], preferred_element_type=F32)
        return carry

    lax.fori_loop(0, nkt, attend, 0)

    for h in range(N_HEADS):
        o = acc_scr[h] / l_scr[h]
        y_ref[0, :, h * HEAD_DIM:(h + 1) * HEAD_DIM] = o.T.astype(BF16)


def _dsa(qt, qit, wtt, k, vt, ki):
    bsz, seq, _ = k.shape
    n_sel = min(TOPK_MAX, seq // 4)
    tq = Q_TILE
    kern = functools.partial(_dsa_kernel, n_sel=n_sel, idx_bits=seq.bit_length())
    return pl.pallas_call(
        kern,
        out_shape=jax.ShapeDtypeStruct((bsz, seq, ATTN_WIDTH), BF16),
        grid=(bsz, seq // tq),
        in_specs=[
            pl.BlockSpec((1, ATTN_WIDTH, tq), lambda b, i: (b, 0, i)),
            pl.BlockSpec((1, IDX_WIDTH, tq), lambda b, i: (b, 0, i)),
            pl.BlockSpec((1, IDX_HEADS, tq), lambda b, i: (b, 0, i)),
            pl.BlockSpec((1, seq, KV_WIDTH), lambda b, i: (b, 0, 0)),
            pl.BlockSpec((1, KV_WIDTH, seq), lambda b, i: (b, 0, 0)),
            pl.BlockSpec((1, seq, IDX_DIM), lambda b, i: (b, 0, 0)),
        ],
        out_specs=pl.BlockSpec((1, tq, ATTN_WIDTH), lambda b, i: (b, i, 0)),
        scratch_shapes=[
            pltpu.VMEM((seq, tq), I32),
            pltpu.VMEM((N_HEADS, 1, tq), F32),
            pltpu.VMEM((N_HEADS, 1, tq), F32),
            pltpu.VMEM((N_HEADS, HEAD_DIM, tq), F32),
            pltpu.VMEM((N_HEADS, 1, tq), F32),
            pltpu.VMEM((1, tq), I32),
            pltpu.VMEM((N_HEADS, K_TILE, tq), F32),
            pltpu.VMEM((N_HEADS, K_TILE, tq), BF16),
        ],
        compiler_params=_cparams("arbitrary", "arbitrary"),
        name="dsa",
    )(qt, qit, wtt, k, vt, ki)


def _deepnorm_ln(x, h, gate, g, b):
    y = DEEPNORM_ALPHA * x + (1.0 + gate) * h
    mu = jnp.mean(y, axis=-1, keepdims=True)
    yc = y - mu
    var = jnp.mean(yc * yc, axis=-1, keepdims=True)
    return yc * lax.rsqrt(var + LN_EPS) * g + b


def _merge_kernel(x_ref, ys_ref, ya_ref, sg_ref, gt_ref, wglu_ref, bglu_ref, pssm_ref, pattn_ref,
                  wout_ref, lng_ref, lnb_ref, o_ref):
    y = jax.nn.gelu(ys_ref[0].astype(F32))
    glu = jnp.dot(y.astype(BF16), wglu_ref[...], preferred_element_type=F32) + bglu_ref[...]
    y = (y * jax.nn.sigmoid(glu)).astype(BF16)
    sg = sg_ref[0]
    merged = (sg[:, :D_MODEL].astype(F32) * jnp.dot(y, pssm_ref[...], preferred_element_type=F32)
              + sg[:, D_MODEL:].astype(F32) * jnp.dot(ya_ref[0], pattn_ref[...], preferred_element_type=F32))
    h = jnp.dot(merged.astype(BF16), wout_ref[...], preferred_element_type=F32)
    o_ref[0] = _deepnorm_ln(x_ref[0], h, gt_ref[0], lng_ref[...], lnb_ref[...])


def _merge(x, ys, ya, sg, gt, wglu, bglu, pssm, pattn, wout, lng, lnb, tm):
    bsz, seq, d = x.shape
    tok = lambda w: pl.BlockSpec((1, tm, w), lambda b, i: (b, i, 0))
    vec = pl.BlockSpec((1, 1, d), lambda b, i: (b, 0, 0))
    const = lambda a: pl.BlockSpec(a.shape, lambda b, i: (0, 0))
    return pl.pallas_call(
        _merge_kernel,
        out_shape=jax.ShapeDtypeStruct(x.shape, F32),
        grid=(bsz, seq // tm),
        in_specs=[tok(d), tok(SSM_WIDTH), tok(ATTN_WIDTH), tok(2 * D_MODEL), vec,
                  const(wglu), const(bglu), const(pssm), const(pattn), const(wout), const(lng), const(lnb)],
        out_specs=tok(d),
        compiler_params=_cparams("arbitrary", "arbitrary"),
        name="merge",
    )(x, ys, ya, sg, gt, wglu, bglu, pssm, pattn, wout, lng, lnb)


def _ffn_kernel(x_ref, sc_ref, sh_ref, gt_ref, wgu_ref, wd_ref, lng_ref, lnb_ref, o_ref, hid_scr):
    x = x_ref[0]
    u = (x * (1.0 + sc_ref[0]) + sh_ref[0]).astype(BF16)
    for j in range(D_FF // FF_CHUNK):
        a = jnp.dot(u, wgu_ref[:, j * FF_CHUNK:(j + 1) * FF_CHUNK], preferred_element_type=F32)
        b = jnp.dot(u, wgu_ref[:, D_FF + j * FF_CHUNK:D_FF + (j + 1) * FF_CHUNK], preferred_element_type=F32)
        hid_scr[:, j * FF_CHUNK:(j + 1) * FF_CHUNK] = (a * jax.nn.sigmoid(a) * b).astype(BF16)
    f = jnp.dot(hid_scr[...], wd_ref[...], preferred_element_type=F32)
    o_ref[0] = _deepnorm_ln(x, f, gt_ref[0], lng_ref[...], lnb_ref[...])


def _ffn(x, sc, sh, gt, wgu, wd, lng, lnb, tm):
    bsz, seq, d = x.shape
    tok = pl.BlockSpec((1, tm, d), lambda b, i: (b, i, 0))
    vec = pl.BlockSpec((1, 1, d), lambda b, i: (b, 0, 0))
    const = lambda a: pl.BlockSpec(a.shape, lambda b, i: (0, 0))
    return pl.pallas_call(
        _ffn_kernel,
        out_shape=jax.ShapeDtypeStruct(x.shape, F32),
        grid=(bsz, seq // tm),
        in_specs=[tok, vec, vec, vec, const(wgu), const(wd), const(lng), const(lnb)],
        out_specs=tok,
        scratch_shapes=[pltpu.VMEM((tm, D_FF), BF16)],
        compiler_params=_cparams("arbitrary", "arbitrary"),
        name="ffn",
    )(x, sc, sh, gt, wgu, wd, lng, lnb)


def kernel(x, c, w_cond, b_cond, w_in, ssm_lam_re, ssm_lam_im, ssm_log_dt, ssm_b_re, ssm_b_im,
           ssm_c_re, ssm_c_im, ssm_d, ssm_w_glu, ssm_b_glu, p_ssm, p_attn, w_out,
           ln1_g, ln1_b, w_gate_up, w_down, ln2_g, ln2_b):
    bsz, seq, d = x.shape
    assert d == D_MODEL and seq % Q_TILE == 0 and seq % S5_CHUNK == 0
    tm = min(512, seq)
    tabs = _rope_tables(seq)
    mod = _cond(c, w_cond, b_cond)
    row = lambda a: a.reshape(1, -1)
    for l in range(DEPTH):
        sh1, sc1, gt1, sh2, sc2, gt2 = [mod[l, :, i * d:(i + 1) * d][:, None, :] for i in range(6)]
        wn, wt = _split_w_in(w_in[l])
        u_ssm, k, ki, sg, qt, vt, qit, wtt = _inproj(x, sc1, sh1, wn, wt, tabs, tm)
        ys = _s5(u_ssm, ssm_lam_re[l], ssm_lam_im[l], ssm_log_dt[l], ssm_b_re[l], ssm_b_im[l],
                 ssm_c_re[l], ssm_c_im[l], ssm_d[l])
        ya = _dsa(qt, qit, wtt, k, vt, ki)
        x = _merge(x, ys, ya, sg, gt1, ssm_w_glu[l].astype(BF16), row(ssm_b_glu[l]),
                   p_ssm[l].astype(BF16), p_attn[l].astype(BF16), w_out[l].astype(BF16),
                   row(ln1_g[l]), row(ln1_b[l]), tm)
        x = _ffn(x, sc2, sh2, gt2, w_gate_up[l].astype(BF16), w_down[l].astype(BF16),
                 row(ln2_g[l]), row(ln2_b[l]), tm)
    return x
```

```python
import functools
import math

import jax
import jax.numpy as jnp
import numpy as np
from jax import lax
from jax.experimental import pallas as pl
from jax.experimental.pallas import tpu as pltpu

F32 = jnp.float32
BF16 = jnp.bfloat16
I32 = jnp.int32
I16 = jnp.int16
HIGHEST = lax.Precision.HIGHEST

D_MODEL = 1024
DEPTH = 2
SSM_WIDTH = 512
SSM_GROUP = 16
SSM_GROUPS = SSM_WIDTH // SSM_GROUP
SSM_STATE = 64
HEAD_DIM = 128
N_HEADS = D_MODEL // HEAD_DIM
N_KV_HEADS = 2
KV_GROUP = N_HEADS // N_KV_HEADS
ATTN_WIDTH = N_HEADS * HEAD_DIM
KV_WIDTH = N_KV_HEADS * HEAD_DIM
IDX_HEADS = 8
IDX_DIM = 64
IDX_WIDTH = IDX_HEADS * IDX_DIM
IDX_SCALE = (IDX_HEADS * IDX_DIM) ** -0.5
TOPK_MAX = 256
ROPE_THETA = 10000.0
D_FF = -(-8 * D_MODEL // (3 * 256)) * 256
DEEPNORM_ALPHA = (2 * DEPTH) ** 0.25
LN_EPS = 1e-5
IN_SIZES = (SSM_WIDTH, ATTN_WIDTH, KV_WIDTH, KV_WIDTH, IDX_WIDTH, IDX_DIM, IDX_HEADS, D_MODEL, D_MODEL)
IN_OFFSETS = [0] + [int(o) for o in np.cumsum(IN_SIZES)]

LANES = 128
SUBLANES = 8
VMEM_LIMIT_BYTES = 56 * 1024 * 1024

S5_CHUNK = LANES
Q_TILE = 256
K_TILE = 256
FF_CHUNK = 256
INT_MIN = -(2 ** 31)
NEG_BIG = -1e30
HALF16 = 2 ** 15
PACK16 = 2 * SUBLANES
Q_SCALE = HEAD_DIM ** -0.5 * math.log2(math.e)


def _cparams(*sem):
    return pltpu.CompilerParams(dimension_semantics=sem, vmem_limit_bytes=VMEM_LIMIT_BYTES)


def _cond_kernel(c_ref, w_ref, b_ref, o_ref):
    c = c_ref[...]
    s = c * jax.nn.sigmoid(c)
    o_ref[0] = jnp.dot(s, w_ref[0], preferred_element_type=F32, precision=HIGHEST) + b_ref[0]


def _cond(c, w_cond, b_cond):
    depth, d, n = w_cond.shape
    bsz = c.shape[0]
    tn = 1536
    return pl.pallas_call(
        _cond_kernel,
        out_shape=jax.ShapeDtypeStruct((depth, bsz, n), F32),
        grid=(depth, n // tn),
        in_specs=[
            pl.BlockSpec((bsz, d), lambda l, j: (0, 0)),
            pl.BlockSpec((1, d, tn), lambda l, j: (l, 0, j)),
            pl.BlockSpec((1, 1, tn), lambda l, j: (l, 0, j)),
        ],
        out_specs=pl.BlockSpec((1, bsz, tn), lambda l, j: (l, 0, j)),
        compiler_params=_cparams("arbitrary", "arbitrary"),
        name="cond",
    )(c, w_cond, b_cond.reshape(depth, 1, n))


NAT_SSM, NAT_K, NAT_KI, NAT_G = 0, 512, 768, 896
NAT_WIDTH = 2944
TR_Q, TR_V, TR_QI, TR_W = 0, 1024, 1280, 1792
TR_ROWS = 1808


def _inproj_kernel(x_ref, sc_ref, sh_ref, wn_ref, wt_ref, cn_ref, sn_ref, cki_ref, ski_ref,
                   ct_ref, st_ref, ct32_ref, st32_ref,
                   ssm_ref, k_ref, ki_ref, sg_ref, qt_ref, vt_ref, qit_ref, wtt_ref):
    u = (x_ref[0] * (1.0 + sc_ref[0]) + sh_ref[0]).astype(BF16)
    zn = jnp.dot(u, wn_ref[...], preferred_element_type=F32)
    zt = lax.dot_general(wt_ref[...], u, (((1,), (1,)), ((), ())),
                         preferred_element_type=F32)

    ssm_ref[0] = zn[:, NAT_SSM:NAT_SSM + SSM_WIDTH].astype(BF16)
    sg_ref[0] = jax.nn.sigmoid(zn[:, NAT_G:NAT_G + 2 * D_MODEL]).astype(BF16)

    cn, sn = cn_ref[...], sn_ref[...]
    for h in range(N_KV_HEADS):
        xh = zn[:, NAT_K + h * HEAD_DIM:NAT_K + (h + 1) * HEAD_DIM]
        rot = pltpu.roll(xh, HEAD_DIM // 2, axis=1)
        k_ref[0, :, h * HEAD_DIM:(h + 1) * HEAD_DIM] = (xh * cn + rot * sn).astype(BF16)

    xk = zn[:, NAT_KI:NAT_KI + LANES]
    lane = lax.broadcasted_iota(I32, xk.shape, 1)
    rot = jnp.where((lane & (IDX_DIM // 2)) == 0,
                    pltpu.roll(xk, LANES - IDX_DIM // 2, axis=1),
                    pltpu.roll(xk, IDX_DIM // 2, axis=1))
    ki_ref[0] = (xk * cki_ref[...] + rot * ski_ref[...])[:, :IDX_DIM].astype(BF16)

    ct, st = ct_ref[...], st_ref[...]
    half = HEAD_DIM // 2
    for h in range(N_HEADS):
        r0 = TR_Q + h * HEAD_DIM
        x1, x2 = zt[r0:r0 + half], zt[r0 + half:r0 + HEAD_DIM]
        qt_ref[0, h * HEAD_DIM:h * HEAD_DIM + half, :] = ((x1 * ct - x2 * st) * Q_SCALE).astype(BF16)
        qt_ref[0, h * HEAD_DIM + half:(h + 1) * HEAD_DIM, :] = ((x2 * ct + x1 * st) * Q_SCALE).astype(BF16)
    ct32, st32 = ct32_ref[...], st32_ref[...]
    half = IDX_DIM // 2
    for h in range(IDX_HEADS):
        r0 = TR_QI + h * IDX_DIM
        x1, x2 = zt[r0:r0 + half], zt[r0 + half:r0 + IDX_DIM]
        qit_ref[0, h * IDX_DIM:h * IDX_DIM + half, :] = (x1 * ct32 - x2 * st32).astype(BF16)
        qit_ref[0, h * IDX_DIM + half:(h + 1) * IDX_DIM, :] = (x2 * ct32 + x1 * st32).astype(BF16)

    vt_ref[0] = zt[TR_V:TR_V + KV_WIDTH].astype(BF16)
    wtt_ref[0] = zt[TR_W:TR_W + IDX_HEADS] * IDX_SCALE


def _rope_tables(seq):
    pos = jnp.arange(seq, dtype=F32)

    def cs(half):
        inv = ROPE_THETA ** (-jnp.arange(half, dtype=F32) / half)
        ang = pos[:, None] * inv[None, :]
        return jnp.cos(ang), jnp.sin(ang)

    c64, s64 = cs(HEAD_DIM // 2)
    c32, s32 = cs(IDX_DIM // 2)
    z = jnp.zeros((seq, LANES - IDX_DIM), F32)
    return dict(
        cn=jnp.concatenate([c64, c64], -1), sn=jnp.concatenate([-s64, s64], -1),
        cki=jnp.concatenate([c32, c32, z], -1), ski=jnp.concatenate([-s32, s32, z], -1),
        ct=c64.T, st=s64.T, ct32=c32.T, st32=s32.T)


def _split_w_in(w_in):
    o = IN_OFFSETS
    ssm, q, k, v, qi, ki, wi, gs, ga = [w_in[:, o[i]:o[i + 1]] for i in range(9)]
    zpad = jnp.zeros((D_MODEL, LANES - IDX_DIM), w_in.dtype)
    wn = jnp.concatenate([ssm, k, ki, zpad, gs, ga], axis=1).astype(BF16)
    wt = jnp.concatenate([q, v, qi, wi, jnp.zeros((D_MODEL, TR_ROWS - TR_W - IDX_HEADS), w_in.dtype)],
                         axis=1).T.astype(BF16)
    return wn, wt


def _inproj(x, sc, sh, wn, wt, tabs, tm):
    bsz, seq, d = x.shape
    grid = (bsz, seq // tm)
    tok = lambda w: pl.BlockSpec((1, tm, w), lambda b, i: (b, i, 0))
    trn = lambda r: pl.BlockSpec((1, r, tm), lambda b, i: (b, 0, i))
    vec = pl.BlockSpec((1, 1, d), lambda b, i: (b, 0, 0))
    const = lambda a: pl.BlockSpec(a.shape, lambda b, i: (0, 0))
    ntab = lambda: pl.BlockSpec((tm, LANES), lambda b, i: (i, 0))
    ttab = lambda r: pl.BlockSpec((r, tm), lambda b, i: (0, i))
    out_shape = (
        jax.ShapeDtypeStruct((bsz, seq, SSM_WIDTH), BF16),
        jax.ShapeDtypeStruct((bsz, seq, KV_WIDTH), BF16),
        jax.ShapeDtypeStruct((bsz, seq, IDX_DIM), BF16),
        jax.ShapeDtypeStruct((bsz, seq, 2 * D_MODEL), BF16),
        jax.ShapeDtypeStruct((bsz, ATTN_WIDTH, seq), BF16),
        jax.ShapeDtypeStruct((bsz, KV_WIDTH, seq), BF16),
        jax.ShapeDtypeStruct((bsz, IDX_WIDTH, seq), BF16),
        jax.ShapeDtypeStruct((bsz, IDX_HEADS, seq), F32),
    )
    return pl.pallas_call(
        _inproj_kernel,
        out_shape=out_shape,
        grid=grid,
        in_specs=[tok(d), vec, vec, const(wn), const(wt), ntab(), ntab(), ntab(), ntab(),
                  ttab(HEAD_DIM // 2), ttab(HEAD_DIM // 2), ttab(IDX_DIM // 2), ttab(IDX_DIM // 2)],
        out_specs=(tok(SSM_WIDTH), tok(KV_WIDTH), tok(IDX_DIM), tok(2 * D_MODEL),
                   trn(ATTN_WIDTH), trn(KV_WIDTH), trn(IDX_WIDTH), trn(IDX_HEADS)),
        compiler_params=_cparams("arbitrary", "arbitrary"),
        name="in_proj",
    )(x, sc, sh, wn, wt, tabs["cn"], tabs["sn"], tabs["cki"], tabs["ski"],
      tabs["ct"], tabs["st"], tabs["ct32"], tabs["st32"])


def _s5_kernel(u_ref, lrr_ref, lir_ref, ldr_ref, lrc_ref, lic_ref, ldc_ref,
               brt_ref, bit_ref, cr_ref, ci_ref, crt_ref, cit_ref, dv_ref,
               y_ref, w_scr, kv_scr, *, bsz):
    t = S5_CHUNK
    p = SSM_STATE
    gsz = SSM_GROUP
    lrr, lir = lrr_ref[0], lir_ref[0]
    dtr = jnp.exp(ldr_ref[0])
    lrc, lic = lrc_ref[0], lic_ref[0]
    dtc = jnp.exp(ldc_ref[0])

    mag = jnp.exp(lrr * dtr)
    ar, ai = mag * jnp.cos(lir * dtr), mag * jnp.sin(lir * dtr)
    den = lrr * lrr + lir * lir
    nr = ar - 1.0
    fr, fi = (nr * lrr + ai * lir) / den, (ai * lrr - nr * lir) / den
    brt, bit = brt_ref[0], bit_ref[0]
    bbr, bbi = fr * brt - fi * bit, fr * bit + fi * brt

    cr, ci = cr_ref[0], ci_ref[0]
    bbr_h, bbi_h = bbr[:, :p], bbi[:, :p]
    cbr = jnp.concatenate([cr[i:i + 1] * bbr_h - ci[i:i + 1] * bbi_h for i in range(gsz)], axis=0)
    cbi = jnp.concatenate([cr[i:i + 1] * bbi_h + ci[i:i + 1] * bbr_h for i in range(gsz)], axis=0)
    lag = lax.broadcasted_iota(I32, (p, t), 1).astype(F32)
    lrc_h, lic_h, dtc_h = lrc[:p], lic[:p], dtc[:p]
    pmag = jnp.exp(lrc_h * dtc_h * lag)
    pth = lic_h * dtc_h * lag
    kv_scr[...] = (jnp.dot(cbr, pmag * jnp.cos(pth), preferred_element_type=F32, precision=HIGHEST)
                   - jnp.dot(cbi, pmag * jnp.sin(pth), preferred_element_type=F32, precision=HIGHEST))

    row = lax.broadcasted_iota(I32, (t, t), 0)
    col = lax.broadcasted_iota(I32, (t, t), 1)
    causal = col >= row

    def toeplitz_cols(i, carry):
        c0 = pl.multiple_of(i * t, t)
        for k in range(gsz):
            vec = kv_scr[pl.ds(i * gsz + k, 1), :]
            blk = pltpu.roll(jnp.broadcast_to(vec, (t, t)), 0, 1, stride=1, stride_axis=0)
            w_scr[k * t:(k + 1) * t, pl.ds(c0, t)] = jnp.where(causal, blk, 0.0).astype(BF16)
        return carry

    lax.fori_loop(0, gsz, toeplitz_cols, 0)

    lo = lax.broadcasted_iota(I32, (t, 2 * p), 1) < p
    rem = (t - 1 - lax.broadcasted_iota(I32, (t, 2 * p), 0)).astype(F32)
    emag = jnp.exp(lrr * dtr * rem)
    eth = lir * dtr * rem
    ecs, esn = emag * jnp.cos(eth), emag * jnp.sin(eth)
    e_same, e_swap = jnp.where(lo, ecs, esn), jnp.where(lo, esn, ecs)
    sgn = jnp.where(lo[:1], -1.0, 1.0)
    wb = jnp.concatenate([e_same * bbr[k:k + 1] + e_swap * (bbi[k:k + 1] * sgn) for k in range(gsz)],
                         axis=0).astype(BF16)

    top = lax.broadcasted_iota(I32, (2 * p, t), 0) < p
    step = (lax.broadcasted_iota(I32, (2 * p, t), 1) + 1).astype(F32)
    cmag = jnp.exp(lrc * dtc * step)
    cth = lic * dtc * step
    mc, ms = cmag * jnp.cos(cth), cmag * jnp.sin(cth)
    crt, cit = crt_ref[0], cit_ref[0]
    wc = jnp.concatenate(
        [jnp.where(top, crt[:, i:i + 1] * mc - cit[:, i:i + 1] * ms,
                   -(crt[:, i:i + 1] * ms + cit[:, i:i + 1] * mc)) for i in range(gsz)],
        axis=1).astype(BF16)

    u = u_ref[0]
    y = jnp.dot(u, w_scr[...], preferred_element_type=F32)
    g = jnp.dot(u, wb, preferred_element_type=F32)
    gr, gi = g[:, :p], g[:, p:]

    m_t = jnp.exp(lrr[:, :p] * dtr[:, :p] * float(t))
    at_r = m_t * jnp.cos(lir[:, :p] * dtr[:, :p] * float(t))
    at_i = m_t * jnp.sin(lir[:, :p] * dtr[:, :p] * float(t))
    nc = u.shape[0] // bsz
    hr = jnp.zeros((bsz, p), F32)
    hi = jnp.zeros((bsz, p), F32)
    hrs, his = [hr], [hi]
    for c in range(1, nc):
        sl = slice((c - 1) * bsz, c * bsz)
        hr, hi = at_r * hr - at_i * hi + gr[sl], at_r * hi + at_i * hr + gi[sl]
        hrs.append(hr)
        his.append(hi)
    hcat = jnp.concatenate([jnp.concatenate(hrs, axis=0), jnp.concatenate(his, axis=0)], axis=1)
    y = y + jnp.dot(hcat.astype(BF16), wc, preferred_element_type=F32)
    y_ref[0] = (y + dv_ref[0] * u.astype(F32)).astype(BF16)


def _s5(u_ssm, lam_re, lam_im, log_dt, b_re, b_im, c_re, c_im, d_skip):
    bsz, seq, _ = u_ssm.shape
    t, g, gs, p = S5_CHUNK, SSM_GROUPS, SSM_GROUP, SSM_STATE
    nc = seq // t
    rows = nc * bsz
    ug = u_ssm.reshape(bsz, nc, t, g, gs).transpose(3, 1, 0, 4, 2).reshape(g, rows, gs * t)

    dup_r = lambda a: jnp.concatenate([a, a], axis=-1)[:, None, :]
    dup_c = lambda a: jnp.concatenate([a, a], axis=-1)[:, :, None]
    ld = jnp.broadcast_to(log_dt[:, None], (g, p))
    brt = jnp.swapaxes(b_re, 1, 2)
    bit = jnp.swapaxes(b_im, 1, 2)
    crt = jnp.swapaxes(c_re, 1, 2)
    cit = jnp.swapaxes(c_im, 1, 2)
    args = (
        ug,
        dup_r(lam_re), dup_r(lam_im), dup_r(ld), dup_c(lam_re), dup_c(lam_im), dup_c(ld),
        jnp.concatenate([brt, brt], -1), jnp.concatenate([bit, bit], -1),
        c_re, c_im, jnp.concatenate([crt, crt], 1), jnp.concatenate([cit, cit], 1),
        jnp.repeat(d_skip.reshape(g, gs), t, axis=1)[:, None, :],
    )
    per_g = lambda a: pl.BlockSpec((1,) + a.shape[1:], lambda i: (i,) + (0,) * (a.ndim - 1))
    yg = pl.pallas_call(
        functools.partial(_s5_kernel, bsz=bsz),
        out_shape=jax.ShapeDtypeStruct((g, rows, gs * t), BF16),
        grid=(g,),
        in_specs=[per_g(a) for a in args],
        out_specs=pl.BlockSpec((1, rows, gs * t), lambda i: (i, 0, 0)),
        scratch_shapes=[pltpu.VMEM((gs * t, gs * t), BF16), pltpu.VMEM((gs * gs, t), F32)],
        compiler_params=_cparams("arbitrary"),
        name="s5",
    )(*args)
    return yg.reshape(g, nc, bsz, gs, t).transpose(2, 1, 4, 0, 3).reshape(bsz, seq, g * gs)


def _dsa_kernel(qt_ref, qit_ref, wt_ref, k_ref, vt_ref, ki_ref, y_ref,
                key_scr, hi_scr, lo_scr, m_scr, acc_scr, sacc_scr, s_scr, cut_scr, lg_scr, p_scr, *, n_sel, idx_bits):
    tq, tk = Q_TILE, K_TILE
    qi = pl.program_id(1)
    nkt = qi + 1
    q0 = qi * tq
    diff = (lax.broadcasted_iota(I32, (tk, tq), 0) - lax.broadcasted_iota(I32, (tk, tq), 1))

    def score_tile(kt, carry):
        k0 = pl.multiple_of(kt * tk, tk)
        ki_t = ki_ref[0, pl.ds(k0, tk), :]
        s = jnp.zeros((tk, tq), F32)
        for h in range(IDX_HEADS):
            rel = jnp.dot(ki_t, qit_ref[0, h * IDX_DIM:(h + 1) * IDX_DIM, :], preferred_element_type=F32)
            s = s + jnp.maximum(rel, 0.0) * wt_ref[0, h:h + 1, :]
        bits = lax.bitcast_convert_type(s, I32)
        key = bits ^ ((bits >> 31) & 0x7FFFFFFF)
        key = jnp.where(s == 0.0, 0, key)
        key = jnp.where(diff <= q0 - k0, key, INT_MIN)
        key_scr[pl.ds(k0, tk), :] = key
        hi_scr[pl.ds(k0, tk), :] = (key >> 16).astype(I16)
        lo_scr[pl.ds(k0, tk), :] = ((key & 0xFFFF) - HALF16).astype(I16)
        return carry

    lax.fori_loop(0, nkt, score_tile, 0)

    def count(pred):
        def body(kt, acc):
            k0 = pl.multiple_of(kt * tk, tk)
            hit = jnp.where(pred(key_scr[pl.ds(k0, tk), :], k0), 1, 0)
            return acc + jnp.sum(hit.reshape(tk // SUBLANES, SUBLANES, tq), axis=0)
        part = lax.fori_loop(0, nkt, body, jnp.zeros((SUBLANES, tq), I32))
        return jnp.sum(part, axis=0, keepdims=True)

    def count16(ref, pred):
        def body(kt, acc):
            k0 = pl.multiple_of(kt * tk, tk)
            hit = jnp.where(pred(ref[pl.ds(k0, tk), :]), jnp.int16(1), jnp.int16(0))
            parts = [hit[r:r + PACK16] for r in range(0, tk, PACK16)]
            while len(parts) > 1:
                parts = [a + b for a, b in zip(parts[::2], parts[1::2])]
            return acc + parts[0]
        part = lax.fori_loop(0, nkt, body, jnp.zeros((PACK16, tq), I16))
        return jnp.sum(part.astype(I32), axis=0, keepdims=True)

    def search16(ref, target):
        def bit_step(i, c_u):
            trial = c_u | (jnp.int32(1) << (15 - i))
            trial_s = (trial - HALF16).astype(I16)
            cnt = count16(ref, lambda v: v >= trial_s)
            return jnp.where(cnt >= target, trial, c_u)
        return lax.fori_loop(0, 16, bit_step, jnp.zeros((1, tq), I32))

    hi_thr = search16(hi_scr, n_sel) - HALF16
    hi_thr16 = hi_thr.astype(I16)
    cnt_above = count16(hi_scr, lambda v: v > hi_thr16)

    def keep_bucket(kt, carry):
        k0 = pl.multiple_of(kt * tk, tk)
        lo_scr[pl.ds(k0, tk), :] = jnp.where(hi_scr[pl.ds(k0, tk), :] == hi_thr16,
                                             lo_scr[pl.ds(k0, tk), :], jnp.int16(-HALF16))
        return carry

    lax.fori_loop(0, nkt, keep_bucket, 0)
    lo_thr = search16(lo_scr, n_sel - cnt_above)
    thr = hi_thr * (2 * HALF16) + lo_thr
    cnt_gt = count(lambda kt_keys, k0: kt_keys > thr)
    cnt_ge = count(lambda kt_keys, k0: kt_keys >= thr)

    sub = lax.broadcasted_iota(I32, (tk, tq), 0)
    cut_scr[...] = jnp.full(cut_scr.shape, 2 ** 31 - 1, I32)

    @pl.when(jnp.max(cnt_ge) > n_sel)
    def _():
        need = n_sel - cnt_gt

        def idx_step(i, j_cut):
            trial = j_cut | (jnp.int32(1) << (idx_bits - 1 - i))
            cnt = count(lambda kt_keys, k0: (kt_keys == thr) & (sub + k0 < trial))
            return jnp.where(cnt <= need, trial, j_cut)

        cut_scr[...] = lax.fori_loop(0, idx_bits, idx_step, jnp.zeros((1, tq), I32))

    thr_sel = jnp.maximum(thr, INT_MIN + 1)
    j_cut = cut_scr[...]

    def to_bias(kt, carry):
        k0 = pl.multiple_of(kt * tk, tk)
        keys = key_scr[pl.ds(k0, tk), :]
        sel = (keys > thr_sel) | ((keys == thr_sel) & (sub + k0 < j_cut))
        key_scr[pl.ds(k0, tk), :] = lax.bitcast_convert_type(jnp.where(sel, 0.0, NEG_BIG), I32)
        return carry

    lax.fori_loop(0, nkt, to_bias, 0)
    key_scr[pl.ds(pl.multiple_of(nkt * tk, tk), tk), :] = jnp.full(
        (tk, tq), np.float32(NEG_BIG).view(np.int32), I32)

    m_scr[...] = jnp.full(m_scr.shape, NEG_BIG, F32)
    sacc_scr[...] = jnp.full(sacc_scr.shape, NEG_BIG, F32)
    s_scr[...] = jnp.full(s_scr.shape, NEG_BIG, F32)
    acc_scr[...] = jnp.zeros(acc_scr.shape, F32)
    p_scr[...] = jnp.zeros(p_scr.shape, BF16)
    ones_rows = jnp.ones((PACK16, tk), BF16)

    def logits(kt):
        k0 = pl.multiple_of(jnp.minimum(kt, nkt - 1) * tk, tk)
        b0 = pl.multiple_of(kt * tk, tk)
        for h in range(N_HEADS):
            g = h // KV_GROUP
            k_t = k_ref[0, pl.ds(k0, tk), g * HEAD_DIM:(g + 1) * HEAD_DIM]
            lg = (jnp.dot(k_t, qt_ref[0, h * HEAD_DIM:(h + 1) * HEAD_DIM, :], preferred_element_type=F32)
                  + lax.bitcast_convert_type(key_scr[pl.ds(b0, tk), :], F32))
            lg_scr[h] = lg
            m_scr[h] = jnp.maximum(m_scr[h], jnp.max(lg, axis=0, keepdims=True))

    def probs():
        for h in range(N_HEADS):
            s = m_scr[h]
            s_scr[h] = s
            p_scr[h] = jnp.exp2(lg_scr[h] - s).astype(BF16)

    def accumulate(kt):
        k0 = pl.multiple_of(jnp.maximum(kt, 0) * tk, tk)
        for h in range(N_HEADS):
            g = h // KV_GROUP
            vt_t = jnp.concatenate([vt_ref[0, g * HEAD_DIM:(g + 1) * HEAD_DIM, pl.ds(k0, tk)], ones_rows], axis=0)
            s_new = s_scr[h]
            alpha = jnp.exp2(sacc_scr[h] - s_new)
            acc_scr[h] = alpha * acc_scr[h] + jnp.dot(vt_t, p_scr[h], preferred_element_type=F32)
            sacc_scr[h] = s_new

    logits(0)

    def attend(i, carry):
        accumulate(i - 1)
        probs()
        logits(i + 1)
        return carry

    lax.fori_loop(0, nkt, attend, 0)
    accumulate(nkt - 1)

    for h in range(N_HEADS):
        o = acc_scr[h, :HEAD_DIM] / acc_scr[h, HEAD_DIM:HEAD_DIM + 1]
        y_ref[0, :, h * HEAD_DIM:(h + 1) * HEAD_DIM] = o.T.astype(BF16)


def _dsa(qt, qit, wtt, k, vt, ki):
    bsz, seq, _ = k.shape
    n_sel = min(TOPK_MAX, seq // 4)
    tq = Q_TILE
    kern = functools.partial(_dsa_kernel, n_sel=n_sel, idx_bits=seq.bit_length())
    return pl.pallas_call(
        kern,
        out_shape=jax.ShapeDtypeStruct((bsz, seq, ATTN_WIDTH), BF16),
        grid=(bsz, seq // tq),
        in_specs=[
            pl.BlockSpec((1, ATTN_WIDTH, tq), lambda b, i: (b, 0, i)),
            pl.BlockSpec((1, IDX_WIDTH, tq), lambda b, i: (b, 0, i)),
            pl.BlockSpec((1, IDX_HEADS, tq), lambda b, i: (b, 0, i)),
            pl.BlockSpec((1, seq, KV_WIDTH), lambda b, i: (b, 0, 0)),
            pl.BlockSpec((1, KV_WIDTH, seq), lambda b, i: (b, 0, 0)),
            pl.BlockSpec((1, seq, IDX_DIM), lambda b, i: (b, 0, 0)),
        ],
        out_specs=pl.BlockSpec((1, tq, ATTN_WIDTH), lambda b, i: (b, i, 0)),
        scratch_shapes=[
            pltpu.VMEM((seq + K_TILE, tq), I32),
            pltpu.VMEM((seq, tq), I16),
            pltpu.VMEM((seq, tq), I16),
            pltpu.VMEM((N_HEADS, 1, tq), F32),
            pltpu.VMEM((N_HEADS, HEAD_DIM + PACK16, tq), F32),
            pltpu.VMEM((N_HEADS, 1, tq), F32),
            pltpu.VMEM((N_HEADS, 1, tq), F32),
            pltpu.VMEM((1, tq), I32),
            pltpu.VMEM((N_HEADS, K_TILE, tq), F32),
            pltpu.VMEM((N_HEADS, K_TILE, tq), BF16),
        ],
        compiler_params=_cparams("arbitrary", "arbitrary"),
        name="dsa",
    )(qt, qit, wtt, k, vt, ki)


def _deepnorm_ln(x, h, gate, g, b):
    y = DEEPNORM_ALPHA * x + (1.0 + gate) * h
    mu = jnp.mean(y, axis=-1, keepdims=True)
    yc = y - mu
    var = jnp.mean(yc * yc, axis=-1, keepdims=True)
    return yc * lax.rsqrt(var + LN_EPS) * g + b


def _merge_kernel(x_ref, ys_ref, ya_ref, sg_ref, gt_ref, wglu_ref, bglu_ref, pssm_ref, pattn_ref,
                  wout_ref, lng_ref, lnb_ref, o_ref):
    y = jax.nn.gelu(ys_ref[0].astype(F32))
    glu = jnp.dot(y.astype(BF16), wglu_ref[...], preferred_element_type=F32) + bglu_ref[...]
    y = (y * jax.nn.sigmoid(glu)).astype(BF16)
    sg = sg_ref[0]
    merged = (sg[:, :D_MODEL].astype(F32) * jnp.dot(y, pssm_ref[...], preferred_element_type=F32)
              + sg[:, D_MODEL:].astype(F32) * jnp.dot(ya_ref[0], pattn_ref[...], preferred_element_type=F32))
    h = jnp.dot(merged.astype(BF16), wout_ref[...], preferred_element_type=F32)
    o_ref[0] = _deepnorm_ln(x_ref[0], h, gt_ref[0], lng_ref[...], lnb_ref[...])


def _merge(x, ys, ya, sg, gt, wglu, bglu, pssm, pattn, wout, lng, lnb, tm):
    bsz, seq, d = x.shape
    tok = lambda w: pl.BlockSpec((1, tm, w), lambda b, i: (b, i, 0))
    vec = pl.BlockSpec((1, 1, d), lambda b, i: (b, 0, 0))
    const = lambda a: pl.BlockSpec(a.shape, lambda b, i: (0, 0))
    return pl.pallas_call(
        _merge_kernel,
        out_shape=jax.ShapeDtypeStruct(x.shape, F32),
        grid=(bsz, seq // tm),
        in_specs=[tok(d), tok(SSM_WIDTH), tok(ATTN_WIDTH), tok(2 * D_MODEL), vec,
                  const(wglu), const(bglu), const(pssm), const(pattn), const(wout), const(lng), const(lnb)],
        out_specs=tok(d),
        compiler_params=_cparams("arbitrary", "arbitrary"),
        name="merge",
    )(x, ys, ya, sg, gt, wglu, bglu, pssm, pattn, wout, lng, lnb)


def _ffn_kernel(x_ref, sc_ref, sh_ref, gt_ref, wgu_ref, wd_ref, lng_ref, lnb_ref, o_ref, hid_scr):
    x = x_ref[0]
    u = (x * (1.0 + sc_ref[0]) + sh_ref[0]).astype(BF16)
    for j in range(D_FF // FF_CHUNK):
        a = jnp.dot(u, wgu_ref[:, j * FF_CHUNK:(j + 1) * FF_CHUNK], preferred_element_type=F32)
        b = jnp.dot(u, wgu_ref[:, D_FF + j * FF_CHUNK:D_FF + (j + 1) * FF_CHUNK], preferred_element_type=F32)
        hid_scr[:, j * FF_CHUNK:(j + 1) * FF_CHUNK] = (a * jax.nn.sigmoid(a) * b).astype(BF16)
    f = jnp.dot(hid_scr[...], wd_ref[...], preferred_element_type=F32)
    o_ref[0] = _deepnorm_ln(x, f, gt_ref[0], lng_ref[...], lnb_ref[...])


def _ffn(x, sc, sh, gt, wgu, wd, lng, lnb, tm):
    bsz, seq, d = x.shape
    tok = pl.BlockSpec((1, tm, d), lambda b, i: (b, i, 0))
    vec = pl.BlockSpec((1, 1, d), lambda b, i: (b, 0, 0))
    const = lambda a: pl.BlockSpec(a.shape, lambda b, i: (0, 0))
    return pl.pallas_call(
        _ffn_kernel,
        out_shape=jax.ShapeDtypeStruct(x.shape, F32),
        grid=(bsz, seq // tm),
        in_specs=[tok, vec, vec, vec, const(wgu), const(wd), const(lng), const(lnb)],
        out_specs=tok,
        scratch_shapes=[pltpu.VMEM((tm, D_FF), BF16)],
        compiler_params=_cparams("arbitrary", "arbitrary"),
        name="ffn",
    )(x, sc, sh, gt, wgu, wd, lng, lnb)


def kernel(x, c, w_cond, b_cond, w_in, ssm_lam_re, ssm_lam_im, ssm_log_dt, ssm_b_re, ssm_b_im,
           ssm_c_re, ssm_c_im, ssm_d, ssm_w_glu, ssm_b_glu, p_ssm, p_attn, w_out,
           ln1_g, ln1_b, w_gate_up, w_down, ln2_g, ln2_b):
    bsz, seq, d = x.shape
    assert d == D_MODEL and seq % Q_TILE == 0 and seq % S5_CHUNK == 0
    tm = min(512, seq)
    tabs = _rope_tables(seq)
    mod = _cond(c, w_cond, b_cond)
    row = lambda a: a.reshape(1, -1)
    for l in range(DEPTH):
        sh1, sc1, gt1, sh2, sc2, gt2 = [mod[l, :, i * d:(i + 1) * d][:, None, :] for i in range(6)]
        wn, wt = _split_w_in(w_in[l])
        u_ssm, k, ki, sg, qt, vt, qit, wtt = _inproj(x, sc1, sh1, wn, wt, tabs, tm)
        ys = _s5(u_ssm, ssm_lam_re[l], ssm_lam_im[l], ssm_log_dt[l], ssm_b_re[l], ssm_b_im[l],
                 ssm_c_re[l], ssm_c_im[l], ssm_d[l])
        ya = _dsa(qt, qit, wtt, k, vt, ki)
        x = _merge(x, ys, ya, sg, gt1, ssm_w_glu[l].astype(BF16), row(ssm_b_glu[l]),
                   p_ssm[l].astype(BF16), p_attn[l].astype(BF16), w_out[l].astype(BF16),
                   row(ln1_g[l]), row(ln1_b[l]), tm)
        x = _ffn(x, sc2, sh2, gt2, w_gate_up[l].astype(BF16), w_down[l].astype(BF16),
                 row(ln2_g[l]), row(ln2_b[l]), tm)
    return x
```

```python
import functools
import math

import jax
import jax.numpy as jnp
import numpy as np
from jax import lax
from jax.experimental import pallas as pl
from jax.experimental.pallas import tpu as pltpu

F32 = jnp.float32
BF16 = jnp.bfloat16
I32 = jnp.int32
I16 = jnp.int16
HIGHEST = lax.Precision.HIGHEST

D_MODEL = 1024
DEPTH = 2
SSM_WIDTH = 512
SSM_GROUP = 16
SSM_GROUPS = SSM_WIDTH // SSM_GROUP
SSM_STATE = 64
HEAD_DIM = 128
N_HEADS = D_MODEL // HEAD_DIM
N_KV_HEADS = 2
KV_GROUP = N_HEADS // N_KV_HEADS
ATTN_WIDTH = N_HEADS * HEAD_DIM
KV_WIDTH = N_KV_HEADS * HEAD_DIM
IDX_HEADS = 8
IDX_DIM = 64
IDX_WIDTH = IDX_HEADS * IDX_DIM
IDX_SCALE = (IDX_HEADS * IDX_DIM) ** -0.5
TOPK_MAX = 256
ROPE_THETA = 10000.0
D_FF = -(-8 * D_MODEL // (3 * 256)) * 256
DEEPNORM_ALPHA = (2 * DEPTH) ** 0.25
LN_EPS = 1e-5
IN_SIZES = (SSM_WIDTH, ATTN_WIDTH, KV_WIDTH, KV_WIDTH, IDX_WIDTH, IDX_DIM, IDX_HEADS, D_MODEL, D_MODEL)
IN_OFFSETS = [0] + [int(o) for o in np.cumsum(IN_SIZES)]

LANES = 128
SUBLANES = 8
VMEM_LIMIT_BYTES = 56 * 1024 * 1024

S5_CHUNK = LANES
Q_TILE = 256
K_TILE = 256
TILE_GROUP = 4
FF_CHUNK = 256
INT_MIN = -(2 ** 31)
NEG_BIG = -1e30
HALF16 = 2 ** 15
PACK16 = 2 * SUBLANES
Q_SCALE = HEAD_DIM ** -0.5 * math.log2(math.e)


def _cparams(*sem):
    return pltpu.CompilerParams(dimension_semantics=sem, vmem_limit_bytes=VMEM_LIMIT_BYTES)


def _cond_kernel(c_ref, w_ref, b_ref, o_ref):
    c = c_ref[...]
    s = c * jax.nn.sigmoid(c)
    o_ref[0] = jnp.dot(s, w_ref[0], preferred_element_type=F32, precision=HIGHEST) + b_ref[0]


def _cond(c, w_cond, b_cond):
    depth, d, n = w_cond.shape
    bsz = c.shape[0]
    tn = 1536
    return pl.pallas_call(
        _cond_kernel,
        out_shape=jax.ShapeDtypeStruct((depth, bsz, n), F32),
        grid=(depth, n // tn),
        in_specs=[
            pl.BlockSpec((bsz, d), lambda l, j: (0, 0)),
            pl.BlockSpec((1, d, tn), lambda l, j: (l, 0, j)),
            pl.BlockSpec((1, 1, tn), lambda l, j: (l, 0, j)),
        ],
        out_specs=pl.BlockSpec((1, bsz, tn), lambda l, j: (l, 0, j)),
        compiler_params=_cparams("arbitrary", "arbitrary"),
        name="cond",
    )(c, w_cond, b_cond.reshape(depth, 1, n))


NAT_SSM, NAT_K, NAT_KI, NAT_G = 0, 512, 768, 896
NAT_WIDTH = 2944
TR_Q, TR_V, TR_QI, TR_W = 0, 1024, 1280, 1792
TR_ROWS = 1808


def _inproj_kernel(x_ref, sc_ref, sh_ref, wn_ref, wt_ref, cn_ref, sn_ref, cki_ref, ski_ref,
                   ct_ref, st_ref, ct32_ref, st32_ref,
                   ssm_ref, k_ref, ki_ref, sg_ref, qt_ref, vt_ref, qit_ref, wtt_ref):
    u = (x_ref[0] * (1.0 + sc_ref[0]) + sh_ref[0]).astype(BF16)
    zn = jnp.dot(u, wn_ref[...], preferred_element_type=F32)
    zt = lax.dot_general(wt_ref[...], u, (((1,), (1,)), ((), ())),
                         preferred_element_type=F32)

    ssm_ref[0] = zn[:, NAT_SSM:NAT_SSM + SSM_WIDTH].astype(BF16)
    sg_ref[0] = jax.nn.sigmoid(zn[:, NAT_G:NAT_G + 2 * D_MODEL]).astype(BF16)

    cn, sn = cn_ref[...], sn_ref[...]
    for h in range(N_KV_HEADS):
        xh = zn[:, NAT_K + h * HEAD_DIM:NAT_K + (h + 1) * HEAD_DIM]
        rot = pltpu.roll(xh, HEAD_DIM // 2, axis=1)
        k_ref[0, :, h * HEAD_DIM:(h + 1) * HEAD_DIM] = (xh * cn + rot * sn).astype(BF16)

    xk = zn[:, NAT_KI:NAT_KI + LANES]
    lane = lax.broadcasted_iota(I32, xk.shape, 1)
    rot = jnp.where((lane & (IDX_DIM // 2)) == 0,
                    pltpu.roll(xk, LANES - IDX_DIM // 2, axis=1),
                    pltpu.roll(xk, IDX_DIM // 2, axis=1))
    ki_ref[0] = (xk * cki_ref[...] + rot * ski_ref[...])[:, :IDX_DIM].astype(BF16)

    ct, st = ct_ref[...], st_ref[...]
    half = HEAD_DIM // 2
    for h in range(N_HEADS):
        r0 = TR_Q + h * HEAD_DIM
        x1, x2 = zt[r0:r0 + half], zt[r0 + half:r0 + HEAD_DIM]
        qt_ref[0, h * HEAD_DIM:h * HEAD_DIM + half, :] = ((x1 * ct - x2 * st) * Q_SCALE).astype(BF16)
        qt_ref[0, h * HEAD_DIM + half:(h + 1) * HEAD_DIM, :] = ((x2 * ct + x1 * st) * Q_SCALE).astype(BF16)
    ct32, st32 = ct32_ref[...], st32_ref[...]
    half = IDX_DIM // 2
    for h in range(IDX_HEADS):
        r0 = TR_QI + h * IDX_DIM
        x1, x2 = zt[r0:r0 + half], zt[r0 + half:r0 + IDX_DIM]
        qit_ref[0, h * IDX_DIM:h * IDX_DIM + half, :] = (x1 * ct32 - x2 * st32).astype(BF16)
        qit_ref[0, h * IDX_DIM + half:(h + 1) * IDX_DIM, :] = (x2 * ct32 + x1 * st32).astype(BF16)

    vt_ref[0] = zt[TR_V:TR_V + KV_WIDTH].astype(BF16)
    wtt_ref[0] = zt[TR_W:TR_W + IDX_HEADS] * IDX_SCALE


def _rope_tables(seq):
    pos = jnp.arange(seq, dtype=F32)

    def cs(half):
        inv = ROPE_THETA ** (-jnp.arange(half, dtype=F32) / half)
        ang = pos[:, None] * inv[None, :]
        return jnp.cos(ang), jnp.sin(ang)

    c64, s64 = cs(HEAD_DIM // 2)
    c32, s32 = cs(IDX_DIM // 2)
    z = jnp.zeros((seq, LANES - IDX_DIM), F32)
    return dict(
        cn=jnp.concatenate([c64, c64], -1), sn=jnp.concatenate([-s64, s64], -1),
        cki=jnp.concatenate([c32, c32, z], -1), ski=jnp.concatenate([-s32, s32, z], -1),
        ct=c64.T, st=s64.T, ct32=c32.T, st32=s32.T)


def _split_w_in(w_in):
    o = IN_OFFSETS
    ssm, q, k, v, qi, ki, wi, gs, ga = [w_in[:, o[i]:o[i + 1]] for i in range(9)]
    zpad = jnp.zeros((D_MODEL, LANES - IDX_DIM), w_in.dtype)
    wn = jnp.concatenate([ssm, k, ki, zpad, gs, ga], axis=1).astype(BF16)
    wt = jnp.concatenate([q, v, qi, wi, jnp.zeros((D_MODEL, TR_ROWS - TR_W - IDX_HEADS), w_in.dtype)],
                         axis=1).T.astype(BF16)
    return wn, wt


def _inproj(x, sc, sh, wn, wt, tabs, tm):
    bsz, seq, d = x.shape
    grid = (bsz, seq // tm)
    tok = lambda w: pl.BlockSpec((1, tm, w), lambda b, i: (b, i, 0))
    trn = lambda r: pl.BlockSpec((1, r, tm), lambda b, i: (b, 0, i))
    vec = pl.BlockSpec((1, 1, d), lambda b, i: (b, 0, 0))
    const = lambda a: pl.BlockSpec(a.shape, lambda b, i: (0, 0))
    ntab = lambda: pl.BlockSpec((tm, LANES), lambda b, i: (i, 0))
    ttab = lambda r: pl.BlockSpec((r, tm), lambda b, i: (0, i))
    out_shape = (
        jax.ShapeDtypeStruct((bsz, seq, SSM_WIDTH), BF16),
        jax.ShapeDtypeStruct((bsz, seq, KV_WIDTH), BF16),
        jax.ShapeDtypeStruct((bsz, seq, IDX_DIM), BF16),
        jax.ShapeDtypeStruct((bsz, seq, 2 * D_MODEL), BF16),
        jax.ShapeDtypeStruct((bsz, ATTN_WIDTH, seq), BF16),
        jax.ShapeDtypeStruct((bsz, KV_WIDTH, seq), BF16),
        jax.ShapeDtypeStruct((bsz, IDX_WIDTH, seq), BF16),
        jax.ShapeDtypeStruct((bsz, IDX_HEADS, seq), F32),
    )
    return pl.pallas_call(
        _inproj_kernel,
        out_shape=out_shape,
        grid=grid,
        in_specs=[tok(d), vec, vec, const(wn), const(wt), ntab(), ntab(), ntab(), ntab(),
                  ttab(HEAD_DIM // 2), ttab(HEAD_DIM // 2), ttab(IDX_DIM // 2), ttab(IDX_DIM // 2)],
        out_specs=(tok(SSM_WIDTH), tok(KV_WIDTH), tok(IDX_DIM), tok(2 * D_MODEL),
                   trn(ATTN_WIDTH), trn(KV_WIDTH), trn(IDX_WIDTH), trn(IDX_HEADS)),
        compiler_params=_cparams("arbitrary", "arbitrary"),
        name="in_proj",
    )(x, sc, sh, wn, wt, tabs["cn"], tabs["sn"], tabs["cki"], tabs["ski"],
      tabs["ct"], tabs["st"], tabs["ct32"], tabs["st32"])


def _s5_kernel(u_ref, lrr_ref, lir_ref, ldr_ref, lrc_ref, lic_ref, ldc_ref,
               brt_ref, bit_ref, cr_ref, ci_ref, crt_ref, cit_ref, dv_ref,
               y_ref, w_scr, kv_scr, *, bsz):
    t = S5_CHUNK
    p = SSM_STATE
    gsz = SSM_GROUP
    lrr, lir = lrr_ref[0], lir_ref[0]
    dtr = jnp.exp(ldr_ref[0])
    lrc, lic = lrc_ref[0], lic_ref[0]
    dtc = jnp.exp(ldc_ref[0])

    mag = jnp.exp(lrr * dtr)
    ar, ai = mag * jnp.cos(lir * dtr), mag * jnp.sin(lir * dtr)
    den = lrr * lrr + lir * lir
    nr = ar - 1.0
    fr, fi = (nr * lrr + ai * lir) / den, (ai * lrr - nr * lir) / den
    brt, bit = brt_ref[0], bit_ref[0]
    bbr, bbi = fr * brt - fi * bit, fr * bit + fi * brt

    cr, ci = cr_ref[0], ci_ref[0]
    bbr_h, bbi_h = bbr[:, :p], bbi[:, :p]
    cbr = jnp.concatenate([cr[i:i + 1] * bbr_h - ci[i:i + 1] * bbi_h for i in range(gsz)], axis=0)
    cbi = jnp.concatenate([cr[i:i + 1] * bbi_h + ci[i:i + 1] * bbr_h for i in range(gsz)], axis=0)
    lag = lax.broadcasted_iota(I32, (p, t), 1).astype(F32)
    lrc_h, lic_h, dtc_h = lrc[:p], lic[:p], dtc[:p]
    pmag = jnp.exp(lrc_h * dtc_h * lag)
    pth = lic_h * dtc_h * lag
    kv_scr[...] = (jnp.dot(cbr, pmag * jnp.cos(pth), preferred_element_type=F32, precision=HIGHEST)
                   - jnp.dot(cbi, pmag * jnp.sin(pth), preferred_element_type=F32, precision=HIGHEST))

    row = lax.broadcasted_iota(I32, (t, t), 0)
    col = lax.broadcasted_iota(I32, (t, t), 1)
    causal = col >= row

    def toeplitz_cols(i, carry):
        c0 = pl.multiple_of(i * t, t)
        for k in range(gsz):
            vec = kv_scr[pl.ds(i * gsz + k, 1), :]
            blk = pltpu.roll(jnp.broadcast_to(vec, (t, t)), 0, 1, stride=1, stride_axis=0)
            w_scr[k * t:(k + 1) * t, pl.ds(c0, t)] = jnp.where(causal, blk, 0.0).astype(BF16)
        return carry

    lax.fori_loop(0, gsz, toeplitz_cols, 0)

    lo = lax.broadcasted_iota(I32, (t, 2 * p), 1) < p
    rem = (t - 1 - lax.broadcasted_iota(I32, (t, 2 * p), 0)).astype(F32)
    emag = jnp.exp(lrr * dtr * rem)
    eth = lir * dtr * rem
    ecs, esn = emag * jnp.cos(eth), emag * jnp.sin(eth)
    e_same, e_swap = jnp.where(lo, ecs, esn), jnp.where(lo, esn, ecs)
    sgn = jnp.where(lo[:1], -1.0, 1.0)
    wb = jnp.concatenate([e_same * bbr[k:k + 1] + e_swap * (bbi[k:k + 1] * sgn) for k in range(gsz)],
                         axis=0).astype(BF16)

    top = lax.broadcasted_iota(I32, (2 * p, t), 0) < p
    step = (lax.broadcasted_iota(I32, (2 * p, t), 1) + 1).astype(F32)
    cmag = jnp.exp(lrc * dtc * step)
    cth = lic * dtc * step
    mc, ms = cmag * jnp.cos(cth), cmag * jnp.sin(cth)
    crt, cit = crt_ref[0], cit_ref[0]
    wc = jnp.concatenate(
        [jnp.where(top, crt[:, i:i + 1] * mc - cit[:, i:i + 1] * ms,
                   -(crt[:, i:i + 1] * ms + cit[:, i:i + 1] * mc)) for i in range(gsz)],
        axis=1).astype(BF16)

    u = u_ref[0]
    y = jnp.dot(u, w_scr[...], preferred_element_type=F32)
    g = jnp.dot(u, wb, preferred_element_type=F32)
    gr, gi = g[:, :p], g[:, p:]

    m_t = jnp.exp(lrr[:, :p] * dtr[:, :p] * float(t))
    at_r = m_t * jnp.cos(lir[:, :p] * dtr[:, :p] * float(t))
    at_i = m_t * jnp.sin(lir[:, :p] * dtr[:, :p] * float(t))
    nc = u.shape[0] // bsz
    hr = jnp.zeros((bsz, p), F32)
    hi = jnp.zeros((bsz, p), F32)
    hrs, his = [hr], [hi]
    for c in range(1, nc):
        sl = slice((c - 1) * bsz, c * bsz)
        hr, hi = at_r * hr - at_i * hi + gr[sl], at_r * hi + at_i * hr + gi[sl]
        hrs.append(hr)
        his.append(hi)
    hcat = jnp.concatenate([jnp.concatenate(hrs, axis=0), jnp.concatenate(his, axis=0)], axis=1)
    y = y + jnp.dot(hcat.astype(BF16), wc, preferred_element_type=F32)
    y_ref[0] = (y + dv_ref[0] * u.astype(F32)).astype(BF16)


def _s5(u_ssm, lam_re, lam_im, log_dt, b_re, b_im, c_re, c_im, d_skip):
    bsz, seq, _ = u_ssm.shape
    t, g, gs, p = S5_CHUNK, SSM_GROUPS, SSM_GROUP, SSM_STATE
    nc = seq // t
    rows = nc * bsz
    ug = u_ssm.reshape(bsz, nc, t, g, gs).transpose(3, 1, 0, 4, 2).reshape(g, rows, gs * t)

    dup_r = lambda a: jnp.concatenate([a, a], axis=-1)[:, None, :]
    dup_c = lambda a: jnp.concatenate([a, a], axis=-1)[:, :, None]
    ld = jnp.broadcast_to(log_dt[:, None], (g, p))
    brt = jnp.swapaxes(b_re, 1, 2)
    bit = jnp.swapaxes(b_im, 1, 2)
    crt = jnp.swapaxes(c_re, 1, 2)
    cit = jnp.swapaxes(c_im, 1, 2)
    args = (
        ug,
        dup_r(lam_re), dup_r(lam_im), dup_r(ld), dup_c(lam_re), dup_c(lam_im), dup_c(ld),
        jnp.concatenate([brt, brt], -1), jnp.concatenate([bit, bit], -1),
        c_re, c_im, jnp.concatenate([crt, crt], 1), jnp.concatenate([cit, cit], 1),
        jnp.repeat(d_skip.reshape(g, gs), t, axis=1)[:, None, :],
    )
    per_g = lambda a: pl.BlockSpec((1,) + a.shape[1:], lambda i: (i,) + (0,) * (a.ndim - 1))
    yg = pl.pallas_call(
        functools.partial(_s5_kernel, bsz=bsz),
        out_shape=jax.ShapeDtypeStruct((g, rows, gs * t), BF16),
        grid=(g,),
        in_specs=[per_g(a) for a in args],
        out_specs=pl.BlockSpec((1, rows, gs * t), lambda i: (i, 0, 0)),
        scratch_shapes=[pltpu.VMEM((gs * t, gs * t), BF16), pltpu.VMEM((gs * gs, t), F32)],
        compiler_params=_cparams("arbitrary"),
        name="s5",
    )(*args)
    return yg.reshape(g, nc, bsz, gs, t).transpose(2, 1, 4, 0, 3).reshape(bsz, seq, g * gs)


def _dsa_kernel(qt_ref, qit_ref, wt_ref, k_ref, vt_ref, ki_ref, y_ref,
                key_scr, hi_scr, lo_scr, m_scr, acc_scr, sacc_scr, s_scr, lg_scr, p_scr, *, n_sel, idx_bits):
    tq, tk = Q_TILE, K_TILE
    qi = pl.program_id(1)
    nkt = qi + 1
    q0 = qi * tq
    diff = (lax.broadcasted_iota(I32, (tk, tq), 0) - lax.broadcasted_iota(I32, (tk, tq), 1))

    ngrp = (nkt + TILE_GROUP - 1) // TILE_GROUP
    npair = (nkt + 1) // 2
    rows = lambda kt: pl.ds(pl.multiple_of(kt * tk, tk), tk)
    filler16 = jnp.full((tk, tq), -HALF16, I16)

    def score_tile(kt):
        k0 = kt * tk
        ki_t = ki_ref[0, rows(kt), :]
        s = jnp.zeros((tk, tq), F32)
        for h in range(IDX_HEADS):
            rel = jnp.dot(ki_t, qit_ref[0, h * IDX_DIM:(h + 1) * IDX_DIM, :], preferred_element_type=F32)
            s = s + jnp.maximum(rel, 0.0) * wt_ref[0, h:h + 1, :]
        bits = lax.bitcast_convert_type(s, I32)
        key = bits ^ ((bits >> 31) & 0x7FFFFFFF)
        key = jnp.where(s == 0.0, 0, key)
        key = jnp.where(diff <= q0 - k0, key, INT_MIN)
        key_scr[rows(kt), :] = key
        hi_scr[rows(kt), :] = (key >> 16).astype(I16)
        lo_scr[rows(kt), :] = ((key & 0xFFFF) - HALF16).astype(I16)

    def score_pair(j, carry):
        score_tile(2 * j)
        score_tile(2 * j + 1)
        return carry

    lax.fori_loop(0, npair, score_pair, 0)

    @pl.when(2 * npair < TILE_GROUP * ngrp)
    def _():
        for kt in (2 * npair, 2 * npair + 1):
            key_scr[rows(kt), :] = jnp.full((tk, tq), INT_MIN, I32)
            hi_scr[rows(kt), :] = filler16
            lo_scr[rows(kt), :] = filler16

    def tree_sum(parts):
        while len(parts) > 1:
            parts = [a + b for a, b in zip(parts[::2], parts[1::2])]
        return parts[0]

    def count(*preds):
        def body(g, accs):
            out = list(accs)
            for u in range(TILE_GROUP):
                kt = g * TILE_GROUP + u
                keys = key_scr[rows(kt), :]
                for n, pred in enumerate(preds):
                    hit = jnp.where(pred(keys, kt * tk), 1, 0)
                    out[n] = out[n] + jnp.sum(hit.reshape(tk // SUBLANES, SUBLANES, tq), axis=0)
            return tuple(out)
        parts = lax.fori_loop(0, ngrp, body, tuple(jnp.zeros((SUBLANES, tq), I32) for _ in preds))
        return [jnp.sum(part, axis=0, keepdims=True) for part in parts]

    def count16(ref, pred):
        def body(g, acc):
            tiles = []
            for u in range(TILE_GROUP):
                hit = jnp.where(pred(ref[rows(g * TILE_GROUP + u), :]), jnp.int16(1), jnp.int16(0))
                tiles.append(tree_sum([hit[r:r + PACK16] for r in range(0, tk, PACK16)]))
            return acc + tree_sum(tiles)
        part = lax.fori_loop(0, ngrp, body, jnp.zeros((PACK16, tq), I16))
        return jnp.sum(part.astype(I32), axis=0, keepdims=True)

    def search16(ref, target):
        def bit_step(i, c_u):
            trial = c_u | (jnp.int32(1) << (15 - i))
            trial_s = (trial - HALF16).astype(I16)
            cnt = count16(ref, lambda v: v >= trial_s)
            return jnp.where(cnt >= target, trial, c_u)
        return lax.fori_loop(0, 16, bit_step, jnp.zeros((1, tq), I32))

    hi_thr = search16(hi_scr, n_sel) - HALF16
    hi_thr16 = hi_thr.astype(I16)
    cnt_above = count16(hi_scr, lambda v: v > hi_thr16)

    def keep_bucket(g, carry):
        for u in range(TILE_GROUP):
            r = rows(g * TILE_GROUP + u)
            lo_scr[r, :] = jnp.where(hi_scr[r, :] == hi_thr16, lo_scr[r, :], jnp.int16(-HALF16))
        return carry

    lax.fori_loop(0, ngrp, keep_bucket, 0)
    lo_thr = search16(lo_scr, n_sel - cnt_above)
    thr = jnp.maximum(hi_thr * (2 * HALF16) + lo_thr, INT_MIN + 1)

    neg_bits = np.float32(NEG_BIG).view(np.int32)

    def to_bias(g, accs):
        n_gt, n_eq = accs
        for u in range(TILE_GROUP):
            kt = g * TILE_GROUP + u
            keys = key_scr[rows(kt), :]
            gt, eq = keys > thr, keys == thr
            key_scr[rows(kt), :] = jnp.where(gt, 0, jnp.where(eq, INT_MIN, neg_bits))
            n_gt = n_gt + jnp.sum(jnp.where(gt, 1, 0).reshape(tk // SUBLANES, SUBLANES, tq), axis=0)
            n_eq = n_eq + jnp.sum(jnp.where(eq, 1, 0).reshape(tk // SUBLANES, SUBLANES, tq), axis=0)
        return n_gt, n_eq

    zeros8 = jnp.zeros((SUBLANES, tq), I32)
    cnt_gt, cnt_eq = [jnp.sum(c, axis=0, keepdims=True) for c in lax.fori_loop(0, ngrp, to_bias, (zeros8, zeros8))]

    @pl.when(jnp.max(cnt_gt + cnt_eq) > n_sel)
    def _():
        need = n_sel - cnt_gt
        sub = lax.broadcasted_iota(I32, (tk, tq), 0)

        def idx_step(i, j_cut):
            trial = j_cut | (jnp.int32(1) << (idx_bits - 1 - i))
            cnt, = count(lambda bits, k0: (bits == INT_MIN) & (sub + k0 < trial))
            return jnp.where(cnt <= need, trial, j_cut)

        j_cut = lax.fori_loop(0, idx_bits, idx_step, jnp.zeros((1, tq), I32))

        def drop(g, carry):
            for u in range(TILE_GROUP):
                kt = g * TILE_GROUP + u
                bits = key_scr[rows(kt), :]
                key_scr[rows(kt), :] = jnp.where((bits == INT_MIN) & (sub + kt * tk >= j_cut), neg_bits, bits)
            return carry

        lax.fori_loop(0, ngrp, drop, 0)

    key_scr[rows(nkt), :] = jnp.full((tk, tq), neg_bits, I32)

    m_scr[...] = jnp.full(m_scr.shape, NEG_BIG, F32)
    sacc_scr[...] = jnp.full(sacc_scr.shape, NEG_BIG, F32)
    s_scr[...] = jnp.full(s_scr.shape, NEG_BIG, F32)
    acc_scr[...] = jnp.zeros(acc_scr.shape, F32)
    p_scr[...] = jnp.zeros(p_scr.shape, BF16)
    ones_rows = jnp.ones((PACK16, tk), BF16)

    def logits(kt):
        k0 = pl.multiple_of(jnp.minimum(kt, nkt - 1) * tk, tk)
        b0 = pl.multiple_of(kt * tk, tk)
        for h in range(N_HEADS):
            g = h // KV_GROUP
            k_t = k_ref[0, pl.ds(k0, tk), g * HEAD_DIM:(g + 1) * HEAD_DIM]
            lg = (jnp.dot(k_t, qt_ref[0, h * HEAD_DIM:(h + 1) * HEAD_DIM, :], preferred_element_type=F32)
                  + lax.bitcast_convert_type(key_scr[pl.ds(b0, tk), :], F32))
            lg_scr[h] = lg
            m_scr[h] = jnp.maximum(m_scr[h], jnp.max(lg, axis=0, keepdims=True))

    def probs():
        for h in range(N_HEADS):
            s = m_scr[h]
            s_scr[h] = s
            p_scr[h] = jnp.exp2(lg_scr[h] - s).astype(BF16)

    def accumulate(kt):
        k0 = pl.multiple_of(jnp.maximum(kt, 0) * tk, tk)
        for h in range(N_HEADS):
            g = h // KV_GROUP
            vt_t = jnp.concatenate([vt_ref[0, g * HEAD_DIM:(g + 1) * HEAD_DIM, pl.ds(k0, tk)], ones_rows], axis=0)
            s_new = s_scr[h]
            alpha = jnp.exp2(sacc_scr[h] - s_new)
            acc_scr[h] = alpha * acc_scr[h] + jnp.dot(vt_t, p_scr[h], preferred_element_type=F32)
            sacc_scr[h] = s_new

    logits(0)

    def attend(i, carry):
        accumulate(i - 1)
        probs()
        logits(i + 1)
        return carry

    lax.fori_loop(0, nkt, attend, 0)
    accumulate(nkt - 1)

    for h in range(N_HEADS):
        o = acc_scr[h, :HEAD_DIM] / acc_scr[h, HEAD_DIM:HEAD_DIM + 1]
        y_ref[0, :, h * HEAD_DIM:(h + 1) * HEAD_DIM] = o.T.astype(BF16)


def _dsa(qt, qit, wtt, k, vt, ki):
    bsz, seq, _ = k.shape
    n_sel = min(TOPK_MAX, seq // 4)
    tq = Q_TILE
    assert Q_TILE == K_TILE and seq % (2 * K_TILE) == 0
    pad_rows = -(-seq // (TILE_GROUP * K_TILE)) * TILE_GROUP * K_TILE
    kern = functools.partial(_dsa_kernel, n_sel=n_sel, idx_bits=seq.bit_length())
    return pl.pallas_call(
        kern,
        out_shape=jax.ShapeDtypeStruct((bsz, seq, ATTN_WIDTH), BF16),
        grid=(bsz, seq // tq),
        in_specs=[
            pl.BlockSpec((1, ATTN_WIDTH, tq), lambda b, i: (b, 0, i)),
            pl.BlockSpec((1, IDX_WIDTH, tq), lambda b, i: (b, 0, i)),
            pl.BlockSpec((1, IDX_HEADS, tq), lambda b, i: (b, 0, i)),
            pl.BlockSpec((1, seq, KV_WIDTH), lambda b, i: (b, 0, 0)),
            pl.BlockSpec((1, KV_WIDTH, seq), lambda b, i: (b, 0, 0)),
            pl.BlockSpec((1, seq, IDX_DIM), lambda b, i: (b, 0, 0)),
        ],
        out_specs=pl.BlockSpec((1, tq, ATTN_WIDTH), lambda b, i: (b, i, 0)),
        scratch_shapes=[
            pltpu.VMEM((pad_rows + K_TILE, tq), I32),
            pltpu.VMEM((pad_rows, tq), I16),
            pltpu.VMEM((pad_rows, tq), I16),
            pltpu.VMEM((N_HEADS, 1, tq), F32),
            pltpu.VMEM((N_HEADS, HEAD_DIM + PACK16, tq), F32),
            pltpu.VMEM((N_HEADS, 1, tq), F32),
            pltpu.VMEM((N_HEADS, 1, tq), F32),
            pltpu.VMEM((N_HEADS, K_TILE, tq), F32),
            pltpu.VMEM((N_HEADS, K_TILE, tq), BF16),
        ],
        compiler_params=_cparams("arbitrary", "arbitrary"),
        name="dsa",
    )(qt, qit, wtt, k, vt, ki)


def _deepnorm_ln(x, h, gate, g, b):
    y = DEEPNORM_ALPHA * x + (1.0 + gate) * h
    mu = jnp.mean(y, axis=-1, keepdims=True)
    yc = y - mu
    var = jnp.mean(yc * yc, axis=-1, keepdims=True)
    return yc * lax.rsqrt(var + LN_EPS) * g + b


def _merge_kernel(x_ref, ys_ref, ya_ref, sg_ref, gt_ref, wglu_ref, bglu_ref, pssm_ref, pattn_ref,
                  wout_ref, lng_ref, lnb_ref, o_ref):
    y = jax.nn.gelu(ys_ref[0].astype(F32))
    glu = jnp.dot(y.astype(BF16), wglu_ref[...], preferred_element_type=F32) + bglu_ref[...]
    y = (y * jax.nn.sigmoid(glu)).astype(BF16)
    sg = sg_ref[0]
    merged = (sg[:, :D_MODEL].astype(F32) * jnp.dot(y, pssm_ref[...], preferred_element_type=F32)
              + sg[:, D_MODEL:].astype(F32) * jnp.dot(ya_ref[0], pattn_ref[...], preferred_element_type=F32))
    h = jnp.dot(merged.astype(BF16), wout_ref[...], preferred_element_type=F32)
    o_ref[0] = _deepnorm_ln(x_ref[0], h, gt_ref[0], lng_ref[...], lnb_ref[...])


def _merge(x, ys, ya, sg, gt, wglu, bglu, pssm, pattn, wout, lng, lnb, tm):
    bsz, seq, d = x.shape
    tok = lambda w: pl.BlockSpec((1, tm, w), lambda b, i: (b, i, 0))
    vec = pl.BlockSpec((1, 1, d), lambda b, i: (b, 0, 0))
    const = lambda a: pl.BlockSpec(a.shape, lambda b, i: (0, 0))
    return pl.pallas_call(
        _merge_kernel,
        out_shape=jax.ShapeDtypeStruct(x.shape, F32),
        grid=(bsz, seq // tm),
        in_specs=[tok(d), tok(SSM_WIDTH), tok(ATTN_WIDTH), tok(2 * D_MODEL), vec,
                  const(wglu), const(bglu), const(pssm), const(pattn), const(wout), const(lng), const(lnb)],
        out_specs=tok(d),
        compiler_params=_cparams("arbitrary", "arbitrary"),
        name="merge",
    )(x, ys, ya, sg, gt, wglu, bglu, pssm, pattn, wout, lng, lnb)


def _ffn_kernel(x_ref, sc_ref, sh_ref, gt_ref, wgu_ref, wd_ref, lng_ref, lnb_ref, o_ref, hid_scr):
    x = x_ref[0]
    u = (x * (1.0 + sc_ref[0]) + sh_ref[0]).astype(BF16)
    for j in range(D_FF // FF_CHUNK):
        a = jnp.dot(u, wgu_ref[:, j * FF_CHUNK:(j + 1) * FF_CHUNK], preferred_element_type=F32)
        b = jnp.dot(u, wgu_ref[:, D_FF + j * FF_CHUNK:D_FF + (j + 1) * FF_CHUNK], preferred_element_type=F32)
        hid_scr[:, j * FF_CHUNK:(j + 1) * FF_CHUNK] = (a * jax.nn.sigmoid(a) * b).astype(BF16)
    f = jnp.dot(hid_scr[...], wd_ref[...], preferred_element_type=F32)
    o_ref[0] = _deepnorm_ln(x, f, gt_ref[0], lng_ref[...], lnb_ref[...])


def _ffn(x, sc, sh, gt, wgu, wd, lng, lnb, tm):
    bsz, seq, d = x.shape
    tok = pl.BlockSpec((1, tm, d), lambda b, i: (b, i, 0))
    vec = pl.BlockSpec((1, 1, d), lambda b, i: (b, 0, 0))
    const = lambda a: pl.BlockSpec(a.shape, lambda b, i: (0, 0))
    return pl.pallas_call(
        _ffn_kernel,
        out_shape=jax.ShapeDtypeStruct(x.shape, F32),
        grid=(bsz, seq // tm),
        in_specs=[tok, vec, vec, vec, const(wgu), const(wd), const(lng), const(lnb)],
        out_specs=tok,
        scratch_shapes=[pltpu.VMEM((tm, D_FF), BF16)],
        compiler_params=_cparams("arbitrary", "arbitrary"),
        name="ffn",
    )(x, sc, sh, gt, wgu, wd, lng, lnb)


def kernel(x, c, w_cond, b_cond, w_in, ssm_lam_re, ssm_lam_im, ssm_log_dt, ssm_b_re, ssm_b_im,
           ssm_c_re, ssm_c_im, ssm_d, ssm_w_glu, ssm_b_glu, p_ssm, p_attn, w_out,
           ln1_g, ln1_b, w_gate_up, w_down, ln2_g, ln2_b):
    bsz, seq, d = x.shape
    assert d == D_MODEL and seq % Q_TILE == 0 and seq % S5_CHUNK == 0
    tm = min(512, seq)
    tabs = _rope_tables(seq)
    mod = _cond(c, w_cond, b_cond)
    row = lambda a: a.reshape(1, -1)
    for l in range(DEPTH):
        sh1, sc1, gt1, sh2, sc2, gt2 = [mod[l, :, i * d:(i + 1) * d][:, None, :] for i in range(6)]
        wn, wt = _split_w_in(w_in[l])
        u_ssm, k, ki, sg, qt, vt, qit, wtt = _inproj(x, sc1, sh1, wn, wt, tabs, tm)
        ys = _s5(u_ssm, ssm_lam_re[l], ssm_lam_im[l], ssm_log_dt[l], ssm_b_re[l], ssm_b_im[l],
                 ssm_c_re[l], ssm_c_im[l], ssm_d[l])
        ya = _dsa(qt, qit, wtt, k, vt, ki)
        x = _merge(x, ys, ya, sg, gt1, ssm_w_glu[l].astype(BF16), row(ssm_b_glu[l]),
                   p_ssm[l].astype(BF16), p_attn[l].astype(BF16), w_out[l].astype(BF16),
                   row(ln1_g[l]), row(ln1_b[l]), tm)
        x = _ffn(x, sc2, sh2, gt2, w_gate_up[l].astype(BF16), w_down[l].astype(BF16),
                 row(ln2_g[l]), row(ln2_b[l]), tm)
    return x
```

```python
import functools
import math

import jax
import jax.numpy as jnp
import numpy as np
from jax import lax
from jax.experimental import pallas as pl
from jax.experimental.pallas import tpu as pltpu

F32 = jnp.float32
BF16 = jnp.bfloat16
I32 = jnp.int32
I16 = jnp.int16
HIGHEST = lax.Precision.HIGHEST

D_MODEL = 1024
DEPTH = 2
SSM_WIDTH = 512
SSM_GROUP = 16
SSM_GROUPS = SSM_WIDTH // SSM_GROUP
SSM_STATE = 64
HEAD_DIM = 128
N_HEADS = D_MODEL // HEAD_DIM
N_KV_HEADS = 2
KV_GROUP = N_HEADS // N_KV_HEADS
ATTN_WIDTH = N_HEADS * HEAD_DIM
KV_WIDTH = N_KV_HEADS * HEAD_DIM
IDX_HEADS = 8
IDX_DIM = 64
IDX_WIDTH = IDX_HEADS * IDX_DIM
IDX_SCALE = (IDX_HEADS * IDX_DIM) ** -0.5
TOPK_MAX = 256
ROPE_THETA = 10000.0
D_FF = -(-8 * D_MODEL // (3 * 256)) * 256
DEEPNORM_ALPHA = (2 * DEPTH) ** 0.25
LN_EPS = 1e-5
IN_SIZES = (SSM_WIDTH, ATTN_WIDTH, KV_WIDTH, KV_WIDTH, IDX_WIDTH, IDX_DIM, IDX_HEADS, D_MODEL, D_MODEL)
IN_OFFSETS = [0] + [int(o) for o in np.cumsum(IN_SIZES)]

LANES = 128
SUBLANES = 8
VMEM_LIMIT_BYTES = 56 * 1024 * 1024

S5_CHUNK = LANES
Q_TILE = 256
K_TILE = 256
TILE_GROUP = 4
ATTN_TILES = 2
FF_CHUNK = 256
INT_MIN = -(2 ** 31)
NEG_BIG = -1e30
HALF16 = 2 ** 15
PACK16 = 2 * SUBLANES
Q_SCALE = HEAD_DIM ** -0.5 * math.log2(math.e)


def _cparams(*sem):
    return pltpu.CompilerParams(dimension_semantics=sem, vmem_limit_bytes=VMEM_LIMIT_BYTES)


def _cond_kernel(c_ref, w_ref, b_ref, o_ref):
    c = c_ref[...]
    s = c * jax.nn.sigmoid(c)
    o_ref[0] = jnp.dot(s, w_ref[0], preferred_element_type=F32, precision=HIGHEST) + b_ref[0]


def _cond(c, w_cond, b_cond):
    depth, d, n = w_cond.shape
    bsz = c.shape[0]
    tn = 1536
    return pl.pallas_call(
        _cond_kernel,
        out_shape=jax.ShapeDtypeStruct((depth, bsz, n), F32),
        grid=(depth, n // tn),
        in_specs=[
            pl.BlockSpec((bsz, d), lambda l, j: (0, 0)),
            pl.BlockSpec((1, d, tn), lambda l, j: (l, 0, j)),
            pl.BlockSpec((1, 1, tn), lambda l, j: (l, 0, j)),
        ],
        out_specs=pl.BlockSpec((1, bsz, tn), lambda l, j: (l, 0, j)),
        compiler_params=_cparams("arbitrary", "arbitrary"),
        name="cond",
    )(c, w_cond, b_cond.reshape(depth, 1, n))


NAT_SSM, NAT_K, NAT_KI, NAT_G = 0, 512, 768, 896
NAT_WIDTH = 2944
TR_Q, TR_V, TR_QI, TR_W = 0, 1024, 1280, 1792
TR_ROWS = 1808


def _inproj_kernel(x_ref, sc_ref, sh_ref, wn_ref, wt_ref, cn_ref, sn_ref, cki_ref, ski_ref,
                   ct_ref, st_ref, ct32_ref, st32_ref,
                   ssm_ref, k_ref, ki_ref, sg_ref, qt_ref, vt_ref, qit_ref, wtt_ref):
    u = (x_ref[0] * (1.0 + sc_ref[0]) + sh_ref[0]).astype(BF16)
    zn = jnp.dot(u, wn_ref[...], preferred_element_type=F32)
    zt = lax.dot_general(wt_ref[...], u, (((1,), (1,)), ((), ())),
                         preferred_element_type=F32)

    ssm_ref[0] = zn[:, NAT_SSM:NAT_SSM + SSM_WIDTH].astype(BF16)
    sg_ref[0] = jax.nn.sigmoid(zn[:, NAT_G:NAT_G + 2 * D_MODEL]).astype(BF16)

    cn, sn = cn_ref[...], sn_ref[...]
    for h in range(N_KV_HEADS):
        xh = zn[:, NAT_K + h * HEAD_DIM:NAT_K + (h + 1) * HEAD_DIM]
        rot = pltpu.roll(xh, HEAD_DIM // 2, axis=1)
        k_ref[0, :, h * HEAD_DIM:(h + 1) * HEAD_DIM] = (xh * cn + rot * sn).astype(BF16)

    xk = zn[:, NAT_KI:NAT_KI + LANES]
    lane = lax.broadcasted_iota(I32, xk.shape, 1)
    rot = jnp.where((lane & (IDX_DIM // 2)) == 0,
                    pltpu.roll(xk, LANES - IDX_DIM // 2, axis=1),
                    pltpu.roll(xk, IDX_DIM // 2, axis=1))
    ki_ref[0] = (xk * cki_ref[...] + rot * ski_ref[...])[:, :IDX_DIM].astype(BF16)

    ct, st = ct_ref[...], st_ref[...]
    half = HEAD_DIM // 2
    for h in range(N_HEADS):
        r0 = TR_Q + h * HEAD_DIM
        x1, x2 = zt[r0:r0 + half], zt[r0 + half:r0 + HEAD_DIM]
        qt_ref[0, h * HEAD_DIM:h * HEAD_DIM + half, :] = ((x1 * ct - x2 * st) * Q_SCALE).astype(BF16)
        qt_ref[0, h * HEAD_DIM + half:(h + 1) * HEAD_DIM, :] = ((x2 * ct + x1 * st) * Q_SCALE).astype(BF16)
    ct32, st32 = ct32_ref[...], st32_ref[...]
    half = IDX_DIM // 2
    for h in range(IDX_HEADS):
        r0 = TR_QI + h * IDX_DIM
        x1, x2 = zt[r0:r0 + half], zt[r0 + half:r0 + IDX_DIM]
        qit_ref[0, h * IDX_DIM:h * IDX_DIM + half, :] = (x1 * ct32 - x2 * st32).astype(BF16)
        qit_ref[0, h * IDX_DIM + half:(h + 1) * IDX_DIM, :] = (x2 * ct32 + x1 * st32).astype(BF16)

    vt_ref[0] = zt[TR_V:TR_V + KV_WIDTH].astype(BF16)
    wtt_ref[0] = zt[TR_W:TR_W + IDX_HEADS] * IDX_SCALE


def _rope_tables(seq):
    pos = jnp.arange(seq, dtype=F32)

    def cs(half):
        inv = ROPE_THETA ** (-jnp.arange(half, dtype=F32) / half)
        ang = pos[:, None] * inv[None, :]
        return jnp.cos(ang), jnp.sin(ang)

    c64, s64 = cs(HEAD_DIM // 2)
    c32, s32 = cs(IDX_DIM // 2)
    z = jnp.zeros((seq, LANES - IDX_DIM), F32)
    return dict(
        cn=jnp.concatenate([c64, c64], -1), sn=jnp.concatenate([-s64, s64], -1),
        cki=jnp.concatenate([c32, c32, z], -1), ski=jnp.concatenate([-s32, s32, z], -1),
        ct=c64.T, st=s64.T, ct32=c32.T, st32=s32.T)


def _split_w_in(w_in):
    o = IN_OFFSETS
    ssm, q, k, v, qi, ki, wi, gs, ga = [w_in[:, o[i]:o[i + 1]] for i in range(9)]
    zpad = jnp.zeros((D_MODEL, LANES - IDX_DIM), w_in.dtype)
    wn = jnp.concatenate([ssm, k, ki, zpad, gs, ga], axis=1).astype(BF16)
    wt = jnp.concatenate([q, v, qi, wi, jnp.zeros((D_MODEL, TR_ROWS - TR_W - IDX_HEADS), w_in.dtype)],
                         axis=1).T.astype(BF16)
    return wn, wt


def _inproj(x, sc, sh, wn, wt, tabs, tm):
    bsz, seq, d = x.shape
    grid = (bsz, seq // tm)
    tok = lambda w: pl.BlockSpec((1, tm, w), lambda b, i: (b, i, 0))
    trn = lambda r: pl.BlockSpec((1, r, tm), lambda b, i: (b, 0, i))
    vec = pl.BlockSpec((1, 1, d), lambda b, i: (b, 0, 0))
    const = lambda a: pl.BlockSpec(a.shape, lambda b, i: (0, 0))
    ntab = lambda: pl.BlockSpec((tm, LANES), lambda b, i: (i, 0))
    ttab = lambda r: pl.BlockSpec((r, tm), lambda b, i: (0, i))
    out_shape = (
        jax.ShapeDtypeStruct((bsz, seq, SSM_WIDTH), BF16),
        jax.ShapeDtypeStruct((bsz, seq, KV_WIDTH), BF16),
        jax.ShapeDtypeStruct((bsz, seq, IDX_DIM), BF16),
        jax.ShapeDtypeStruct((bsz, seq, 2 * D_MODEL), BF16),
        jax.ShapeDtypeStruct((bsz, ATTN_WIDTH, seq), BF16),
        jax.ShapeDtypeStruct((bsz, KV_WIDTH, seq), BF16),
        jax.ShapeDtypeStruct((bsz, IDX_WIDTH, seq), BF16),
        jax.ShapeDtypeStruct((bsz, IDX_HEADS, seq), F32),
    )
    return pl.pallas_call(
        _inproj_kernel,
        out_shape=out_shape,
        grid=grid,
        in_specs=[tok(d), vec, vec, const(wn), const(wt), ntab(), ntab(), ntab(), ntab(),
                  ttab(HEAD_DIM // 2), ttab(HEAD_DIM // 2), ttab(IDX_DIM // 2), ttab(IDX_DIM // 2)],
        out_specs=(tok(SSM_WIDTH), tok(KV_WIDTH), tok(IDX_DIM), tok(2 * D_MODEL),
                   trn(ATTN_WIDTH), trn(KV_WIDTH), trn(IDX_WIDTH), trn(IDX_HEADS)),
        compiler_params=_cparams("arbitrary", "arbitrary"),
        name="in_proj",
    )(x, sc, sh, wn, wt, tabs["cn"], tabs["sn"], tabs["cki"], tabs["ski"],
      tabs["ct"], tabs["st"], tabs["ct32"], tabs["st32"])


def _s5_kernel(u_ref, lrr_ref, lir_ref, ldr_ref, lrc_ref, lic_ref, ldc_ref,
               brt_ref, bit_ref, cr_ref, ci_ref, crt_ref, cit_ref, dv_ref,
               y_ref, w_scr, kv_scr, *, bsz):
    t = S5_CHUNK
    p = SSM_STATE
    gsz = SSM_GROUP
    lrr, lir = lrr_ref[0], lir_ref[0]
    dtr = jnp.exp(ldr_ref[0])
    lrc, lic = lrc_ref[0], lic_ref[0]
    dtc = jnp.exp(ldc_ref[0])

    mag = jnp.exp(lrr * dtr)
    ar, ai = mag * jnp.cos(lir * dtr), mag * jnp.sin(lir * dtr)
    den = lrr * lrr + lir * lir
    nr = ar - 1.0
    fr, fi = (nr * lrr + ai * lir) / den, (ai * lrr - nr * lir) / den
    brt, bit = brt_ref[0], bit_ref[0]
    bbr, bbi = fr * brt - fi * bit, fr * bit + fi * brt

    cr, ci = cr_ref[0], ci_ref[0]
    bbr_h, bbi_h = bbr[:, :p], bbi[:, :p]
    cbr = jnp.concatenate([cr[i:i + 1] * bbr_h - ci[i:i + 1] * bbi_h for i in range(gsz)], axis=0)
    cbi = jnp.concatenate([cr[i:i + 1] * bbi_h + ci[i:i + 1] * bbr_h for i in range(gsz)], axis=0)
    lag = lax.broadcasted_iota(I32, (p, t), 1).astype(F32)
    lrc_h, lic_h, dtc_h = lrc[:p], lic[:p], dtc[:p]
    pmag = jnp.exp(lrc_h * dtc_h * lag)
    pth = lic_h * dtc_h * lag
    kv_scr[...] = (jnp.dot(cbr, pmag * jnp.cos(pth), preferred_element_type=F32, precision=HIGHEST)
                   - jnp.dot(cbi, pmag * jnp.sin(pth), preferred_element_type=F32, precision=HIGHEST))

    row = lax.broadcasted_iota(I32, (t, t), 0)
    col = lax.broadcasted_iota(I32, (t, t), 1)
    causal = col >= row

    def toeplitz_cols(i, carry):
        c0 = pl.multiple_of(i * t, t)
        for k in range(gsz):
            vec = kv_scr[pl.ds(i * gsz + k, 1), :]
            blk = pltpu.roll(jnp.broadcast_to(vec, (t, t)), 0, 1, stride=1, stride_axis=0)
            w_scr[k * t:(k + 1) * t, pl.ds(c0, t)] = jnp.where(causal, blk, 0.0).astype(BF16)
        return carry

    lax.fori_loop(0, gsz, toeplitz_cols, 0)

    lo = lax.broadcasted_iota(I32, (t, 2 * p), 1) < p
    rem = (t - 1 - lax.broadcasted_iota(I32, (t, 2 * p), 0)).astype(F32)
    emag = jnp.exp(lrr * dtr * rem)
    eth = lir * dtr * rem
    ecs, esn = emag * jnp.cos(eth), emag * jnp.sin(eth)
    e_same, e_swap = jnp.where(lo, ecs, esn), jnp.where(lo, esn, ecs)
    sgn = jnp.where(lo[:1], -1.0, 1.0)
    wb = jnp.concatenate([e_same * bbr[k:k + 1] + e_swap * (bbi[k:k + 1] * sgn) for k in range(gsz)],
                         axis=0).astype(BF16)

    top = lax.broadcasted_iota(I32, (2 * p, t), 0) < p
    step = (lax.broadcasted_iota(I32, (2 * p, t), 1) + 1).astype(F32)
    cmag = jnp.exp(lrc * dtc * step)
    cth = lic * dtc * step
    mc, ms = cmag * jnp.cos(cth), cmag * jnp.sin(cth)
    crt, cit = crt_ref[0], cit_ref[0]
    wc = jnp.concatenate(
        [jnp.where(top, crt[:, i:i + 1] * mc - cit[:, i:i + 1] * ms,
                   -(crt[:, i:i + 1] * ms + cit[:, i:i + 1] * mc)) for i in range(gsz)],
        axis=1).astype(BF16)

    u = u_ref[0]
    y = jnp.dot(u, w_scr[...], preferred_element_type=F32)
    g = jnp.dot(u, wb, preferred_element_type=F32)
    gr, gi = g[:, :p], g[:, p:]

    m_t = jnp.exp(lrr[:, :p] * dtr[:, :p] * float(t))
    at_r = m_t * jnp.cos(lir[:, :p] * dtr[:, :p] * float(t))
    at_i = m_t * jnp.sin(lir[:, :p] * dtr[:, :p] * float(t))
    nc = u.shape[0] // bsz
    hr = jnp.zeros((bsz, p), F32)
    hi = jnp.zeros((bsz, p), F32)
    hrs, his = [hr], [hi]
    for c in range(1, nc):
        sl = slice((c - 1) * bsz, c * bsz)
        hr, hi = at_r * hr - at_i * hi + gr[sl], at_r * hi + at_i * hr + gi[sl]
        hrs.append(hr)
        his.append(hi)
    hcat = jnp.concatenate([jnp.concatenate(hrs, axis=0), jnp.concatenate(his, axis=0)], axis=1)
    y = y + jnp.dot(hcat.astype(BF16), wc, preferred_element_type=F32)
    y_ref[0] = (y + dv_ref[0] * u.astype(F32)).astype(BF16)


def _s5(u_ssm, lam_re, lam_im, log_dt, b_re, b_im, c_re, c_im, d_skip):
    bsz, seq, _ = u_ssm.shape
    t, g, gs, p = S5_CHUNK, SSM_GROUPS, SSM_GROUP, SSM_STATE
    nc = seq // t
    rows = nc * bsz
    ug = u_ssm.reshape(bsz, nc, t, g, gs).transpose(3, 1, 0, 4, 2).reshape(g, rows, gs * t)

    dup_r = lambda a: jnp.concatenate([a, a], axis=-1)[:, None, :]
    dup_c = lambda a: jnp.concatenate([a, a], axis=-1)[:, :, None]
    ld = jnp.broadcast_to(log_dt[:, None], (g, p))
    brt = jnp.swapaxes(b_re, 1, 2)
    bit = jnp.swapaxes(b_im, 1, 2)
    crt = jnp.swapaxes(c_re, 1, 2)
    cit = jnp.swapaxes(c_im, 1, 2)
    args = (
        ug,
        dup_r(lam_re), dup_r(lam_im), dup_r(ld), dup_c(lam_re), dup_c(lam_im), dup_c(ld),
        jnp.concatenate([brt, brt], -1), jnp.concatenate([bit, bit], -1),
        c_re, c_im, jnp.concatenate([crt, crt], 1), jnp.concatenate([cit, cit], 1),
        jnp.repeat(d_skip.reshape(g, gs), t, axis=1)[:, None, :],
    )
    per_g = lambda a: pl.BlockSpec((1,) + a.shape[1:], lambda i: (i,) + (0,) * (a.ndim - 1))
    yg = pl.pallas_call(
        functools.partial(_s5_kernel, bsz=bsz),
        out_shape=jax.ShapeDtypeStruct((g, rows, gs * t), BF16),
        grid=(g,),
        in_specs=[per_g(a) for a in args],
        out_specs=pl.BlockSpec((1, rows, gs * t), lambda i: (i, 0, 0)),
        scratch_shapes=[pltpu.VMEM((gs * t, gs * t), BF16), pltpu.VMEM((gs * gs, t), F32)],
        compiler_params=_cparams("arbitrary"),
        name="s5",
    )(*args)
    return yg.reshape(g, nc, bsz, gs, t).transpose(2, 1, 4, 0, 3).reshape(bsz, seq, g * gs)


def _dsa_kernel(qt_ref, qit_ref, wt_ref, k_ref, vt_ref, ki_ref, y_ref,
                key_scr, hi_scr, lo_scr, m_scr, acc_scr, sacc_scr, s_scr, lg_scr, p_scr, *, n_sel, idx_bits):
    tq, tk = Q_TILE, K_TILE
    qi = pl.program_id(1)
    nkt = qi + 1
    q0 = qi * tq
    diff = (lax.broadcasted_iota(I32, (tk, tq), 0) - lax.broadcasted_iota(I32, (tk, tq), 1))

    ngrp = (nkt + TILE_GROUP - 1) // TILE_GROUP
    npair = (nkt + 1) // 2
    rows = lambda kt: pl.ds(pl.multiple_of(kt * tk, tk), tk)
    filler16 = jnp.full((tk, tq), -HALF16, I16)

    def score_tile(kt, causal_mask):
        k0 = kt * tk
        ki_t = ki_ref[0, rows(kt), :]
        s = jnp.zeros((tk, tq), F32)
        for h in range(IDX_HEADS):
            rel = jnp.dot(ki_t, qit_ref[0, h * IDX_DIM:(h + 1) * IDX_DIM, :], preferred_element_type=F32)
            s = s + jnp.maximum(rel, 0.0) * wt_ref[0, h:h + 1, :]
        bits = lax.bitcast_convert_type(s, I32)
        key = bits ^ ((bits >> 31) & 0x7FFFFFFF)
        key = jnp.where(s == 0.0, 0, key)
        if causal_mask:
            key = jnp.where(diff <= q0 - k0, key, INT_MIN)
        key_scr[rows(kt), :] = key
        hi_scr[rows(kt), :] = (key >> 16).astype(I16)
        lo_scr[rows(kt), :] = ((key & 0xFFFF) - HALF16).astype(I16)

    def score_pair(j, carry):
        score_tile(2 * j, False)
        score_tile(2 * j + 1, False)
        return carry

    lax.fori_loop(0, npair - 1, score_pair, 0)
    score_tile(2 * npair - 2, True)
    score_tile(2 * npair - 1, True)

    @pl.when(2 * npair < TILE_GROUP * ngrp)
    def _():
        for kt in (2 * npair, 2 * npair + 1):
            key_scr[rows(kt), :] = jnp.full((tk, tq), INT_MIN, I32)
            hi_scr[rows(kt), :] = filler16
            lo_scr[rows(kt), :] = filler16

    def tree_sum(parts):
        while len(parts) > 1:
            parts = [a + b for a, b in zip(parts[::2], parts[1::2])]
        return parts[0]

    def count(*preds):
        def body(g, accs):
            out = list(accs)
            for u in range(TILE_GROUP):
                kt = g * TILE_GROUP + u
                keys = key_scr[rows(kt), :]
                for n, pred in enumerate(preds):
                    hit = jnp.where(pred(keys, kt * tk), 1, 0)
                    out[n] = out[n] + jnp.sum(hit.reshape(tk // SUBLANES, SUBLANES, tq), axis=0)
            return tuple(out)
        parts = lax.fori_loop(0, ngrp, body, tuple(jnp.zeros((SUBLANES, tq), I32) for _ in preds))
        return [jnp.sum(part, axis=0, keepdims=True) for part in parts]

    def count16(ref, pred):
        def body(g, acc):
            tiles = []
            for u in range(TILE_GROUP):
                hit = jnp.where(pred(ref[rows(g * TILE_GROUP + u), :]), jnp.int16(1), jnp.int16(0))
                tiles.append(tree_sum([hit[r:r + PACK16] for r in range(0, tk, PACK16)]))
            return acc + tree_sum(tiles)
        part = lax.fori_loop(0, ngrp, body, jnp.zeros((PACK16, tq), I16))
        return jnp.sum(part.astype(I32), axis=0, keepdims=True)

    def search16(ref, target):
        def bit_step(i, c_u):
            trial = c_u | (jnp.int32(1) << (15 - i))
            trial_s = (trial - HALF16).astype(I16)
            cnt = count16(ref, lambda v: v >= trial_s)
            return jnp.where(cnt >= target, trial, c_u)
        return lax.fori_loop(0, 16, bit_step, jnp.zeros((1, tq), I32))

    hi_thr = search16(hi_scr, n_sel) - HALF16
    hi_thr16 = hi_thr.astype(I16)
    cnt_above = count16(hi_scr, lambda v: v > hi_thr16)

    def keep_bucket(g, carry):
        for u in range(TILE_GROUP):
            r = rows(g * TILE_GROUP + u)
            lo_scr[r, :] = jnp.where(hi_scr[r, :] == hi_thr16, lo_scr[r, :], jnp.int16(-HALF16))
        return carry

    lax.fori_loop(0, ngrp, keep_bucket, 0)
    lo_thr = search16(lo_scr, n_sel - cnt_above)
    thr = jnp.maximum(hi_thr * (2 * HALF16) + lo_thr, INT_MIN + 1)

    neg_bits = np.float32(NEG_BIG).view(np.int32)

    def to_bias(g, accs):
        n_gt, n_eq = accs
        for u in range(TILE_GROUP):
            kt = g * TILE_GROUP + u
            keys = key_scr[rows(kt), :]
            gt, eq = keys > thr, keys == thr
            key_scr[rows(kt), :] = jnp.where(gt, 0, jnp.where(eq, INT_MIN, neg_bits))
            n_gt = n_gt + jnp.sum(jnp.where(gt, 1, 0).reshape(tk // SUBLANES, SUBLANES, tq), axis=0)
            n_eq = n_eq + jnp.sum(jnp.where(eq, 1, 0).reshape(tk // SUBLANES, SUBLANES, tq), axis=0)
        return n_gt, n_eq

    zeros8 = jnp.zeros((SUBLANES, tq), I32)
    cnt_gt, cnt_eq = [jnp.sum(c, axis=0, keepdims=True) for c in lax.fori_loop(0, ngrp, to_bias, (zeros8, zeros8))]

    @pl.when(jnp.max(cnt_gt + cnt_eq) > n_sel)
    def _():
        need = n_sel - cnt_gt
        sub = lax.broadcasted_iota(I32, (tk, tq), 0)

        def idx_step(i, j_cut):
            trial = j_cut | (jnp.int32(1) << (idx_bits - 1 - i))
            cnt, = count(lambda bits, k0: (bits == INT_MIN) & (sub + k0 < trial))
            return jnp.where(cnt <= need, trial, j_cut)

        j_cut = lax.fori_loop(0, idx_bits, idx_step, jnp.zeros((1, tq), I32))

        def drop(g, carry):
            for u in range(TILE_GROUP):
                kt = g * TILE_GROUP + u
                bits = key_scr[rows(kt), :]
                key_scr[rows(kt), :] = jnp.where((bits == INT_MIN) & (sub + kt * tk >= j_cut), neg_bits, bits)
            return carry

        lax.fori_loop(0, ngrp, drop, 0)

    ta = ATTN_TILES * tk
    nstep = (nkt + ATTN_TILES - 1) // ATTN_TILES
    m_scr[...] = jnp.full(m_scr.shape, NEG_BIG, F32)
    sacc_scr[...] = jnp.full(sacc_scr.shape, NEG_BIG, F32)
    acc_scr[...] = jnp.zeros(acc_scr.shape, F32)
    ones_rows = jnp.ones((PACK16, ta), BF16)
    span = lambda i: pl.ds(pl.multiple_of(i * ta, ta), ta)

    def logits(i):
        for h in range(N_HEADS):
            g = h // KV_GROUP
            lg = (jnp.dot(k_ref[0, span(i), g * HEAD_DIM:(g + 1) * HEAD_DIM],
                          qt_ref[0, h * HEAD_DIM:(h + 1) * HEAD_DIM, :], preferred_element_type=F32)
                  + lax.bitcast_convert_type(key_scr[span(i), :], F32))
            lg_scr[h] = lg
            m_scr[h] = jnp.maximum(m_scr[h], jnp.max(lg, axis=0, keepdims=True))

    def probs():
        for h in range(N_HEADS):
            s = m_scr[h]
            s_scr[h] = s
            p_scr[h] = jnp.exp2(lg_scr[h] - s).astype(BF16)

    def accumulate(i):
        for h in range(N_HEADS):
            g = h // KV_GROUP
            vt_t = jnp.concatenate([vt_ref[0, g * HEAD_DIM:(g + 1) * HEAD_DIM, span(i)], ones_rows], axis=0)
            s_new = s_scr[h]
            alpha = jnp.exp2(sacc_scr[h] - s_new)
            acc_scr[h] = alpha * acc_scr[h] + jnp.dot(vt_t, p_scr[h], preferred_element_type=F32)
            sacc_scr[h] = s_new

    logits(0)

    @pl.when(nstep == 1)
    def _():
        probs()
        accumulate(0)

    @pl.when(nstep > 1)
    def _():
        probs()
        logits(1)

        def step(i, carry):
            accumulate(i - 1)
            probs()
            logits(i + 1)
            return carry

        lax.fori_loop(1, nstep - 1, step, 0)
        accumulate(nstep - 2)
        probs()
        accumulate(nstep - 1)

    for h in range(N_HEADS):
        o = acc_scr[h, :HEAD_DIM] / acc_scr[h, HEAD_DIM:HEAD_DIM + 1]
        y_ref[0, :, h * HEAD_DIM:(h + 1) * HEAD_DIM] = o.T.astype(BF16)


def _dsa(qt, qit, wtt, k, vt, ki):
    bsz, seq, _ = k.shape
    n_sel = min(TOPK_MAX, seq // 4)
    tq = Q_TILE
    assert Q_TILE == K_TILE and seq % (2 * K_TILE) == 0
    pad_rows = -(-seq // (TILE_GROUP * K_TILE)) * TILE_GROUP * K_TILE
    kern = functools.partial(_dsa_kernel, n_sel=n_sel, idx_bits=seq.bit_length())
    return pl.pallas_call(
        kern,
        out_shape=jax.ShapeDtypeStruct((bsz, seq, ATTN_WIDTH), BF16),
        grid=(bsz, seq // tq),
        in_specs=[
            pl.BlockSpec((1, ATTN_WIDTH, tq), lambda b, i: (b, 0, i)),
            pl.BlockSpec((1, IDX_WIDTH, tq), lambda b, i: (b, 0, i)),
            pl.BlockSpec((1, IDX_HEADS, tq), lambda b, i: (b, 0, i)),
            pl.BlockSpec((1, seq, KV_WIDTH), lambda b, i: (b, 0, 0)),
            pl.BlockSpec((1, KV_WIDTH, seq), lambda b, i: (b, 0, 0)),
            pl.BlockSpec((1, seq, IDX_DIM), lambda b, i: (b, 0, 0)),
        ],
        out_specs=pl.BlockSpec((1, tq, ATTN_WIDTH), lambda b, i: (b, i, 0)),
        scratch_shapes=[
            pltpu.VMEM((pad_rows, tq), I32),
            pltpu.VMEM((pad_rows, tq), I16),
            pltpu.VMEM((pad_rows, tq), I16),
            pltpu.VMEM((N_HEADS, 1, tq), F32),
            pltpu.VMEM((N_HEADS, HEAD_DIM + PACK16, tq), F32),
            pltpu.VMEM((N_HEADS, 1, tq), F32),
            pltpu.VMEM((N_HEADS, 1, tq), F32),
            pltpu.VMEM((N_HEADS, ATTN_TILES * K_TILE, tq), F32),
            pltpu.VMEM((N_HEADS, ATTN_TILES * K_TILE, tq), BF16),
        ],
        compiler_params=_cparams("arbitrary", "arbitrary"),
        name="dsa",
    )(qt, qit, wtt, k, vt, ki)


def _deepnorm_ln(x, h, gate, g, b):
    y = DEEPNORM_ALPHA * x + (1.0 + gate) * h
    mu = jnp.mean(y, axis=-1, keepdims=True)
    yc = y - mu
    var = jnp.mean(yc * yc, axis=-1, keepdims=True)
    return yc * lax.rsqrt(var + LN_EPS) * g + b


def _merge_kernel(x_ref, ys_ref, ya_ref, sg_ref, gt_ref, wglu_ref, bglu_ref, pssm_ref, pattn_ref,
                  wout_ref, lng_ref, lnb_ref, o_ref):
    y = jax.nn.gelu(ys_ref[0].astype(F32))
    glu = jnp.dot(y.astype(BF16), wglu_ref[...], preferred_element_type=F32) + bglu_ref[...]
    y = (y * jax.nn.sigmoid(glu)).astype(BF16)
    sg = sg_ref[0]
    merged = (sg[:, :D_MODEL].astype(F32) * jnp.dot(y, pssm_ref[...], preferred_element_type=F32)
              + sg[:, D_MODEL:].astype(F32) * jnp.dot(ya_ref[0], pattn_ref[...], preferred_element_type=F32))
    h = jnp.dot(merged.astype(BF16), wout_ref[...], preferred_element_type=F32)
    o_ref[0] = _deepnorm_ln(x_ref[0], h, gt_ref[0], lng_ref[...], lnb_ref[...])


def _merge(x, ys, ya, sg, gt, wglu, bglu, pssm, pattn, wout, lng, lnb, tm):
    bsz, seq, d = x.shape
    tok = lambda w: pl.BlockSpec((1, tm, w), lambda b, i: (b, i, 0))
    vec = pl.BlockSpec((1, 1, d), lambda b, i: (b, 0, 0))
    const = lambda a: pl.BlockSpec(a.shape, lambda b, i: (0, 0))
    return pl.pallas_call(
        _merge_kernel,
        out_shape=jax.ShapeDtypeStruct(x.shape, F32),
        grid=(bsz, seq // tm),
        in_specs=[tok(d), tok(SSM_WIDTH), tok(ATTN_WIDTH), tok(2 * D_MODEL), vec,
                  const(wglu), const(bglu), const(pssm), const(pattn), const(wout), const(lng), const(lnb)],
        out_specs=tok(d),
        compiler_params=_cparams("arbitrary", "arbitrary"),
        name="merge",
    )(x, ys, ya, sg, gt, wglu, bglu, pssm, pattn, wout, lng, lnb)


def _ffn_kernel(x_ref, sc_ref, sh_ref, gt_ref, wgu_ref, wd_ref, lng_ref, lnb_ref, o_ref, hid_scr):
    x = x_ref[0]
    u = (x * (1.0 + sc_ref[0]) + sh_ref[0]).astype(BF16)
    for j in range(D_FF // FF_CHUNK):
        a = jnp.dot(u, wgu_ref[:, j * FF_CHUNK:(j + 1) * FF_CHUNK], preferred_element_type=F32)
        b = jnp.dot(u, wgu_ref[:, D_FF + j * FF_CHUNK:D_FF + (j + 1) * FF_CHUNK], preferred_element_type=F32)
        hid_scr[:, j * FF_CHUNK:(j + 1) * FF_CHUNK] = (a * jax.nn.sigmoid(a) * b).astype(BF16)
    f = jnp.dot(hid_scr[...], wd_ref[...], preferred_element_type=F32)
    o_ref[0] = _deepnorm_ln(x, f, gt_ref[0], lng_ref[...], lnb_ref[...])


def _ffn(x, sc, sh, gt, wgu, wd, lng, lnb, tm):
    bsz, seq, d = x.shape
    tok = pl.BlockSpec((1, tm, d), lambda b, i: (b, i, 0))
    vec = pl.BlockSpec((1, 1, d), lambda b, i: (b, 0, 0))
    const = lambda a: pl.BlockSpec(a.shape, lambda b, i: (0, 0))
    return pl.pallas_call(
        _ffn_kernel,
        out_shape=jax.ShapeDtypeStruct(x.shape, F32),
        grid=(bsz, seq // tm),
        in_specs=[tok, vec, vec, vec, const(wgu), const(wd), const(lng), const(lnb)],
        out_specs=tok,
        scratch_shapes=[pltpu.VMEM((tm, D_FF), BF16)],
        compiler_params=_cparams("arbitrary", "arbitrary"),
        name="ffn",
    )(x, sc, sh, gt, wgu, wd, lng, lnb)


def kernel(x, c, w_cond, b_cond, w_in, ssm_lam_re, ssm_lam_im, ssm_log_dt, ssm_b_re, ssm_b_im,
           ssm_c_re, ssm_c_im, ssm_d, ssm_w_glu, ssm_b_glu, p_ssm, p_attn, w_out,
           ln1_g, ln1_b, w_gate_up, w_down, ln2_g, ln2_b):
    bsz, seq, d = x.shape
    assert d == D_MODEL and seq % Q_TILE == 0 and seq % S5_CHUNK == 0
    tm = min(512, seq)
    tabs = _rope_tables(seq)
    mod = _cond(c, w_cond, b_cond)
    row = lambda a: a.reshape(1, -1)
    for l in range(DEPTH):
        sh1, sc1, gt1, sh2, sc2, gt2 = [mod[l, :, i * d:(i + 1) * d][:, None, :] for i in range(6)]
        wn, wt = _split_w_in(w_in[l])
        u_ssm, k, ki, sg, qt, vt, qit, wtt = _inproj(x, sc1, sh1, wn, wt, tabs, tm)
        ys = _s5(u_ssm, ssm_lam_re[l], ssm_lam_im[l], ssm_log_dt[l], ssm_b_re[l], ssm_b_im[l],
                 ssm_c_re[l], ssm_c_im[l], ssm_d[l])
        ya = _dsa(qt, qit, wtt, k, vt, ki)
        x = _merge(x, ys, ya, sg, gt1, ssm_w_glu[l].astype(BF16), row(ssm_b_glu[l]),
                   p_ssm[l].astype(BF16), p_attn[l].astype(BF16), w_out[l].astype(BF16),
                   row(ln1_g[l]), row(ln1_b[l]), tm)
        x = _ffn(x, sc2, sh2, gt2, w_gate_up[l].astype(BF16), w_down[l].astype(BF16),
                 row(ln2_g[l]), row(ln2_b[l]), tm)
    return x
```

```python
import functools
import math

import jax
import jax.numpy as jnp
import numpy as np
from jax import lax
from jax.experimental import pallas as pl
from jax.experimental.pallas import tpu as pltpu

F32 = jnp.float32
BF16 = jnp.bfloat16
I32 = jnp.int32
I16 = jnp.int16
HIGHEST = lax.Precision.HIGHEST

D_MODEL = 1024
DEPTH = 2
SSM_WIDTH = 512
SSM_GROUP = 16
SSM_GROUPS = SSM_WIDTH // SSM_GROUP
SSM_STATE = 64
HEAD_DIM = 128
N_HEADS = D_MODEL // HEAD_DIM
N_KV_HEADS = 2
KV_GROUP = N_HEADS // N_KV_HEADS
ATTN_WIDTH = N_HEADS * HEAD_DIM
KV_WIDTH = N_KV_HEADS * HEAD_DIM
IDX_HEADS = 8
IDX_DIM = 64
IDX_WIDTH = IDX_HEADS * IDX_DIM
IDX_SCALE = (IDX_HEADS * IDX_DIM) ** -0.5
TOPK_MAX = 256
ROPE_THETA = 10000.0
D_FF = -(-8 * D_MODEL // (3 * 256)) * 256
DEEPNORM_ALPHA = (2 * DEPTH) ** 0.25
LN_EPS = 1e-5
IN_SIZES = (SSM_WIDTH, ATTN_WIDTH, KV_WIDTH, KV_WIDTH, IDX_WIDTH, IDX_DIM, IDX_HEADS, D_MODEL, D_MODEL)
IN_OFFSETS = [0] + [int(o) for o in np.cumsum(IN_SIZES)]

LANES = 128
SUBLANES = 8
VMEM_LIMIT_BYTES = 56 * 1024 * 1024

S5_CHUNK = LANES
Q_TILE = 256
K_TILE = 256
TILE_GROUP = 4
ATTN_TILES = 2
FF_CHUNK = 256
INT_MIN = -(2 ** 31)
NEG_BIG = -1e30
HALF16 = 2 ** 15
PACK16 = 2 * SUBLANES
Q_SCALE = HEAD_DIM ** -0.5 * math.log2(math.e)


def _cparams(*sem):
    return pltpu.CompilerParams(dimension_semantics=sem, vmem_limit_bytes=VMEM_LIMIT_BYTES)


def _cond_kernel(c_ref, w_ref, b_ref, o_ref):
    c = c_ref[...]
    s = c * jax.nn.sigmoid(c)
    o_ref[0] = jnp.dot(s, w_ref[0], preferred_element_type=F32, precision=HIGHEST) + b_ref[0]


def _cond(c, w_cond, b_cond):
    depth, d, n = w_cond.shape
    bsz = c.shape[0]
    tn = 1536
    return pl.pallas_call(
        _cond_kernel,
        out_shape=jax.ShapeDtypeStruct((depth, bsz, n), F32),
        grid=(depth, n // tn),
        in_specs=[
            pl.BlockSpec((bsz, d), lambda l, j: (0, 0)),
            pl.BlockSpec((1, d, tn), lambda l, j: (l, 0, j)),
            pl.BlockSpec((1, 1, tn), lambda l, j: (l, 0, j)),
        ],
        out_specs=pl.BlockSpec((1, bsz, tn), lambda l, j: (l, 0, j)),
        compiler_params=_cparams("arbitrary", "arbitrary"),
        name="cond",
    )(c, w_cond, b_cond.reshape(depth, 1, n))


NAT_SSM, NAT_K, NAT_KI, NAT_G = 0, 512, 768, 896
NAT_WIDTH = 2944
TR_Q, TR_V, TR_QI, TR_W = 0, 1024, 1280, 1792
TR_ROWS = 1808


def _inproj_kernel(x_ref, sc_ref, sh_ref, wn_ref, wt_ref, cn_ref, sn_ref, cki_ref, ski_ref,
                   ct_ref, st_ref, ct32_ref, st32_ref,
                   ssm_ref, k_ref, ki_ref, sg_ref, qt_ref, vt_ref, qit_ref, wtt_ref):
    u = (x_ref[0] * (1.0 + sc_ref[0]) + sh_ref[0]).astype(BF16)
    zn = jnp.dot(u, wn_ref[...], preferred_element_type=F32)
    zt = lax.dot_general(wt_ref[...], u, (((1,), (1,)), ((), ())),
                         preferred_element_type=F32)

    ssm_ref[0] = zn[:, NAT_SSM:NAT_SSM + SSM_WIDTH].astype(BF16)
    sg_ref[0] = jax.nn.sigmoid(zn[:, NAT_G:NAT_G + 2 * D_MODEL]).astype(BF16)

    cn, sn = cn_ref[...], sn_ref[...]
    for h in range(N_KV_HEADS):
        xh = zn[:, NAT_K + h * HEAD_DIM:NAT_K + (h + 1) * HEAD_DIM]
        rot = pltpu.roll(xh, HEAD_DIM // 2, axis=1)
        k_ref[0, :, h * HEAD_DIM:(h + 1) * HEAD_DIM] = (xh * cn + rot * sn).astype(BF16)

    xk = zn[:, NAT_KI:NAT_KI + LANES]
    lane = lax.broadcasted_iota(I32, xk.shape, 1)
    rot = jnp.where((lane & (IDX_DIM // 2)) == 0,
                    pltpu.roll(xk, LANES - IDX_DIM // 2, axis=1),
                    pltpu.roll(xk, IDX_DIM // 2, axis=1))
    ki_ref[0] = (xk * cki_ref[...] + rot * ski_ref[...])[:, :IDX_DIM].astype(BF16)

    ct, st = ct_ref[...], st_ref[...]
    half = HEAD_DIM // 2
    for h in range(N_HEADS):
        r0 = TR_Q + h * HEAD_DIM
        x1, x2 = zt[r0:r0 + half], zt[r0 + half:r0 + HEAD_DIM]
        qt_ref[0, h * HEAD_DIM:h * HEAD_DIM + half, :] = ((x1 * ct - x2 * st) * Q_SCALE).astype(BF16)
        qt_ref[0, h * HEAD_DIM + half:(h + 1) * HEAD_DIM, :] = ((x2 * ct + x1 * st) * Q_SCALE).astype(BF16)
    ct32, st32 = ct32_ref[...], st32_ref[...]
    half = IDX_DIM // 2
    for h in range(IDX_HEADS):
        r0 = TR_QI + h * IDX_DIM
        x1, x2 = zt[r0:r0 + half], zt[r0 + half:r0 + IDX_DIM]
        qit_ref[0, h * IDX_DIM:h * IDX_DIM + half, :] = (x1 * ct32 - x2 * st32).astype(BF16)
        qit_ref[0, h * IDX_DIM + half:(h + 1) * IDX_DIM, :] = (x2 * ct32 + x1 * st32).astype(BF16)

    vt_ref[0] = zt[TR_V:TR_V + KV_WIDTH].astype(BF16)
    wtt_ref[0] = zt[TR_W:TR_W + IDX_HEADS] * IDX_SCALE


def _rope_tables(seq):
    pos = jnp.arange(seq, dtype=F32)

    def cs(half):
        inv = ROPE_THETA ** (-jnp.arange(half, dtype=F32) / half)
        ang = pos[:, None] * inv[None, :]
        return jnp.cos(ang), jnp.sin(ang)

    c64, s64 = cs(HEAD_DIM // 2)
    c32, s32 = cs(IDX_DIM // 2)
    z = jnp.zeros((seq, LANES - IDX_DIM), F32)
    return dict(
        cn=jnp.concatenate([c64, c64], -1), sn=jnp.concatenate([-s64, s64], -1),
        cki=jnp.concatenate([c32, c32, z], -1), ski=jnp.concatenate([-s32, s32, z], -1),
        ct=c64.T, st=s64.T, ct32=c32.T, st32=s32.T)


def _split_w_in(w_in):
    o = IN_OFFSETS
    ssm, q, k, v, qi, ki, wi, gs, ga = [w_in[:, o[i]:o[i + 1]] for i in range(9)]
    zpad = jnp.zeros((D_MODEL, LANES - IDX_DIM), w_in.dtype)
    wn = jnp.concatenate([ssm, k, ki, zpad, gs, ga], axis=1).astype(BF16)
    wt = jnp.concatenate([q, v, qi, wi, jnp.zeros((D_MODEL, TR_ROWS - TR_W - IDX_HEADS), w_in.dtype)],
                         axis=1).T.astype(BF16)
    return wn, wt


def _inproj(x, sc, sh, wn, wt, tabs, tm):
    bsz, seq, d = x.shape
    grid = (bsz, seq // tm)
    tok = lambda w: pl.BlockSpec((1, tm, w), lambda b, i: (b, i, 0))
    trn = lambda r: pl.BlockSpec((1, r, tm), lambda b, i: (b, 0, i))
    vec = pl.BlockSpec((1, 1, d), lambda b, i: (b, 0, 0))
    const = lambda a: pl.BlockSpec(a.shape, lambda b, i: (0, 0))
    ntab = lambda: pl.BlockSpec((tm, LANES), lambda b, i: (i, 0))
    ttab = lambda r: pl.BlockSpec((r, tm), lambda b, i: (0, i))
    out_shape = (
        jax.ShapeDtypeStruct((bsz, seq, SSM_WIDTH), BF16),
        jax.ShapeDtypeStruct((bsz, seq, KV_WIDTH), BF16),
        jax.ShapeDtypeStruct((bsz, seq, IDX_DIM), BF16),
        jax.ShapeDtypeStruct((bsz, seq, 2 * D_MODEL), BF16),
        jax.ShapeDtypeStruct((bsz, ATTN_WIDTH, seq), BF16),
        jax.ShapeDtypeStruct((bsz, KV_WIDTH, seq), BF16),
        jax.ShapeDtypeStruct((bsz, IDX_WIDTH, seq), BF16),
        jax.ShapeDtypeStruct((bsz, IDX_HEADS, seq), F32),
    )
    return pl.pallas_call(
        _inproj_kernel,
        out_shape=out_shape,
        grid=grid,
        in_specs=[tok(d), vec, vec, const(wn), const(wt), ntab(), ntab(), ntab(), ntab(),
                  ttab(HEAD_DIM // 2), ttab(HEAD_DIM // 2), ttab(IDX_DIM // 2), ttab(IDX_DIM // 2)],
        out_specs=(tok(SSM_WIDTH), tok(KV_WIDTH), tok(IDX_DIM), tok(2 * D_MODEL),
                   trn(ATTN_WIDTH), trn(KV_WIDTH), trn(IDX_WIDTH), trn(IDX_HEADS)),
        compiler_params=_cparams("arbitrary", "arbitrary"),
        name="in_proj",
    )(x, sc, sh, wn, wt, tabs["cn"], tabs["sn"], tabs["cki"], tabs["ski"],
      tabs["ct"], tabs["st"], tabs["ct32"], tabs["st32"])


def _s5_kernel(u_ref, lrr_ref, lir_ref, ldr_ref, lrc_ref, lic_ref, ldc_ref,
               brt_ref, bit_ref, cr_ref, ci_ref, crt_ref, cit_ref, dv_ref,
               y_ref, w_scr, kv_scr, *, bsz):
    t = S5_CHUNK
    p = SSM_STATE
    gsz = SSM_GROUP
    lrr, lir = lrr_ref[0], lir_ref[0]
    dtr = jnp.exp(ldr_ref[0])
    lrc, lic = lrc_ref[0], lic_ref[0]
    dtc = jnp.exp(ldc_ref[0])

    mag = jnp.exp(lrr * dtr)
    ar, ai = mag * jnp.cos(lir * dtr), mag * jnp.sin(lir * dtr)
    den = lrr * lrr + lir * lir
    nr = ar - 1.0
    fr, fi = (nr * lrr + ai * lir) / den, (ai * lrr - nr * lir) / den
    brt, bit = brt_ref[0], bit_ref[0]
    bbr, bbi = fr * brt - fi * bit, fr * bit + fi * brt

    cr, ci = cr_ref[0], ci_ref[0]
    bbr_h, bbi_h = bbr[:, :p], bbi[:, :p]
    cbr = jnp.concatenate([cr[i:i + 1] * bbr_h - ci[i:i + 1] * bbi_h for i in range(gsz)], axis=0)
    cbi = jnp.concatenate([cr[i:i + 1] * bbi_h + ci[i:i + 1] * bbr_h for i in range(gsz)], axis=0)
    lag = lax.broadcasted_iota(I32, (p, t), 1).astype(F32)
    lrc_h, lic_h, dtc_h = lrc[:p], lic[:p], dtc[:p]
    pmag = jnp.exp(lrc_h * dtc_h * lag)
    pth = lic_h * dtc_h * lag
    kv_scr[...] = (jnp.dot(cbr, pmag * jnp.cos(pth), preferred_element_type=F32, precision=HIGHEST)
                   - jnp.dot(cbi, pmag * jnp.sin(pth), preferred_element_type=F32, precision=HIGHEST))

    row = lax.broadcasted_iota(I32, (t, t), 0)
    col = lax.broadcasted_iota(I32, (t, t), 1)
    causal = col >= row

    def toeplitz_cols(i, carry):
        c0 = pl.multiple_of(i * t, t)
        for k in range(gsz):
            vec = kv_scr[pl.ds(i * gsz + k, 1), :]
            blk = pltpu.roll(jnp.broadcast_to(vec, (t, t)), 0, 1, stride=1, stride_axis=0)
            w_scr[k * t:(k + 1) * t, pl.ds(c0, t)] = jnp.where(causal, blk, 0.0).astype(BF16)
        return carry

    lax.fori_loop(0, gsz, toeplitz_cols, 0)

    lo = lax.broadcasted_iota(I32, (t, 2 * p), 1) < p
    rem = (t - 1 - lax.broadcasted_iota(I32, (t, 2 * p), 0)).astype(F32)
    emag = jnp.exp(lrr * dtr * rem)
    eth = lir * dtr * rem
    ecs, esn = emag * jnp.cos(eth), emag * jnp.sin(eth)
    e_same, e_swap = jnp.where(lo, ecs, esn), jnp.where(lo, esn, ecs)
    sgn = jnp.where(lo[:1], -1.0, 1.0)
    wb = jnp.concatenate([e_same * bbr[k:k + 1] + e_swap * (bbi[k:k + 1] * sgn) for k in range(gsz)],
                         axis=0).astype(BF16)

    top = lax.broadcasted_iota(I32, (2 * p, t), 0) < p
    step = (lax.broadcasted_iota(I32, (2 * p, t), 1) + 1).astype(F32)
    cmag = jnp.exp(lrc * dtc * step)
    cth = lic * dtc * step
    mc, ms = cmag * jnp.cos(cth), cmag * jnp.sin(cth)
    crt, cit = crt_ref[0], cit_ref[0]
    wc = jnp.concatenate(
        [jnp.where(top, crt[:, i:i + 1] * mc - cit[:, i:i + 1] * ms,
                   -(crt[:, i:i + 1] * ms + cit[:, i:i + 1] * mc)) for i in range(gsz)],
        axis=1).astype(BF16)

    u = u_ref[0]
    y = jnp.dot(u, w_scr[...], preferred_element_type=F32)
    g = jnp.dot(u, wb, preferred_element_type=F32)
    gr, gi = g[:, :p], g[:, p:]

    m_t = jnp.exp(lrr[:, :p] * dtr[:, :p] * float(t))
    at_r = m_t * jnp.cos(lir[:, :p] * dtr[:, :p] * float(t))
    at_i = m_t * jnp.sin(lir[:, :p] * dtr[:, :p] * float(t))
    nc = u.shape[0] // bsz
    hr = jnp.zeros((bsz, p), F32)
    hi = jnp.zeros((bsz, p), F32)
    hrs, his = [hr], [hi]
    for c in range(1, nc):
        sl = slice((c - 1) * bsz, c * bsz)
        hr, hi = at_r * hr - at_i * hi + gr[sl], at_r * hi + at_i * hr + gi[sl]
        hrs.append(hr)
        his.append(hi)
    hcat = jnp.concatenate([jnp.concatenate(hrs, axis=0), jnp.concatenate(his, axis=0)], axis=1)
    y = y + jnp.dot(hcat.astype(BF16), wc, preferred_element_type=F32)
    y_ref[0] = (y + dv_ref[0] * u.astype(F32)).astype(BF16)


def _s5(u_ssm, lam_re, lam_im, log_dt, b_re, b_im, c_re, c_im, d_skip):
    bsz, seq, _ = u_ssm.shape
    t, g, gs, p = S5_CHUNK, SSM_GROUPS, SSM_GROUP, SSM_STATE
    nc = seq // t
    rows = nc * bsz
    ug = u_ssm.reshape(bsz, nc, t, g, gs).transpose(3, 1, 0, 4, 2).reshape(g, rows, gs * t)

    dup_r = lambda a: jnp.concatenate([a, a], axis=-1)[:, None, :]
    dup_c = lambda a: jnp.concatenate([a, a], axis=-1)[:, :, None]
    ld = jnp.broadcast_to(log_dt[:, None], (g, p))
    brt = jnp.swapaxes(b_re, 1, 2)
    bit = jnp.swapaxes(b_im, 1, 2)
    crt = jnp.swapaxes(c_re, 1, 2)
    cit = jnp.swapaxes(c_im, 1, 2)
    args = (
        ug,
        dup_r(lam_re), dup_r(lam_im), dup_r(ld), dup_c(lam_re), dup_c(lam_im), dup_c(ld),
        jnp.concatenate([brt, brt], -1), jnp.concatenate([bit, bit], -1),
        c_re, c_im, jnp.concatenate([crt, crt], 1), jnp.concatenate([cit, cit], 1),
        jnp.repeat(d_skip.reshape(g, gs), t, axis=1)[:, None, :],
    )
    per_g = lambda a: pl.BlockSpec((1,) + a.shape[1:], lambda i: (i,) + (0,) * (a.ndim - 1))
    yg = pl.pallas_call(
        functools.partial(_s5_kernel, bsz=bsz),
        out_shape=jax.ShapeDtypeStruct((g, rows, gs * t), BF16),
        grid=(g,),
        in_specs=[per_g(a) for a in args],
        out_specs=pl.BlockSpec((1, rows, gs * t), lambda i: (i, 0, 0)),
        scratch_shapes=[pltpu.VMEM((gs * t, gs * t), BF16), pltpu.VMEM((gs * gs, t), F32)],
        compiler_params=_cparams("arbitrary"),
        name="s5",
    )(*args)
    return yg.reshape(g, nc, bsz, gs, t).transpose(2, 1, 4, 0, 3).reshape(bsz, seq, g * gs)


def _dsa_kernel(qt_ref, qit_ref, wt_ref, k_ref, vt_ref, ki_ref, y_ref,
                key_scr, hi_scr, lo_scr, m_scr, acc_scr, sacc_scr, s_scr, lg_scr, p_scr, *, n_sel, idx_bits):
    tq, tk = Q_TILE, K_TILE
    qi = pl.program_id(1)
    nkt = qi + 1
    q0 = qi * tq
    diff = (lax.broadcasted_iota(I32, (tk, tq), 0) - lax.broadcasted_iota(I32, (tk, tq), 1))

    ngrp = (nkt + TILE_GROUP - 1) // TILE_GROUP
    npair = (nkt + 1) // 2
    rows = lambda kt: pl.ds(pl.multiple_of(kt * tk, tk), tk)
    filler16 = jnp.full((tk, tq), -HALF16, I16)

    def score_tile(kt, causal_mask):
        k0 = kt * tk
        ki_t = ki_ref[0, rows(kt), :]
        s = jnp.zeros((tk, tq), F32)
        for h in range(IDX_HEADS):
            rel = jnp.dot(ki_t, qit_ref[0, h * IDX_DIM:(h + 1) * IDX_DIM, :], preferred_element_type=F32)
            s = s + jnp.maximum(rel, 0.0) * wt_ref[0, h:h + 1, :]
        bits = lax.bitcast_convert_type(s, I32)
        sign = bits >> 31
        key = ((bits & 0x7FFFFFFF) ^ sign) - sign
        if causal_mask:
            key = jnp.where(diff <= q0 - k0, key, INT_MIN)
        key_scr[rows(kt), :] = key
        hi_scr[rows(kt), :] = (key >> 16).astype(I16)
        lo_scr[rows(kt), :] = ((key & 0xFFFF) - HALF16).astype(I16)

    def score_pair(j, carry):
        score_tile(2 * j, False)
        score_tile(2 * j + 1, False)
        return carry

    lax.fori_loop(0, npair - 1, score_pair, 0)
    score_tile(2 * npair - 2, True)
    score_tile(2 * npair - 1, True)

    @pl.when(2 * npair < TILE_GROUP * ngrp)
    def _():
        for kt in (2 * npair, 2 * npair + 1):
            key_scr[rows(kt), :] = jnp.full((tk, tq), INT_MIN, I32)
            hi_scr[rows(kt), :] = filler16
            lo_scr[rows(kt), :] = filler16

    def tree_sum(parts):
        while len(parts) > 1:
            parts = [a + b for a, b in zip(parts[::2], parts[1::2])]
        return parts[0]

    def count(*preds):
        def body(g, accs):
            out = list(accs)
            for u in range(TILE_GROUP):
                kt = g * TILE_GROUP + u
                keys = key_scr[rows(kt), :]
                for n, pred in enumerate(preds):
                    hit = jnp.where(pred(keys, kt * tk), 1, 0)
                    out[n] = out[n] + jnp.sum(hit.reshape(tk // SUBLANES, SUBLANES, tq), axis=0)
            return tuple(out)
        parts = lax.fori_loop(0, ngrp, body, tuple(jnp.zeros((SUBLANES, tq), I32) for _ in preds))
        return [jnp.sum(part, axis=0, keepdims=True) for part in parts]

    def count16(ref, pred, with_tile=False):
        def body(g, acc):
            tiles = []
            for u in range(TILE_GROUP):
                kt = g * TILE_GROUP + u
                vals = ref[rows(kt), :]
                hit = jnp.where(pred(vals, kt) if with_tile else pred(vals), jnp.int16(1), jnp.int16(0))
                tiles.append(tree_sum([hit[r:r + PACK16] for r in range(0, tk, PACK16)]))
            return acc + tree_sum(tiles)
        part = lax.fori_loop(0, ngrp, body, jnp.zeros((PACK16, tq), I16))
        return jnp.sum(part.astype(I32), axis=0, keepdims=True)

    def search16(ref, target):
        def bit_step(i, state):
            c_u, c_cnt = state
            trial = c_u | (jnp.int32(1) << (15 - i))
            trial_s = (trial - HALF16).astype(I16)
            cnt = count16(ref, lambda v: v >= trial_s)
            ok = cnt >= target
            return jnp.where(ok, trial, c_u), jnp.where(ok, cnt, c_cnt)
        zero = jnp.zeros((1, tq), I32)
        return lax.fori_loop(0, 16, bit_step, (zero, zero))

    hi_u, n_hi_ge = search16(hi_scr, n_sel)
    hi_thr = hi_u - HALF16
    hi_thr16 = hi_thr.astype(I16)

    def keep_bucket(hi, kt):
        lo_scr[rows(kt), :] = jnp.where(hi == hi_thr16, lo_scr[rows(kt), :], jnp.int16(-HALF16))
        return hi > hi_thr16

    cnt_above = count16(hi_scr, keep_bucket, with_tile=True)
    lo_thr, n_lo_ge = search16(lo_scr, n_sel - cnt_above)
    thr = jnp.maximum(hi_thr * (2 * HALF16) + lo_thr, INT_MIN + 1)
    in_bucket = jnp.where(lo_thr != 0, n_lo_ge, n_hi_ge - cnt_above)
    cnt_ge = jnp.where(hi_u != 0, cnt_above + in_bucket, 0)

    neg_bits = np.float32(NEG_BIG).view(np.int32)

    def to_bias(g, carry):
        for u in range(TILE_GROUP):
            kt = g * TILE_GROUP + u
            keys = key_scr[rows(kt), :]
            key_scr[rows(kt), :] = jnp.where(keys > thr, 0, jnp.where(keys == thr, INT_MIN, neg_bits))
        return carry

    lax.fori_loop(0, ngrp, to_bias, 0)

    @pl.when(jnp.max(cnt_ge) > n_sel)
    def _():
        cnt_gt, = count(lambda bits, k0: bits == 0)
        need = n_sel - cnt_gt
        sub = lax.broadcasted_iota(I32, (tk, tq), 0)

        def idx_step(i, j_cut):
            trial = j_cut | (jnp.int32(1) << (idx_bits - 1 - i))
            cnt, = count(lambda bits, k0: (bits == INT_MIN) & (sub + k0 < trial))
            return jnp.where(cnt <= need, trial, j_cut)

        j_cut = lax.fori_loop(0, idx_bits, idx_step, jnp.zeros((1, tq), I32))

        def drop(g, carry):
            for u in range(TILE_GROUP):
                kt = g * TILE_GROUP + u
                bits = key_scr[rows(kt), :]
                key_scr[rows(kt), :] = jnp.where((bits == INT_MIN) & (sub + kt * tk >= j_cut), neg_bits, bits)
            return carry

        lax.fori_loop(0, ngrp, drop, 0)

    ta = ATTN_TILES * tk
    nstep = (nkt + ATTN_TILES - 1) // ATTN_TILES
    m_scr[...] = jnp.full(m_scr.shape, NEG_BIG, F32)
    sacc_scr[...] = jnp.full(sacc_scr.shape, NEG_BIG, F32)
    acc_scr[...] = jnp.zeros(acc_scr.shape, F32)
    ones_rows = jnp.ones((PACK16, ta), BF16)
    span = lambda i: pl.ds(pl.multiple_of(i * ta, ta), ta)

    def logits(i):
        for h in range(N_HEADS):
            g = h // KV_GROUP
            lg = (jnp.dot(k_ref[0, span(i), g * HEAD_DIM:(g + 1) * HEAD_DIM],
                          qt_ref[0, h * HEAD_DIM:(h + 1) * HEAD_DIM, :], preferred_element_type=F32)
                  + lax.bitcast_convert_type(key_scr[span(i), :], F32))
            lg_scr[h] = lg
            m_scr[h] = jnp.maximum(m_scr[h], jnp.max(lg, axis=0, keepdims=True))

    def probs():
        for h in range(N_HEADS):
            s = m_scr[h]
            s_scr[h] = s
            p_scr[h] = jnp.exp2(lg_scr[h] - s).astype(BF16)

    def accumulate(i):
        for h in range(N_HEADS):
            g = h // KV_GROUP
            vt_t = jnp.concatenate([vt_ref[0, g * HEAD_DIM:(g + 1) * HEAD_DIM, span(i)], ones_rows], axis=0)
            s_new = s_scr[h]
            alpha = jnp.exp2(sacc_scr[h] - s_new)
            acc_scr[h] = alpha * acc_scr[h] + jnp.dot(vt_t, p_scr[h], preferred_element_type=F32)
            sacc_scr[h] = s_new

    logits(0)

    @pl.when(nstep == 1)
    def _():
        probs()
        accumulate(0)

    @pl.when(nstep > 1)
    def _():
        probs()
        logits(1)

        def step(i, carry):
            accumulate(i - 1)
            probs()
            logits(i + 1)
            return carry

        lax.fori_loop(1, nstep - 1, step, 0)
        accumulate(nstep - 2)
        probs()
        accumulate(nstep - 1)

    for h in range(N_HEADS):
        o = acc_scr[h, :HEAD_DIM] / acc_scr[h, HEAD_DIM:HEAD_DIM + 1]
        y_ref[0, :, h * HEAD_DIM:(h + 1) * HEAD_DIM] = o.T.astype(BF16)


def _dsa(qt, qit, wtt, k, vt, ki):
    bsz, seq, _ = k.shape
    n_sel = min(TOPK_MAX, seq // 4)
    tq = Q_TILE
    assert Q_TILE == K_TILE and seq % (2 * K_TILE) == 0
    pad_rows = -(-seq // (TILE_GROUP * K_TILE)) * TILE_GROUP * K_TILE
    kern = functools.partial(_dsa_kernel, n_sel=n_sel, idx_bits=seq.bit_length())
    return pl.pallas_call(
        kern,
        out_shape=jax.ShapeDtypeStruct((bsz, seq, ATTN_WIDTH), BF16),
        grid=(bsz, seq // tq),
        in_specs=[
            pl.BlockSpec((1, ATTN_WIDTH, tq), lambda b, i: (b, 0, i)),
            pl.BlockSpec((1, IDX_WIDTH, tq), lambda b, i: (b, 0, i)),
            pl.BlockSpec((1, IDX_HEADS, tq), lambda b, i: (b, 0, i)),
            pl.BlockSpec((1, seq, KV_WIDTH), lambda b, i: (b, 0, 0)),
            pl.BlockSpec((1, KV_WIDTH, seq), lambda b, i: (b, 0, 0)),
            pl.BlockSpec((1, seq, IDX_DIM), lambda b, i: (b, 0, 0)),
        ],
        out_specs=pl.BlockSpec((1, tq, ATTN_WIDTH), lambda b, i: (b, i, 0)),
        scratch_shapes=[
            pltpu.VMEM((pad_rows, tq), I32),
            pltpu.VMEM((pad_rows, tq), I16),
            pltpu.VMEM((pad_rows, tq), I16),
            pltpu.VMEM((N_HEADS, 1, tq), F32),
            pltpu.VMEM((N_HEADS, HEAD_DIM + PACK16, tq), F32),
            pltpu.VMEM((N_HEADS, 1, tq), F32),
            pltpu.VMEM((N_HEADS, 1, tq), F32),
            pltpu.VMEM((N_HEADS, ATTN_TILES * K_TILE, tq), F32),
            pltpu.VMEM((N_HEADS, ATTN_TILES * K_TILE, tq), BF16),
        ],
        compiler_params=_cparams("arbitrary", "arbitrary"),
        name="dsa",
    )(qt, qit, wtt, k, vt, ki)


def _deepnorm_ln(x, h, gate, g, b):
    y = DEEPNORM_ALPHA * x + (1.0 + gate) * h
    mu = jnp.mean(y, axis=-1, keepdims=True)
    yc = y - mu
    var = jnp.mean(yc * yc, axis=-1, keepdims=True)
    return yc * lax.rsqrt(var + LN_EPS) * g + b


def _merge_kernel(x_ref, ys_ref, ya_ref, sg_ref, gt_ref, wglu_ref, bglu_ref, pssm_ref, pattn_ref,
                  wout_ref, lng_ref, lnb_ref, o_ref):
    y = jax.nn.gelu(ys_ref[0].astype(F32))
    glu = jnp.dot(y.astype(BF16), wglu_ref[...], preferred_element_type=F32) + bglu_ref[...]
    y = (y * jax.nn.sigmoid(glu)).astype(BF16)
    sg = sg_ref[0]
    merged = (sg[:, :D_MODEL].astype(F32) * jnp.dot(y, pssm_ref[...], preferred_element_type=F32)
              + sg[:, D_MODEL:].astype(F32) * jnp.dot(ya_ref[0], pattn_ref[...], preferred_element_type=F32))
    h = jnp.dot(merged.astype(BF16), wout_ref[...], preferred_element_type=F32)
    o_ref[0] = _deepnorm_ln(x_ref[0], h, gt_ref[0], lng_ref[...], lnb_ref[...])


def _merge(x, ys, ya, sg, gt, wglu, bglu, pssm, pattn, wout, lng, lnb, tm):
    bsz, seq, d = x.shape
    tok = lambda w: pl.BlockSpec((1, tm, w), lambda b, i: (b, i, 0))
    vec = pl.BlockSpec((1, 1, d), lambda b, i: (b, 0, 0))
    const = lambda a: pl.BlockSpec(a.shape, lambda b, i: (0, 0))
    return pl.pallas_call(
        _merge_kernel,
        out_shape=jax.ShapeDtypeStruct(x.shape, F32),
        grid=(bsz, seq // tm),
        in_specs=[tok(d), tok(SSM_WIDTH), tok(ATTN_WIDTH), tok(2 * D_MODEL), vec,
                  const(wglu), const(bglu), const(pssm), const(pattn), const(wout), const(lng), const(lnb)],
        out_specs=tok(d),
        compiler_params=_cparams("arbitrary", "arbitrary"),
        name="merge",
    )(x, ys, ya, sg, gt, wglu, bglu, pssm, pattn, wout, lng, lnb)


def _ffn_kernel(x_ref, sc_ref, sh_ref, gt_ref, wgu_ref, wd_ref, lng_ref, lnb_ref, o_ref, hid_scr):
    x = x_ref[0]
    u = (x * (1.0 + sc_ref[0]) + sh_ref[0]).astype(BF16)
    for j in range(D_FF // FF_CHUNK):
        a = jnp.dot(u, wgu_ref[:, j * FF_CHUNK:(j + 1) * FF_CHUNK], preferred_element_type=F32)
        b = jnp.dot(u, wgu_ref[:, D_FF + j * FF_CHUNK:D_FF + (j + 1) * FF_CHUNK], preferred_element_type=F32)
        hid_scr[:, j * FF_CHUNK:(j + 1) * FF_CHUNK] = (a * jax.nn.sigmoid(a) * b).astype(BF16)
    f = jnp.dot(hid_scr[...], wd_ref[...], preferred_element_type=F32)
    o_ref[0] = _deepnorm_ln(x, f, gt_ref[0], lng_ref[...], lnb_ref[...])


def _ffn(x, sc, sh, gt, wgu, wd, lng, lnb, tm):
    bsz, seq, d = x.shape
    tok = pl.BlockSpec((1, tm, d), lambda b, i: (b, i, 0))
    vec = pl.BlockSpec((1, 1, d), lambda b, i: (b, 0, 0))
    const = lambda a: pl.BlockSpec(a.shape, lambda b, i: (0, 0))
    return pl.pallas_call(
        _ffn_kernel,
        out_shape=jax.ShapeDtypeStruct(x.shape, F32),
        grid=(bsz, seq // tm),
        in_specs=[tok, vec, vec, vec, const(wgu), const(wd), const(lng), const(lnb)],
        out_specs=tok,
        scratch_shapes=[pltpu.VMEM((tm, D_FF), BF16)],
        compiler_params=_cparams("arbitrary", "arbitrary"),
        name="ffn",
    )(x, sc, sh, gt, wgu, wd, lng, lnb)


def kernel(x, c, w_cond, b_cond, w_in, ssm_lam_re, ssm_lam_im, ssm_log_dt, ssm_b_re, ssm_b_im,
           ssm_c_re, ssm_c_im, ssm_d, ssm_w_glu, ssm_b_glu, p_ssm, p_attn, w_out,
           ln1_g, ln1_b, w_gate_up, w_down, ln2_g, ln2_b):
    bsz, seq, d = x.shape
    assert d == D_MODEL and seq % Q_TILE == 0 and seq % S5_CHUNK == 0
    tm = min(512, seq)
    tabs = _rope_tables(seq)
    mod = _cond(c, w_cond, b_cond)
    row = lambda a: a.reshape(1, -1)
    for l in range(DEPTH):
        sh1, sc1, gt1, sh2, sc2, gt2 = [mod[l, :, i * d:(i + 1) * d][:, None, :] for i in range(6)]
        wn, wt = _split_w_in(w_in[l])
        u_ssm, k, ki, sg, qt, vt, qit, wtt = _inproj(x, sc1, sh1, wn, wt, tabs, tm)
        ys = _s5(u_ssm, ssm_lam_re[l], ssm_lam_im[l], ssm_log_dt[l], ssm_b_re[l], ssm_b_im[l],
                 ssm_c_re[l], ssm_c_im[l], ssm_d[l])
        ya = _dsa(qt, qit, wtt, k, vt, ki)
        x = _merge(x, ys, ya, sg, gt1, ssm_w_glu[l].astype(BF16), row(ssm_b_glu[l]),
                   p_ssm[l].astype(BF16), p_attn[l].astype(BF16), w_out[l].astype(BF16),
                   row(ln1_g[l]), row(ln1_b[l]), tm)
        x = _ffn(x, sc2, sh2, gt2, w_gate_up[l].astype(BF16), w_down[l].astype(BF16),
                 row(ln2_g[l]), row(ln2_b[l]), tm)
    return x
```

```python
import functools
import math

import jax
import jax.numpy as jnp
import numpy as np
from jax import lax
from jax.experimental import pallas as pl
from jax.experimental.pallas import tpu as pltpu

F32 = jnp.float32
BF16 = jnp.bfloat16
I32 = jnp.int32
I16 = jnp.int16
HIGHEST = lax.Precision.HIGHEST

D_MODEL = 1024
DEPTH = 2
SSM_WIDTH = 512
SSM_GROUP = 16
SSM_GROUPS = SSM_WIDTH // SSM_GROUP
SSM_STATE = 64
HEAD_DIM = 128
N_HEADS = D_MODEL // HEAD_DIM
N_KV_HEADS = 2
KV_GROUP = N_HEADS // N_KV_HEADS
ATTN_WIDTH = N_HEADS * HEAD_DIM
KV_WIDTH = N_KV_HEADS * HEAD_DIM
IDX_HEADS = 8
IDX_DIM = 64
IDX_WIDTH = IDX_HEADS * IDX_DIM
IDX_SCALE = (IDX_HEADS * IDX_DIM) ** -0.5
TOPK_MAX = 256
ROPE_THETA = 10000.0
D_FF = -(-8 * D_MODEL // (3 * 256)) * 256
DEEPNORM_ALPHA = (2 * DEPTH) ** 0.25
LN_EPS = 1e-5
IN_SIZES = (SSM_WIDTH, ATTN_WIDTH, KV_WIDTH, KV_WIDTH, IDX_WIDTH, IDX_DIM, IDX_HEADS, D_MODEL, D_MODEL)
IN_OFFSETS = [0] + [int(o) for o in np.cumsum(IN_SIZES)]

LANES = 128
SUBLANES = 8
VMEM_LIMIT_BYTES = 56 * 1024 * 1024

COND_TILE = 1536
TOKEN_TILE = 512
S5_CHUNK = LANES
Q_TILE = 256
K_TILE = 256
TILE_GROUP = 4
ATTN_TILES = 2
FF_CHUNK = 256
INT_MIN = -(2 ** 31)
NEG_BIG = -1e30
HALF16 = 2 ** 15
PACK16 = 2 * SUBLANES
Q_SCALE = HEAD_DIM ** -0.5 * math.log2(math.e)


def _cparams(*sem):
    return pltpu.CompilerParams(dimension_semantics=sem, vmem_limit_bytes=VMEM_LIMIT_BYTES)


def _cond_kernel(c_ref, w_ref, b_ref, o_ref):
    c = c_ref[...]
    s = c * jax.nn.sigmoid(c)
    o_ref[0] = jnp.dot(s, w_ref[0], preferred_element_type=F32, precision=HIGHEST) + b_ref[0]


def _cond(c, w_cond, b_cond):
    depth, d, n = w_cond.shape
    bsz = c.shape[0]
    tn = COND_TILE
    assert n % tn == 0
    return pl.pallas_call(
        _cond_kernel,
        out_shape=jax.ShapeDtypeStruct((depth, bsz, n), F32),
        grid=(depth, n // tn),
        in_specs=[
            pl.BlockSpec((bsz, d), lambda l, j: (0, 0)),
            pl.BlockSpec((1, d, tn), lambda l, j: (l, 0, j)),
            pl.BlockSpec((1, 1, tn), lambda l, j: (l, 0, j)),
        ],
        out_specs=pl.BlockSpec((1, bsz, tn), lambda l, j: (l, 0, j)),
        compiler_params=_cparams("arbitrary", "arbitrary"),
        name="cond",
    )(c, w_cond, b_cond.reshape(depth, 1, n))


NAT_SSM = 0
NAT_K = NAT_SSM + SSM_WIDTH
NAT_KI = NAT_K + KV_WIDTH
NAT_G = NAT_KI + LANES
NAT_WIDTH = NAT_G + 2 * D_MODEL
TR_Q = 0
TR_V = TR_Q + ATTN_WIDTH
TR_QI = TR_V + KV_WIDTH
TR_W = TR_QI + IDX_WIDTH
TR_ROWS = -(-(TR_W + IDX_HEADS) // PACK16) * PACK16


def _inproj_kernel(x_ref, sc_ref, sh_ref, wn_ref, wt_ref, cn_ref, sn_ref, cki_ref, ski_ref,
                   ct_ref, st_ref, ct32_ref, st32_ref,
                   ssm_ref, k_ref, ki_ref, sg_ref, qt_ref, vt_ref, qit_ref, wtt_ref):
    u = (x_ref[0] * (1.0 + sc_ref[0]) + sh_ref[0]).astype(BF16)
    zn = jnp.dot(u, wn_ref[...], preferred_element_type=F32)
    zt = lax.dot_general(wt_ref[...], u, (((1,), (1,)), ((), ())),
                         preferred_element_type=F32)

    ssm_ref[0] = zn[:, NAT_SSM:NAT_SSM + SSM_WIDTH].astype(BF16)
    sg_ref[0] = jax.nn.sigmoid(zn[:, NAT_G:NAT_G + 2 * D_MODEL]).astype(BF16)

    cn, sn = cn_ref[...], sn_ref[...]
    for h in range(N_KV_HEADS):
        xh = zn[:, NAT_K + h * HEAD_DIM:NAT_K + (h + 1) * HEAD_DIM]
        rot = pltpu.roll(xh, HEAD_DIM // 2, axis=1)
        k_ref[0, :, h * HEAD_DIM:(h + 1) * HEAD_DIM] = (xh * cn + rot * sn).astype(BF16)

    xk = zn[:, NAT_KI:NAT_KI + LANES]
    lane = lax.broadcasted_iota(I32, xk.shape, 1)
    rot = jnp.where((lane & (IDX_DIM // 2)) == 0,
                    pltpu.roll(xk, LANES - IDX_DIM // 2, axis=1),
                    pltpu.roll(xk, IDX_DIM // 2, axis=1))
    ki_ref[0] = (xk * cki_ref[...] + rot * ski_ref[...])[:, :IDX_DIM].astype(BF16)

    ct, st = ct_ref[...], st_ref[...]
    half = HEAD_DIM // 2
    for h in range(N_HEADS):
        r0 = TR_Q + h * HEAD_DIM
        x1, x2 = zt[r0:r0 + half], zt[r0 + half:r0 + HEAD_DIM]
        qt_ref[0, h * HEAD_DIM:h * HEAD_DIM + half, :] = ((x1 * ct - x2 * st) * Q_SCALE).astype(BF16)
        qt_ref[0, h * HEAD_DIM + half:(h + 1) * HEAD_DIM, :] = ((x2 * ct + x1 * st) * Q_SCALE).astype(BF16)
    ct32, st32 = ct32_ref[...], st32_ref[...]
    half = IDX_DIM // 2
    for h in range(IDX_HEADS):
        r0 = TR_QI + h * IDX_DIM
        x1, x2 = zt[r0:r0 + half], zt[r0 + half:r0 + IDX_DIM]
        qit_ref[0, h * IDX_DIM:h * IDX_DIM + half, :] = (x1 * ct32 - x2 * st32).astype(BF16)
        qit_ref[0, h * IDX_DIM + half:(h + 1) * IDX_DIM, :] = (x2 * ct32 + x1 * st32).astype(BF16)

    vt_ref[0] = zt[TR_V:TR_V + KV_WIDTH].astype(BF16)
    wtt_ref[0] = zt[TR_W:TR_W + IDX_HEADS] * IDX_SCALE


def _rope_tables(seq):
    pos = jnp.arange(seq, dtype=F32)

    def cs(half):
        inv = ROPE_THETA ** (-jnp.arange(half, dtype=F32) / half)
        ang = pos[:, None] * inv[None, :]
        return jnp.cos(ang), jnp.sin(ang)

    c64, s64 = cs(HEAD_DIM // 2)
    c32, s32 = cs(IDX_DIM // 2)
    z = jnp.zeros((seq, LANES - IDX_DIM), F32)
    return dict(
        cn=jnp.concatenate([c64, c64], -1), sn=jnp.concatenate([-s64, s64], -1),
        cki=jnp.concatenate([c32, c32, z], -1), ski=jnp.concatenate([-s32, s32, z], -1),
        ct=c64.T, st=s64.T, ct32=c32.T, st32=s32.T)


def _split_w_in(w_in):
    o = IN_OFFSETS
    ssm, q, k, v, qi, ki, wi, gs, ga = [w_in[:, o[i]:o[i + 1]] for i in range(9)]
    zpad = jnp.zeros((D_MODEL, LANES - IDX_DIM), w_in.dtype)
    wn = jnp.concatenate([ssm, k, ki, zpad, gs, ga], axis=1).astype(BF16)
    wt = jnp.concatenate([q, v, qi, wi, jnp.zeros((D_MODEL, TR_ROWS - TR_W - IDX_HEADS), w_in.dtype)],
                         axis=1).T.astype(BF16)
    return wn, wt


def _inproj(x, sc, sh, wn, wt, tabs, tm):
    bsz, seq, d = x.shape
    grid = (bsz, seq // tm)
    tok = lambda w: pl.BlockSpec((1, tm, w), lambda b, i: (b, i, 0))
    trn = lambda r: pl.BlockSpec((1, r, tm), lambda b, i: (b, 0, i))
    vec = pl.BlockSpec((1, 1, d), lambda b, i: (b, 0, 0))
    const = lambda a: pl.BlockSpec(a.shape, lambda b, i: (0, 0))
    ntab = lambda: pl.BlockSpec((tm, LANES), lambda b, i: (i, 0))
    ttab = lambda r: pl.BlockSpec((r, tm), lambda b, i: (0, i))
    out_shape = (
        jax.ShapeDtypeStruct((bsz, seq, SSM_WIDTH), BF16),
        jax.ShapeDtypeStruct((bsz, seq, KV_WIDTH), BF16),
        jax.ShapeDtypeStruct((bsz, seq, IDX_DIM), BF16),
        jax.ShapeDtypeStruct((bsz, seq, 2 * D_MODEL), BF16),
        jax.ShapeDtypeStruct((bsz, ATTN_WIDTH, seq), BF16),
        jax.ShapeDtypeStruct((bsz, KV_WIDTH, seq), BF16),
        jax.ShapeDtypeStruct((bsz, IDX_WIDTH, seq), BF16),
        jax.ShapeDtypeStruct((bsz, IDX_HEADS, seq), F32),
    )
    return pl.pallas_call(
        _inproj_kernel,
        out_shape=out_shape,
        grid=grid,
        in_specs=[tok(d), vec, vec, const(wn), const(wt), ntab(), ntab(), ntab(), ntab(),
                  ttab(HEAD_DIM // 2), ttab(HEAD_DIM // 2), ttab(IDX_DIM // 2), ttab(IDX_DIM // 2)],
        out_specs=(tok(SSM_WIDTH), tok(KV_WIDTH), tok(IDX_DIM), tok(2 * D_MODEL),
                   trn(ATTN_WIDTH), trn(KV_WIDTH), trn(IDX_WIDTH), trn(IDX_HEADS)),
        compiler_params=_cparams("arbitrary", "arbitrary"),
        name="in_proj",
    )(x, sc, sh, wn, wt, tabs["cn"], tabs["sn"], tabs["cki"], tabs["ski"],
      tabs["ct"], tabs["st"], tabs["ct32"], tabs["st32"])


def _s5_kernel(u_ref, lrr_ref, lir_ref, ldr_ref, lrc_ref, lic_ref, ldc_ref,
               brt_ref, bit_ref, cr_ref, ci_ref, crt_ref, cit_ref, dv_ref,
               y_ref, w_scr, kv_scr, *, bsz):
    t = S5_CHUNK
    p = SSM_STATE
    gsz = SSM_GROUP
    lrr, lir = lrr_ref[0], lir_ref[0]
    dtr = jnp.exp(ldr_ref[0])
    lrc, lic = lrc_ref[0], lic_ref[0]
    dtc = jnp.exp(ldc_ref[0])

    mag = jnp.exp(lrr * dtr)
    ar, ai = mag * jnp.cos(lir * dtr), mag * jnp.sin(lir * dtr)
    den = lrr * lrr + lir * lir
    nr = ar - 1.0
    fr, fi = (nr * lrr + ai * lir) / den, (ai * lrr - nr * lir) / den
    brt, bit = brt_ref[0], bit_ref[0]
    bbr, bbi = fr * brt - fi * bit, fr * bit + fi * brt

    cr, ci = cr_ref[0], ci_ref[0]
    bbr_h, bbi_h = bbr[:, :p], bbi[:, :p]
    cbr = jnp.concatenate([cr[i:i + 1] * bbr_h - ci[i:i + 1] * bbi_h for i in range(gsz)], axis=0)
    cbi = jnp.concatenate([cr[i:i + 1] * bbi_h + ci[i:i + 1] * bbr_h for i in range(gsz)], axis=0)
    lag = lax.broadcasted_iota(I32, (p, t), 1).astype(F32)
    lrc_h, lic_h, dtc_h = lrc[:p], lic[:p], dtc[:p]
    pmag = jnp.exp(lrc_h * dtc_h * lag)
    pth = lic_h * dtc_h * lag
    kv_scr[...] = (jnp.dot(cbr, pmag * jnp.cos(pth), preferred_element_type=F32, precision=HIGHEST)
                   - jnp.dot(cbi, pmag * jnp.sin(pth), preferred_element_type=F32, precision=HIGHEST))

    row = lax.broadcasted_iota(I32, (t, t), 0)
    col = lax.broadcasted_iota(I32, (t, t), 1)
    causal = col >= row

    def toeplitz_cols(i, carry):
        c0 = pl.multiple_of(i * t, t)
        for k in range(gsz):
            vec = kv_scr[pl.ds(i * gsz + k, 1), :]
            blk = pltpu.roll(jnp.broadcast_to(vec, (t, t)), 0, 1, stride=1, stride_axis=0)
            w_scr[k * t:(k + 1) * t, pl.ds(c0, t)] = jnp.where(causal, blk, 0.0).astype(BF16)
        return carry

    lax.fori_loop(0, gsz, toeplitz_cols, 0)

    lo = lax.broadcasted_iota(I32, (t, 2 * p), 1) < p
    rem = (t - 1 - lax.broadcasted_iota(I32, (t, 2 * p), 0)).astype(F32)
    emag = jnp.exp(lrr * dtr * rem)
    eth = lir * dtr * rem
    ecs, esn = emag * jnp.cos(eth), emag * jnp.sin(eth)
    e_same, e_swap = jnp.where(lo, ecs, esn), jnp.where(lo, esn, ecs)
    sgn = jnp.where(lo[:1], -1.0, 1.0)
    wb = jnp.concatenate([e_same * bbr[k:k + 1] + e_swap * (bbi[k:k + 1] * sgn) for k in range(gsz)],
                         axis=0).astype(BF16)

    top = lax.broadcasted_iota(I32, (2 * p, t), 0) < p
    step = (lax.broadcasted_iota(I32, (2 * p, t), 1) + 1).astype(F32)
    cmag = jnp.exp(lrc * dtc * step)
    cth = lic * dtc * step
    mc, ms = cmag * jnp.cos(cth), cmag * jnp.sin(cth)
    crt, cit = crt_ref[0], cit_ref[0]
    wc = jnp.concatenate(
        [jnp.where(top, crt[:, i:i + 1] * mc - cit[:, i:i + 1] * ms,
                   -(crt[:, i:i + 1] * ms + cit[:, i:i + 1] * mc)) for i in range(gsz)],
        axis=1).astype(BF16)

    u = u_ref[0]
    y = jnp.dot(u, w_scr[...], preferred_element_type=F32)
    g = jnp.dot(u, wb, preferred_element_type=F32)
    gr, gi = g[:, :p], g[:, p:]

    m_t = jnp.exp(lrr[:, :p] * dtr[:, :p] * float(t))
    at_r = m_t * jnp.cos(lir[:, :p] * dtr[:, :p] * float(t))
    at_i = m_t * jnp.sin(lir[:, :p] * dtr[:, :p] * float(t))
    nc = u.shape[0] // bsz
    hr = jnp.zeros((bsz, p), F32)
    hi = jnp.zeros((bsz, p), F32)
    hrs, his = [hr], [hi]
    for c in range(1, nc):
        sl = slice((c - 1) * bsz, c * bsz)
        hr, hi = at_r * hr - at_i * hi + gr[sl], at_r * hi + at_i * hr + gi[sl]
        hrs.append(hr)
        his.append(hi)
    hcat = jnp.concatenate([jnp.concatenate(hrs, axis=0), jnp.concatenate(his, axis=0)], axis=1)
    y = y + jnp.dot(hcat.astype(BF16), wc, preferred_element_type=F32)
    y_ref[0] = (y + dv_ref[0] * u.astype(F32)).astype(BF16)


def _s5(u_ssm, lam_re, lam_im, log_dt, b_re, b_im, c_re, c_im, d_skip):
    bsz, seq, _ = u_ssm.shape
    t, g, gs, p = S5_CHUNK, SSM_GROUPS, SSM_GROUP, SSM_STATE
    nc = seq // t
    rows = nc * bsz
    ug = u_ssm.reshape(bsz, nc, t, g, gs).transpose(3, 1, 0, 4, 2).reshape(g, rows, gs * t)

    dup_r = lambda a: jnp.concatenate([a, a], axis=-1)[:, None, :]
    dup_c = lambda a: jnp.concatenate([a, a], axis=-1)[:, :, None]
    ld = jnp.broadcast_to(log_dt[:, None], (g, p))
    brt = jnp.swapaxes(b_re, 1, 2)
    bit = jnp.swapaxes(b_im, 1, 2)
    crt = jnp.swapaxes(c_re, 1, 2)
    cit = jnp.swapaxes(c_im, 1, 2)
    args = (
        ug,
        dup_r(lam_re), dup_r(lam_im), dup_r(ld), dup_c(lam_re), dup_c(lam_im), dup_c(ld),
        jnp.concatenate([brt, brt], -1), jnp.concatenate([bit, bit], -1),
        c_re, c_im, jnp.concatenate([crt, crt], 1), jnp.concatenate([cit, cit], 1),
        jnp.repeat(d_skip.reshape(g, gs), t, axis=1)[:, None, :],
    )
    per_g = lambda a: pl.BlockSpec((1,) + a.shape[1:], lambda i: (i,) + (0,) * (a.ndim - 1))
    yg = pl.pallas_call(
        functools.partial(_s5_kernel, bsz=bsz),
        out_shape=jax.ShapeDtypeStruct((g, rows, gs * t), BF16),
        grid=(g,),
        in_specs=[per_g(a) for a in args],
        out_specs=pl.BlockSpec((1, rows, gs * t), lambda i: (i, 0, 0)),
        scratch_shapes=[pltpu.VMEM((gs * t, gs * t), BF16), pltpu.VMEM((gs * gs, t), F32)],
        compiler_params=_cparams("arbitrary"),
        name="s5",
    )(*args)
    return yg.reshape(g, nc, bsz, gs, t).transpose(2, 1, 4, 0, 3).reshape(bsz, seq, g * gs)


def _dsa_kernel(qt_ref, qit_ref, wt_ref, k_ref, vt_ref, ki_ref, y_ref,
                key_scr, hi_scr, lo_scr, m_scr, acc_scr, sacc_scr, s_scr, lg_scr, p_scr, *, n_sel, idx_bits):
    tq, tk = Q_TILE, K_TILE
    qi = pl.program_id(1)
    nkt = qi + 1
    q0 = qi * tq
    diff = (lax.broadcasted_iota(I32, (tk, tq), 0) - lax.broadcasted_iota(I32, (tk, tq), 1))

    ngrp = (nkt + TILE_GROUP - 1) // TILE_GROUP
    npair = (nkt + 1) // 2
    rows = lambda kt: pl.ds(pl.multiple_of(kt * tk, tk), tk)
    filler16 = jnp.full((tk, tq), -HALF16, I16)

    def score_tile(kt, causal_mask):
        k0 = kt * tk
        ki_t = ki_ref[0, rows(kt), :]
        s = jnp.zeros((tk, tq), F32)
        for h in range(IDX_HEADS):
            rel = jnp.dot(ki_t, qit_ref[0, h * IDX_DIM:(h + 1) * IDX_DIM, :], preferred_element_type=F32)
            s = s + jnp.maximum(rel, 0.0) * wt_ref[0, h:h + 1, :]
        bits = lax.bitcast_convert_type(s, I32)
        sign = bits >> 31
        key = ((bits & 0x7FFFFFFF) ^ sign) - sign
        if causal_mask:
            key = jnp.where(diff <= q0 - k0, key, INT_MIN)
        key_scr[rows(kt), :] = key
        hi_scr[rows(kt), :] = (key >> 16).astype(I16)
        lo_scr[rows(kt), :] = ((key & 0xFFFF) - HALF16).astype(I16)

    def score_pair(j, carry):
        score_tile(2 * j, False)
        score_tile(2 * j + 1, False)
        return carry

    lax.fori_loop(0, npair - 1, score_pair, 0)
    score_tile(2 * npair - 2, True)
    score_tile(2 * npair - 1, True)

    @pl.when(2 * npair < TILE_GROUP * ngrp)
    def _():
        for kt in (2 * npair, 2 * npair + 1):
            key_scr[rows(kt), :] = jnp.full((tk, tq), INT_MIN, I32)
            hi_scr[rows(kt), :] = filler16
            lo_scr[rows(kt), :] = filler16

    def tree_sum(parts):
        while len(parts) > 1:
            parts = [a + b for a, b in zip(parts[::2], parts[1::2])]
        return parts[0]

    def count(*preds):
        def body(g, accs):
            out = list(accs)
            for u in range(TILE_GROUP):
                kt = g * TILE_GROUP + u
                keys = key_scr[rows(kt), :]
                for n, pred in enumerate(preds):
                    hit = jnp.where(pred(keys, kt * tk), 1, 0)
                    out[n] = out[n] + jnp.sum(hit.reshape(tk // SUBLANES, SUBLANES, tq), axis=0)
            return tuple(out)
        parts = lax.fori_loop(0, ngrp, body, tuple(jnp.zeros((SUBLANES, tq), I32) for _ in preds))
        return [jnp.sum(part, axis=0, keepdims=True) for part in parts]

    def count16(ref, pred, with_tile=False):
        def body(g, acc):
            tiles = []
            for u in range(TILE_GROUP):
                kt = g * TILE_GROUP + u
                vals = ref[rows(kt), :]
                hit = jnp.where(pred(vals, kt) if with_tile else pred(vals), jnp.int16(1), jnp.int16(0))
                tiles.append(tree_sum([hit[r:r + PACK16] for r in range(0, tk, PACK16)]))
            return acc + tree_sum(tiles)
        part = lax.fori_loop(0, ngrp, body, jnp.zeros((PACK16, tq), I16))
        return jnp.sum(part.astype(I32), axis=0, keepdims=True)

    def search16(ref, target):
        def bit_step(i, state):
            c_u, c_cnt = state
            trial = c_u | (jnp.int32(1) << (15 - i))
            trial_s = (trial - HALF16).astype(I16)
            cnt = count16(ref, lambda v: v >= trial_s)
            ok = cnt >= target
            return jnp.where(ok, trial, c_u), jnp.where(ok, cnt, c_cnt)
        zero = jnp.zeros((1, tq), I32)
        return lax.fori_loop(0, 16, bit_step, (zero, zero))

    hi_u, n_hi_ge = search16(hi_scr, n_sel)
    hi_thr = hi_u - HALF16
    hi_thr16 = hi_thr.astype(I16)

    def keep_bucket(hi, kt):
        lo_scr[rows(kt), :] = jnp.where(hi == hi_thr16, lo_scr[rows(kt), :], jnp.int16(-HALF16))
        return hi > hi_thr16

    cnt_above = count16(hi_scr, keep_bucket, with_tile=True)
    lo_thr, n_lo_ge = search16(lo_scr, n_sel - cnt_above)
    thr = jnp.maximum(hi_thr * (2 * HALF16) + lo_thr, INT_MIN + 1)
    in_bucket = jnp.where(lo_thr != 0, n_lo_ge, n_hi_ge - cnt_above)
    cnt_ge = jnp.where(hi_u != 0, cnt_above + in_bucket, 0)

    neg_bits = np.float32(NEG_BIG).view(np.int32)

    def to_bias(g, carry):
        for u in range(TILE_GROUP):
            kt = g * TILE_GROUP + u
            keys = key_scr[rows(kt), :]
            key_scr[rows(kt), :] = jnp.where(keys > thr, 0, jnp.where(keys == thr, INT_MIN, neg_bits))
        return carry

    lax.fori_loop(0, ngrp, to_bias, 0)

    @pl.when(jnp.max(cnt_ge) > n_sel)
    def _():
        cnt_gt, = count(lambda bits, k0: bits == 0)
        need = n_sel - cnt_gt
        sub = lax.broadcasted_iota(I32, (tk, tq), 0)

        def idx_step(i, j_cut):
            trial = j_cut | (jnp.int32(1) << (idx_bits - 1 - i))
            cnt, = count(lambda bits, k0: (bits == INT_MIN) & (sub + k0 < trial))
            return jnp.where(cnt <= need, trial, j_cut)

        j_cut = lax.fori_loop(0, idx_bits, idx_step, jnp.zeros((1, tq), I32))

        def drop(g, carry):
            for u in range(TILE_GROUP):
                kt = g * TILE_GROUP + u
                bits = key_scr[rows(kt), :]
                key_scr[rows(kt), :] = jnp.where((bits == INT_MIN) & (sub + kt * tk >= j_cut), neg_bits, bits)
            return carry

        lax.fori_loop(0, ngrp, drop, 0)

    ta = ATTN_TILES * tk
    nstep = (nkt + ATTN_TILES - 1) // ATTN_TILES
    m_scr[...] = jnp.full(m_scr.shape, NEG_BIG, F32)
    sacc_scr[...] = jnp.full(sacc_scr.shape, NEG_BIG, F32)
    acc_scr[...] = jnp.zeros(acc_scr.shape, F32)
    ones_rows = jnp.ones((PACK16, ta), BF16)
    span = lambda i: pl.ds(pl.multiple_of(i * ta, ta), ta)

    def logits(i):
        for h in range(N_HEADS):
            g = h // KV_GROUP
            lg = (jnp.dot(k_ref[0, span(i), g * HEAD_DIM:(g + 1) * HEAD_DIM],
                          qt_ref[0, h * HEAD_DIM:(h + 1) * HEAD_DIM, :], preferred_element_type=F32)
                  + lax.bitcast_convert_type(key_scr[span(i), :], F32))
            lg_scr[h] = lg
            m_scr[h] = jnp.maximum(m_scr[h], jnp.max(lg, axis=0, keepdims=True))

    def probs():
        for h in range(N_HEADS):
            s = m_scr[h]
            s_scr[h] = s
            p_scr[h] = jnp.exp2(lg_scr[h] - s).astype(BF16)

    def accumulate(i):
        for h in range(N_HEADS):
            g = h // KV_GROUP
            vt_t = jnp.concatenate([vt_ref[0, g * HEAD_DIM:(g + 1) * HEAD_DIM, span(i)], ones_rows], axis=0)
            s_new = s_scr[h]
            alpha = jnp.exp2(sacc_scr[h] - s_new)
            acc_scr[h] = alpha * acc_scr[h] + jnp.dot(vt_t, p_scr[h], preferred_element_type=F32)
            sacc_scr[h] = s_new

    logits(0)

    @pl.when(nstep == 1)
    def _():
        probs()
        accumulate(0)

    @pl.when(nstep > 1)
    def _():
        probs()
        logits(1)

        def step(i, carry):
            accumulate(i - 1)
            probs()
            logits(i + 1)
            return carry

        lax.fori_loop(1, nstep - 1, step, 0)
        accumulate(nstep - 2)
        probs()
        accumulate(nstep - 1)

    for h in range(N_HEADS):
        o = acc_scr[h, :HEAD_DIM] / acc_scr[h, HEAD_DIM:HEAD_DIM + 1]
        y_ref[0, :, h * HEAD_DIM:(h + 1) * HEAD_DIM] = o.T.astype(BF16)


def _dsa(qt, qit, wtt, k, vt, ki):
    bsz, seq, _ = k.shape
    n_sel = min(TOPK_MAX, seq // 4)
    tq = Q_TILE
    assert Q_TILE == K_TILE and seq % (2 * K_TILE) == 0
    pad_rows = -(-seq // (TILE_GROUP * K_TILE)) * TILE_GROUP * K_TILE
    kern = functools.partial(_dsa_kernel, n_sel=n_sel, idx_bits=seq.bit_length())
    return pl.pallas_call(
        kern,
        out_shape=jax.ShapeDtypeStruct((bsz, seq, ATTN_WIDTH), BF16),
        grid=(bsz, seq // tq),
        in_specs=[
            pl.BlockSpec((1, ATTN_WIDTH, tq), lambda b, i: (b, 0, i)),
            pl.BlockSpec((1, IDX_WIDTH, tq), lambda b, i: (b, 0, i)),
            pl.BlockSpec((1, IDX_HEADS, tq), lambda b, i: (b, 0, i)),
            pl.BlockSpec((1, seq, KV_WIDTH), lambda b, i: (b, 0, 0)),
            pl.BlockSpec((1, KV_WIDTH, seq), lambda b, i: (b, 0, 0)),
            pl.BlockSpec((1, seq, IDX_DIM), lambda b, i: (b, 0, 0)),
        ],
        out_specs=pl.BlockSpec((1, tq, ATTN_WIDTH), lambda b, i: (b, i, 0)),
        scratch_shapes=[
            pltpu.VMEM((pad_rows, tq), I32),
            pltpu.VMEM((pad_rows, tq), I16),
            pltpu.VMEM((pad_rows, tq), I16),
            pltpu.VMEM((N_HEADS, 1, tq), F32),
            pltpu.VMEM((N_HEADS, HEAD_DIM + PACK16, tq), F32),
            pltpu.VMEM((N_HEADS, 1, tq), F32),
            pltpu.VMEM((N_HEADS, 1, tq), F32),
            pltpu.VMEM((N_HEADS, ATTN_TILES * K_TILE, tq), F32),
            pltpu.VMEM((N_HEADS, ATTN_TILES * K_TILE, tq), BF16),
        ],
        compiler_params=_cparams("arbitrary", "arbitrary"),
        name="dsa",
    )(qt, qit, wtt, k, vt, ki)


def _deepnorm_ln(x, h, gate, g, b):
    y = DEEPNORM_ALPHA * x + (1.0 + gate) * h
    mu = jnp.mean(y, axis=-1, keepdims=True)
    yc = y - mu
    var = jnp.mean(yc * yc, axis=-1, keepdims=True)
    return yc * lax.rsqrt(var + LN_EPS) * g + b


def _merge_kernel(x_ref, ys_ref, ya_ref, sg_ref, gt_ref, wglu_ref, bglu_ref, pssm_ref, pattn_ref,
                  wout_ref, lng_ref, lnb_ref, o_ref):
    y = jax.nn.gelu(ys_ref[0].astype(F32))
    glu = jnp.dot(y.astype(BF16), wglu_ref[...], preferred_element_type=F32) + bglu_ref[...]
    y = (y * jax.nn.sigmoid(glu)).astype(BF16)
    sg = sg_ref[0]
    merged = (sg[:, :D_MODEL].astype(F32) * jnp.dot(y, pssm_ref[...], preferred_element_type=F32)
              + sg[:, D_MODEL:].astype(F32) * jnp.dot(ya_ref[0], pattn_ref[...], preferred_element_type=F32))
    h = jnp.dot(merged.astype(BF16), wout_ref[...], preferred_element_type=F32)
    o_ref[0] = _deepnorm_ln(x_ref[0], h, gt_ref[0], lng_ref[...], lnb_ref[...])


def _merge(x, ys, ya, sg, gt, wglu, bglu, pssm, pattn, wout, lng, lnb, tm):
    bsz, seq, d = x.shape
    tok = lambda w: pl.BlockSpec((1, tm, w), lambda b, i: (b, i, 0))
    vec = pl.BlockSpec((1, 1, d), lambda b, i: (b, 0, 0))
    const = lambda a: pl.BlockSpec(a.shape, lambda b, i: (0, 0))
    return pl.pallas_call(
        _merge_kernel,
        out_shape=jax.ShapeDtypeStruct(x.shape, F32),
        grid=(bsz, seq // tm),
        in_specs=[tok(d), tok(SSM_WIDTH), tok(ATTN_WIDTH), tok(2 * D_MODEL), vec,
                  const(wglu), const(bglu), const(pssm), const(pattn), const(wout), const(lng), const(lnb)],
        out_specs=tok(d),
        compiler_params=_cparams("arbitrary", "arbitrary"),
        name="merge",
    )(x, ys, ya, sg, gt, wglu, bglu, pssm, pattn, wout, lng, lnb)


def _ffn_kernel(x_ref, sc_ref, sh_ref, gt_ref, wgu_ref, wd_ref, lng_ref, lnb_ref, o_ref, hid_scr):
    x = x_ref[0]
    u = (x * (1.0 + sc_ref[0]) + sh_ref[0]).astype(BF16)
    for j in range(D_FF // FF_CHUNK):
        a = jnp.dot(u, wgu_ref[:, j * FF_CHUNK:(j + 1) * FF_CHUNK], preferred_element_type=F32)
        b = jnp.dot(u, wgu_ref[:, D_FF + j * FF_CHUNK:D_FF + (j + 1) * FF_CHUNK], preferred_element_type=F32)
        hid_scr[:, j * FF_CHUNK:(j + 1) * FF_CHUNK] = (a * jax.nn.sigmoid(a) * b).astype(BF16)
    f = jnp.dot(hid_scr[...], wd_ref[...], preferred_element_type=F32)
    o_ref[0] = _deepnorm_ln(x, f, gt_ref[0], lng_ref[...], lnb_ref[...])


def _ffn(x, sc, sh, gt, wgu, wd, lng, lnb, tm):
    bsz, seq, d = x.shape
    tok = pl.BlockSpec((1, tm, d), lambda b, i: (b, i, 0))
    vec = pl.BlockSpec((1, 1, d), lambda b, i: (b, 0, 0))
    const = lambda a: pl.BlockSpec(a.shape, lambda b, i: (0, 0))
    return pl.pallas_call(
        _ffn_kernel,
        out_shape=jax.ShapeDtypeStruct(x.shape, F32),
        grid=(bsz, seq // tm),
        in_specs=[tok, vec, vec, vec, const(wgu), const(wd), const(lng), const(lnb)],
        out_specs=tok,
        scratch_shapes=[pltpu.VMEM((tm, D_FF), BF16)],
        compiler_params=_cparams("arbitrary", "arbitrary"),
        name="ffn",
    )(x, sc, sh, gt, wgu, wd, lng, lnb)


def kernel(x, c, w_cond, b_cond, w_in, ssm_lam_re, ssm_lam_im, ssm_log_dt, ssm_b_re, ssm_b_im,
           ssm_c_re, ssm_c_im, ssm_d, ssm_w_glu, ssm_b_glu, p_ssm, p_attn, w_out,
           ln1_g, ln1_b, w_gate_up, w_down, ln2_g, ln2_b):
    bsz, seq, d = x.shape
    assert d == D_MODEL and seq % Q_TILE == 0 and seq % S5_CHUNK == 0
    tm = min(TOKEN_TILE, seq)
    assert seq % tm == 0
    tabs = _rope_tables(seq)
    mod = _cond(c, w_cond, b_cond)
    row = lambda a: a.reshape(1, -1)
    for l in range(DEPTH):
        sh1, sc1, gt1, sh2, sc2, gt2 = [mod[l, :, i * d:(i + 1) * d][:, None, :] for i in range(6)]
        wn, wt = _split_w_in(w_in[l])
        u_ssm, k, ki, sg, qt, vt, qit, wtt = _inproj(x, sc1, sh1, wn, wt, tabs, tm)
        ys = _s5(u_ssm, ssm_lam_re[l], ssm_lam_im[l], ssm_log_dt[l], ssm_b_re[l], ssm_b_im[l],
                 ssm_c_re[l], ssm_c_im[l], ssm_d[l])
        ya = _dsa(qt, qit, wtt, k, vt, ki)
        x = _merge(x, ys, ya, sg, gt1, ssm_w_glu[l].astype(BF16), row(ssm_b_glu[l]),
                   p_ssm[l].astype(BF16), p_attn[l].astype(BF16), w_out[l].astype(BF16),
                   row(ln1_g[l]), row(ln1_b[l]), tm)
        x = _ffn(x, sc2, sh2, gt2, w_gate_up[l].astype(BF16), w_down[l].astype(BF16),
                 row(ln2_g[l]), row(ln2_b[l]), tm)
    return x
```

```python
import functools
import math

import jax
import jax.numpy as jnp
import numpy as np
from jax import lax
from jax.experimental import pallas as pl
from jax.experimental.pallas import tpu as pltpu

F32 = jnp.float32
BF16 = jnp.bfloat16
I32 = jnp.int32
I16 = jnp.int16
HIGHEST = lax.Precision.HIGHEST

D_MODEL = 1024
DEPTH = 2
SSM_WIDTH = 512
SSM_GROUP = 16
SSM_GROUPS = SSM_WIDTH // SSM_GROUP
SSM_STATE = 64
HEAD_DIM = 128
N_HEADS = D_MODEL // HEAD_DIM
N_KV_HEADS = 2
KV_GROUP = N_HEADS // N_KV_HEADS
ATTN_WIDTH = N_HEADS * HEAD_DIM
KV_WIDTH = N_KV_HEADS * HEAD_DIM
IDX_HEADS = 8
IDX_DIM = 64
IDX_WIDTH = IDX_HEADS * IDX_DIM
IDX_SCALE = (IDX_HEADS * IDX_DIM) ** -0.5
TOPK_MAX = 256
ROPE_THETA = 10000.0
D_FF = -(-8 * D_MODEL // (3 * 256)) * 256
DEEPNORM_ALPHA = (2 * DEPTH) ** 0.25
LN_EPS = 1e-5
IN_SIZES = (SSM_WIDTH, ATTN_WIDTH, KV_WIDTH, KV_WIDTH, IDX_WIDTH, IDX_DIM, IDX_HEADS, D_MODEL, D_MODEL)
IN_OFFSETS = [0] + [int(o) for o in np.cumsum(IN_SIZES)]

LANES = 128
SUBLANES = 8
VMEM_LIMIT_BYTES = 56 * 1024 * 1024

COND_TILE = 1536
TOKEN_TILE = 512
S5_CHUNK = LANES
Q_TILE = 256
K_TILE = 256
TILE_GROUP = 4
ATTN_TILES = 2
FF_CHUNK = 256
INT_MIN = -(2 ** 31)
NEG_BIG = -1e30
HALF16 = 2 ** 15
PACK16 = 2 * SUBLANES
Q_SCALE = HEAD_DIM ** -0.5 * math.log2(math.e)


def _cparams(*sem):
    return pltpu.CompilerParams(dimension_semantics=sem, vmem_limit_bytes=VMEM_LIMIT_BYTES)


def _cond_kernel(c_ref, w_ref, b_ref, o_ref):
    c = c_ref[...]
    s = c * jax.nn.sigmoid(c)
    o_ref[0] = jnp.dot(s, w_ref[0], preferred_element_type=F32, precision=HIGHEST) + b_ref[0]


def _cond(c, w_cond, b_cond):
    depth, d, n = w_cond.shape
    bsz = c.shape[0]
    tn = COND_TILE
    assert n % tn == 0
    return pl.pallas_call(
        _cond_kernel,
        out_shape=jax.ShapeDtypeStruct((depth, bsz, n), F32),
        grid=(depth, n // tn),
        in_specs=[
            pl.BlockSpec((bsz, d), lambda l, j: (0, 0)),
            pl.BlockSpec((1, d, tn), lambda l, j: (l, 0, j)),
            pl.BlockSpec((1, 1, tn), lambda l, j: (l, 0, j)),
        ],
        out_specs=pl.BlockSpec((1, bsz, tn), lambda l, j: (l, 0, j)),
        compiler_params=_cparams("arbitrary", "arbitrary"),
        name="cond",
    )(c, w_cond, b_cond.reshape(depth, 1, n))


NAT_SSM = 0
NAT_K = NAT_SSM + SSM_WIDTH
NAT_KI = NAT_K + KV_WIDTH
NAT_G = NAT_KI + LANES
NAT_WIDTH = NAT_G + 2 * D_MODEL
TR_Q = 0
TR_V = TR_Q + ATTN_WIDTH
TR_QI = TR_V + KV_WIDTH
TR_W = TR_QI + IDX_WIDTH
TR_ROWS = -(-(TR_W + IDX_HEADS) // PACK16) * PACK16


def _inproj_kernel(x_ref, sc_ref, sh_ref, wn_ref, wt_ref, cn_ref, sn_ref, cki_ref, ski_ref,
                   ct_ref, st_ref, ct32_ref, st32_ref,
                   ssm_ref, k_ref, ki_ref, sg_ref, qt_ref, vt_ref, qit_ref, wtt_ref):
    u = (x_ref[0] * (1.0 + sc_ref[0]) + sh_ref[0]).astype(BF16)
    zn = jnp.dot(u, wn_ref[...], preferred_element_type=F32)
    zt = lax.dot_general(wt_ref[...], u, (((1,), (1,)), ((), ())),
                         preferred_element_type=F32)

    ssm_ref[0] = zn[:, NAT_SSM:NAT_SSM + SSM_WIDTH].astype(BF16)
    sg_ref[0] = jax.nn.sigmoid(zn[:, NAT_G:NAT_G + 2 * D_MODEL]).astype(BF16)

    cn, sn = cn_ref[...], sn_ref[...]
    for h in range(N_KV_HEADS):
        xh = zn[:, NAT_K + h * HEAD_DIM:NAT_K + (h + 1) * HEAD_DIM]
        rot = pltpu.roll(xh, HEAD_DIM // 2, axis=1)
        k_ref[0, :, h * HEAD_DIM:(h + 1) * HEAD_DIM] = (xh * cn + rot * sn).astype(BF16)

    xk = zn[:, NAT_KI:NAT_KI + LANES]
    lane = lax.broadcasted_iota(I32, xk.shape, 1)
    rot = jnp.where((lane & (IDX_DIM // 2)) == 0,
                    pltpu.roll(xk, LANES - IDX_DIM // 2, axis=1),
                    pltpu.roll(xk, IDX_DIM // 2, axis=1))
    ki_ref[0] = (xk * cki_ref[...] + rot * ski_ref[...])[:, :IDX_DIM].astype(BF16)

    ct, st = ct_ref[...], st_ref[...]
    half = HEAD_DIM // 2
    for h in range(N_HEADS):
        r0 = TR_Q + h * HEAD_DIM
        x1, x2 = zt[r0:r0 + half], zt[r0 + half:r0 + HEAD_DIM]
        qt_ref[0, h * HEAD_DIM:h * HEAD_DIM + half, :] = ((x1 * ct - x2 * st) * Q_SCALE).astype(BF16)
        qt_ref[0, h * HEAD_DIM + half:(h + 1) * HEAD_DIM, :] = ((x2 * ct + x1 * st) * Q_SCALE).astype(BF16)
    ct32, st32 = ct32_ref[...], st32_ref[...]
    half = IDX_DIM // 2
    for h in range(IDX_HEADS):
        r0 = TR_QI + h * IDX_DIM
        x1, x2 = zt[r0:r0 + half], zt[r0 + half:r0 + IDX_DIM]
        qit_ref[0, h * IDX_DIM:h * IDX_DIM + half, :] = (x1 * ct32 - x2 * st32).astype(BF16)
        qit_ref[0, h * IDX_DIM + half:(h + 1) * IDX_DIM, :] = (x2 * ct32 + x1 * st32).astype(BF16)

    vt_ref[0] = zt[TR_V:TR_V + KV_WIDTH].astype(BF16)
    wtt_ref[0] = zt[TR_W:TR_W + IDX_HEADS] * IDX_SCALE


def _rope_tables(seq):
    pos = jnp.arange(seq, dtype=F32)

    def cs(half):
        inv = ROPE_THETA ** (-jnp.arange(half, dtype=F32) / half)
        ang = pos[:, None] * inv[None, :]
        return jnp.cos(ang), jnp.sin(ang)

    c64, s64 = cs(HEAD_DIM // 2)
    c32, s32 = cs(IDX_DIM // 2)
    z = jnp.zeros((seq, LANES - IDX_DIM), F32)
    return dict(
        cn=jnp.concatenate([c64, c64], -1), sn=jnp.concatenate([-s64, s64], -1),
        cki=jnp.concatenate([c32, c32, z], -1), ski=jnp.concatenate([-s32, s32, z], -1),
        ct=c64.T, st=s64.T, ct32=c32.T, st32=s32.T)


def _split_w_in(w_in):
    o = IN_OFFSETS
    ssm, q, k, v, qi, ki, wi, gs, ga = [w_in[:, o[i]:o[i + 1]] for i in range(9)]
    zpad = jnp.zeros((D_MODEL, LANES - IDX_DIM), w_in.dtype)
    wn = jnp.concatenate([ssm, k, ki, zpad, gs, ga], axis=1).astype(BF16)
    wt = jnp.concatenate([q, v, qi, wi, jnp.zeros((D_MODEL, TR_ROWS - TR_W - IDX_HEADS), w_in.dtype)],
                         axis=1).T.astype(BF16)
    return wn, wt


def _inproj(x, sc, sh, wn, wt, tabs, tm):
    bsz, seq, d = x.shape
    grid = (bsz, seq // tm)
    tok = lambda w: pl.BlockSpec((1, tm, w), lambda b, i: (b, i, 0))
    trn = lambda r: pl.BlockSpec((1, r, tm), lambda b, i: (b, 0, i))
    vec = pl.BlockSpec((1, 1, d), lambda b, i: (b, 0, 0))
    const = lambda a: pl.BlockSpec(a.shape, lambda b, i: (0, 0))
    ntab = lambda: pl.BlockSpec((tm, LANES), lambda b, i: (i, 0))
    ttab = lambda r: pl.BlockSpec((r, tm), lambda b, i: (0, i))
    out_shape = (
        jax.ShapeDtypeStruct((bsz, seq, SSM_WIDTH), BF16),
        jax.ShapeDtypeStruct((bsz, seq, KV_WIDTH), BF16),
        jax.ShapeDtypeStruct((bsz, seq, IDX_DIM), BF16),
        jax.ShapeDtypeStruct((bsz, seq, 2 * D_MODEL), BF16),
        jax.ShapeDtypeStruct((bsz, ATTN_WIDTH, seq), BF16),
        jax.ShapeDtypeStruct((bsz, KV_WIDTH, seq), BF16),
        jax.ShapeDtypeStruct((bsz, IDX_WIDTH, seq), BF16),
        jax.ShapeDtypeStruct((bsz, IDX_HEADS, seq), F32),
    )
    return pl.pallas_call(
        _inproj_kernel,
        out_shape=out_shape,
        grid=grid,
        in_specs=[tok(d), vec, vec, const(wn), const(wt), ntab(), ntab(), ntab(), ntab(),
                  ttab(HEAD_DIM // 2), ttab(HEAD_DIM // 2), ttab(IDX_DIM // 2), ttab(IDX_DIM // 2)],
        out_specs=(tok(SSM_WIDTH), tok(KV_WIDTH), tok(IDX_DIM), tok(2 * D_MODEL),
                   trn(ATTN_WIDTH), trn(KV_WIDTH), trn(IDX_WIDTH), trn(IDX_HEADS)),
        compiler_params=_cparams("arbitrary", "arbitrary"),
        name="in_proj",
    )(x, sc, sh, wn, wt, tabs["cn"], tabs["sn"], tabs["cki"], tabs["ski"],
      tabs["ct"], tabs["st"], tabs["ct32"], tabs["st32"])


def _s5_kernel(u_ref, lrr_ref, lir_ref, ldr_ref, lrc_ref, lic_ref, ldc_ref,
               brt_ref, bit_ref, cr_ref, ci_ref, crt_ref, cit_ref, dv_ref,
               y_ref, w_scr, kv_scr, *, bsz):
    t = S5_CHUNK
    p = SSM_STATE
    gsz = SSM_GROUP
    lrr, lir = lrr_ref[0], lir_ref[0]
    dtr = jnp.exp(ldr_ref[0])
    lrc, lic = lrc_ref[0], lic_ref[0]
    dtc = jnp.exp(ldc_ref[0])

    mag = jnp.exp(lrr * dtr)
    ar, ai = mag * jnp.cos(lir * dtr), mag * jnp.sin(lir * dtr)
    den = lrr * lrr + lir * lir
    nr = ar - 1.0
    fr, fi = (nr * lrr + ai * lir) / den, (ai * lrr - nr * lir) / den
    brt, bit = brt_ref[0], bit_ref[0]
    bbr, bbi = fr * brt - fi * bit, fr * bit + fi * brt

    cr, ci = cr_ref[0], ci_ref[0]
    bbr_h, bbi_h = bbr[:, :p], bbi[:, :p]
    cbr = jnp.concatenate([cr[i:i + 1] * bbr_h - ci[i:i + 1] * bbi_h for i in range(gsz)], axis=0)
    cbi = jnp.concatenate([cr[i:i + 1] * bbi_h + ci[i:i + 1] * bbr_h for i in range(gsz)], axis=0)
    lag = lax.broadcasted_iota(I32, (p, t), 1).astype(F32)
    lrc_h, lic_h, dtc_h = lrc[:p], lic[:p], dtc[:p]
    pmag = jnp.exp(lrc_h * dtc_h * lag)
    pth = lic_h * dtc_h * lag
    kv_scr[...] = (jnp.dot(cbr, pmag * jnp.cos(pth), preferred_element_type=F32, precision=HIGHEST)
                   - jnp.dot(cbi, pmag * jnp.sin(pth), preferred_element_type=F32, precision=HIGHEST))

    row = lax.broadcasted_iota(I32, (t, t), 0)
    col = lax.broadcasted_iota(I32, (t, t), 1)
    causal = col >= row

    def toeplitz_cols(i, carry):
        c0 = pl.multiple_of(i * t, t)
        for k in range(gsz):
            vec = kv_scr[pl.ds(i * gsz + k, 1), :]
            blk = pltpu.roll(jnp.broadcast_to(vec, (t, t)), 0, 1, stride=1, stride_axis=0)
            w_scr[k * t:(k + 1) * t, pl.ds(c0, t)] = jnp.where(causal, blk, 0.0).astype(BF16)
        return carry

    lax.fori_loop(0, gsz, toeplitz_cols, 0)

    lo = lax.broadcasted_iota(I32, (t, 2 * p), 1) < p
    rem = (t - 1 - lax.broadcasted_iota(I32, (t, 2 * p), 0)).astype(F32)
    emag = jnp.exp(lrr * dtr * rem)
    eth = lir * dtr * rem
    ecs, esn = emag * jnp.cos(eth), emag * jnp.sin(eth)
    e_same, e_swap = jnp.where(lo, ecs, esn), jnp.where(lo, esn, ecs)
    sgn = jnp.where(lo[:1], -1.0, 1.0)
    wb = jnp.concatenate([e_same * bbr[k:k + 1] + e_swap * (bbi[k:k + 1] * sgn) for k in range(gsz)],
                         axis=0).astype(BF16)

    top = lax.broadcasted_iota(I32, (2 * p, t), 0) < p
    step = (lax.broadcasted_iota(I32, (2 * p, t), 1) + 1).astype(F32)
    cmag = jnp.exp(lrc * dtc * step)
    cth = lic * dtc * step
    mc, ms = cmag * jnp.cos(cth), cmag * jnp.sin(cth)
    crt, cit = crt_ref[0], cit_ref[0]
    wc = jnp.concatenate(
        [jnp.where(top, crt[:, i:i + 1] * mc - cit[:, i:i + 1] * ms,
                   -(crt[:, i:i + 1] * ms + cit[:, i:i + 1] * mc)) for i in range(gsz)],
        axis=1).astype(BF16)

    u = u_ref[0]
    y = jnp.dot(u, w_scr[...], preferred_element_type=F32)
    g = jnp.dot(u, wb, preferred_element_type=F32)
    gr, gi = g[:, :p], g[:, p:]

    m_t = jnp.exp(lrr[:, :p] * dtr[:, :p] * float(t))
    at_r = m_t * jnp.cos(lir[:, :p] * dtr[:, :p] * float(t))
    at_i = m_t * jnp.sin(lir[:, :p] * dtr[:, :p] * float(t))
    nc = u.shape[0] // bsz
    hr = jnp.zeros((bsz, p), F32)
    hi = jnp.zeros((bsz, p), F32)
    hrs, his = [hr], [hi]
    for c in range(1, nc):
        sl = slice((c - 1) * bsz, c * bsz)
        hr, hi = at_r * hr - at_i * hi + gr[sl], at_r * hi + at_i * hr + gi[sl]
        hrs.append(hr)
        his.append(hi)
    hcat = jnp.concatenate([jnp.concatenate(hrs, axis=0), jnp.concatenate(his, axis=0)], axis=1)
    y = y + jnp.dot(hcat.astype(BF16), wc, preferred_element_type=F32)
    y_ref[0] = (y + dv_ref[0] * u.astype(F32)).astype(BF16)


def _s5(u_ssm, lam_re, lam_im, log_dt, b_re, b_im, c_re, c_im, d_skip):
    bsz, seq, _ = u_ssm.shape
    t, g, gs, p = S5_CHUNK, SSM_GROUPS, SSM_GROUP, SSM_STATE
    nc = seq // t
    rows = nc * bsz
    ug = u_ssm.reshape(bsz, nc, t, g, gs).transpose(3, 1, 0, 4, 2).reshape(g, rows, gs * t)

    dup_r = lambda a: jnp.concatenate([a, a], axis=-1)[:, None, :]
    dup_c = lambda a: jnp.concatenate([a, a], axis=-1)[:, :, None]
    ld = jnp.broadcast_to(log_dt[:, None], (g, p))
    brt = jnp.swapaxes(b_re, 1, 2)
    bit = jnp.swapaxes(b_im, 1, 2)
    crt = jnp.swapaxes(c_re, 1, 2)
    cit = jnp.swapaxes(c_im, 1, 2)
    args = (
        ug,
        dup_r(lam_re), dup_r(lam_im), dup_r(ld), dup_c(lam_re), dup_c(lam_im), dup_c(ld),
        jnp.concatenate([brt, brt], -1), jnp.concatenate([bit, bit], -1),
        c_re, c_im, jnp.concatenate([crt, crt], 1), jnp.concatenate([cit, cit], 1),
        jnp.repeat(d_skip.reshape(g, gs), t, axis=1)[:, None, :],
    )
    per_g = lambda a: pl.BlockSpec((1,) + a.shape[1:], lambda i: (i,) + (0,) * (a.ndim - 1))
    yg = pl.pallas_call(
        functools.partial(_s5_kernel, bsz=bsz),
        out_shape=jax.ShapeDtypeStruct((g, rows, gs * t), BF16),
        grid=(g,),
        in_specs=[per_g(a) for a in args],
        out_specs=pl.BlockSpec((1, rows, gs * t), lambda i: (i, 0, 0)),
        scratch_shapes=[pltpu.VMEM((gs * t, gs * t), BF16), pltpu.VMEM((gs * gs, t), F32)],
        compiler_params=_cparams("arbitrary"),
        name="s5",
    )(*args)
    return yg.reshape(g, nc, bsz, gs, t).transpose(2, 1, 4, 0, 3).reshape(bsz, seq, g * gs)


def _dsa_kernel(qt_ref, qit_ref, wt_ref, k_ref, vt_ref, ki_ref, y_ref,
                key_scr, hi_scr, lo_scr, m_scr, acc_scr, sacc_scr, s_scr, lg_scr, p_scr, *, n_sel, idx_bits):
    tq, tk = Q_TILE, K_TILE
    qi = pl.program_id(1)
    nkt = qi + 1
    q0 = qi * tq
    diff = (lax.broadcasted_iota(I32, (tk, tq), 0) - lax.broadcasted_iota(I32, (tk, tq), 1))

    ngrp = (nkt + TILE_GROUP - 1) // TILE_GROUP
    npair = (nkt + 1) // 2
    rows = lambda kt: pl.ds(pl.multiple_of(kt * tk, tk), tk)
    filler16 = jnp.full((tk, tq), -HALF16, I16)

    def score_tile(kt, causal_mask):
        k0 = kt * tk
        ki_t = ki_ref[0, rows(kt), :]
        s = jnp.zeros((tk, tq), F32)
        for h in range(IDX_HEADS):
            rel = jnp.dot(ki_t, qit_ref[0, h * IDX_DIM:(h + 1) * IDX_DIM, :], preferred_element_type=F32)
            s = s + jnp.maximum(rel, 0.0) * wt_ref[0, h:h + 1, :]
        bits = lax.bitcast_convert_type(s, I32)
        sign = bits >> 31
        key = ((bits & 0x7FFFFFFF) ^ sign) - sign
        if causal_mask:
            key = jnp.where(diff <= q0 - k0, key, INT_MIN)
        key_scr[rows(kt), :] = key
        hi_scr[rows(kt), :] = (key >> 16).astype(I16)
        lo_scr[rows(kt), :] = ((key & 0xFFFF) - HALF16).astype(I16)

    def score_pair(j, carry):
        score_tile(2 * j, False)
        score_tile(2 * j + 1, False)
        return carry

    lax.fori_loop(0, npair - 1, score_pair, 0)
    score_tile(2 * npair - 2, True)
    score_tile(2 * npair - 1, True)

    @pl.when(2 * npair < TILE_GROUP * ngrp)
    def _():
        for kt in (2 * npair, 2 * npair + 1):
            key_scr[rows(kt), :] = jnp.full((tk, tq), INT_MIN, I32)
            hi_scr[rows(kt), :] = filler16
            lo_scr[rows(kt), :] = filler16

    def tree_sum(parts):
        while len(parts) > 1:
            parts = [a + b for a, b in zip(parts[::2], parts[1::2])]
        return parts[0]

    def count16(ref, pred, with_tile=False):
        def body(g, acc):
            tiles = []
            for u in range(TILE_GROUP):
                kt = g * TILE_GROUP + u
                vals = ref[rows(kt), :]
                hit = jnp.where(pred(vals, kt) if with_tile else pred(vals), jnp.int16(1), jnp.int16(0))
                tiles.append(tree_sum([hit[r:r + PACK16] for r in range(0, tk, PACK16)]))
            return acc + tree_sum(tiles)
        part = lax.fori_loop(0, ngrp, body, jnp.zeros((PACK16, tq), I16))
        return jnp.sum(part.astype(I32), axis=0, keepdims=True)

    def search16(ref, target):
        def bit_step(i, state):
            c_u, c_cnt = state
            trial = c_u | (jnp.int32(1) << (15 - i))
            trial_s = (trial - HALF16).astype(I16)
            cnt = count16(ref, lambda v: v >= trial_s)
            ok = cnt >= target
            return jnp.where(ok, trial, c_u), jnp.where(ok, cnt, c_cnt)
        zero = jnp.zeros((1, tq), I32)
        return lax.fori_loop(0, 16, bit_step, (zero, zero))

    hi_u, n_hi_ge = search16(hi_scr, n_sel)
    hi_thr = hi_u - HALF16
    hi_thr16 = hi_thr.astype(I16)

    def keep_bucket(hi, kt):
        lo_scr[rows(kt), :] = jnp.where(hi == hi_thr16, lo_scr[rows(kt), :], jnp.int16(-HALF16))
        return hi > hi_thr16

    cnt_above = count16(hi_scr, keep_bucket, with_tile=True)
    lo_thr, n_lo_ge = search16(lo_scr, n_sel - cnt_above)
    thr = jnp.maximum(hi_thr * (2 * HALF16) + lo_thr, INT_MIN + 1)
    in_bucket = jnp.where(lo_thr != 0, n_lo_ge, n_hi_ge - cnt_above)
    cnt_ge = jnp.where(hi_u != 0, cnt_above + in_bucket, 0)

    neg_bits = np.float32(NEG_BIG).view(np.int32)

    def to_bias(g, carry):
        for u in range(TILE_GROUP):
            kt = g * TILE_GROUP + u
            keys = key_scr[rows(kt), :]
            key_scr[rows(kt), :] = jnp.where(keys > thr, 0, jnp.where(keys == thr, INT_MIN, neg_bits))
        return carry

    lax.fori_loop(0, ngrp, to_bias, 0)

    @pl.when(jnp.max(cnt_ge) > n_sel)
    def _():
        lo_thr16 = (lo_thr - HALF16).astype(I16)
        cnt_gt = cnt_above + count16(lo_scr, lambda v: v > lo_thr16)
        need = n_sel - cnt_gt
        sub = lax.broadcasted_iota(I32, (tk, tq), 0)
        no_tie = jnp.int16(HALF16 - 1)

        def tie_index(g, carry):
            for u in range(TILE_GROUP):
                kt = g * TILE_GROUP + u
                idx = (sub + kt * tk).astype(I16)
                at_lo = jnp.where(lo_scr[rows(kt), :] == lo_thr16, idx, no_tie)
                lo_scr[rows(kt), :] = jnp.where(hi_scr[rows(kt), :] == hi_thr16, at_lo, no_tie)
            return carry

        lax.fori_loop(0, ngrp, tie_index, 0)

        def idx_step(i, j_cut):
            trial = j_cut | (jnp.int32(1) << (idx_bits - 1 - i))
            trial16 = trial.astype(I16)
            cnt = count16(lo_scr, lambda v: v < trial16)
            return jnp.where(cnt <= need, trial, j_cut)

        j_cut = lax.fori_loop(0, idx_bits, idx_step, jnp.zeros((1, tq), I32))

        def drop(g, carry):
            for u in range(TILE_GROUP):
                kt = g * TILE_GROUP + u
                bits = key_scr[rows(kt), :]
                key_scr[rows(kt), :] = jnp.where((bits == INT_MIN) & (sub + kt * tk >= j_cut), neg_bits, bits)
            return carry

        lax.fori_loop(0, ngrp, drop, 0)

    ta = ATTN_TILES * tk
    nstep = (nkt + ATTN_TILES - 1) // ATTN_TILES
    m_scr[...] = jnp.full(m_scr.shape, NEG_BIG, F32)
    sacc_scr[...] = jnp.full(sacc_scr.shape, NEG_BIG, F32)
    acc_scr[...] = jnp.zeros(acc_scr.shape, F32)
    ones_rows = jnp.ones((PACK16, ta), BF16)
    span = lambda i: pl.ds(pl.multiple_of(i * ta, ta), ta)

    def logits(i):
        for h in range(N_HEADS):
            g = h // KV_GROUP
            lg = (jnp.dot(k_ref[0, span(i), g * HEAD_DIM:(g + 1) * HEAD_DIM],
                          qt_ref[0, h * HEAD_DIM:(h + 1) * HEAD_DIM, :], preferred_element_type=F32)
                  + lax.bitcast_convert_type(key_scr[span(i), :], F32))
            lg_scr[h] = lg
            m_scr[h] = jnp.maximum(m_scr[h], jnp.max(lg, axis=0, keepdims=True))

    def probs():
        for h in range(N_HEADS):
            s = m_scr[h]
            s_scr[h] = s
            p_scr[h] = jnp.exp2(lg_scr[h] - s).astype(BF16)

    def accumulate(i):
        for h in range(N_HEADS):
            g = h // KV_GROUP
            vt_t = jnp.concatenate([vt_ref[0, g * HEAD_DIM:(g + 1) * HEAD_DIM, span(i)], ones_rows], axis=0)
            s_new = s_scr[h]
            alpha = jnp.exp2(sacc_scr[h] - s_new)
            acc_scr[h] = alpha * acc_scr[h] + jnp.dot(vt_t, p_scr[h], preferred_element_type=F32)
            sacc_scr[h] = s_new

    logits(0)

    @pl.when(nstep == 1)
    def _():
        probs()
        accumulate(0)

    @pl.when(nstep > 1)
    def _():
        probs()
        logits(1)

        def step(i, carry):
            accumulate(i - 1)
            probs()
            logits(i + 1)
            return carry

        lax.fori_loop(1, nstep - 1, step, 0)
        accumulate(nstep - 2)
        probs()
        accumulate(nstep - 1)

    for h in range(N_HEADS):
        o = acc_scr[h, :HEAD_DIM] / acc_scr[h, HEAD_DIM:HEAD_DIM + 1]
        y_ref[0, :, h * HEAD_DIM:(h + 1) * HEAD_DIM] = o.T.astype(BF16)


def _dsa(qt, qit, wtt, k, vt, ki):
    bsz, seq, _ = k.shape
    n_sel = min(TOPK_MAX, seq // 4)
    tq = Q_TILE
    assert Q_TILE == K_TILE and seq % (2 * K_TILE) == 0 and seq < HALF16
    pad_rows = -(-seq // (TILE_GROUP * K_TILE)) * TILE_GROUP * K_TILE
    kern = functools.partial(_dsa_kernel, n_sel=n_sel, idx_bits=seq.bit_length())
    return pl.pallas_call(
        kern,
        out_shape=jax.ShapeDtypeStruct((bsz, seq, ATTN_WIDTH), BF16),
        grid=(bsz, seq // tq),
        in_specs=[
            pl.BlockSpec((1, ATTN_WIDTH, tq), lambda b, i: (b, 0, i)),
            pl.BlockSpec((1, IDX_WIDTH, tq), lambda b, i: (b, 0, i)),
            pl.BlockSpec((1, IDX_HEADS, tq), lambda b, i: (b, 0, i)),
            pl.BlockSpec((1, seq, KV_WIDTH), lambda b, i: (b, 0, 0)),
            pl.BlockSpec((1, KV_WIDTH, seq), lambda b, i: (b, 0, 0)),
            pl.BlockSpec((1, seq, IDX_DIM), lambda b, i: (b, 0, 0)),
        ],
        out_specs=pl.BlockSpec((1, tq, ATTN_WIDTH), lambda b, i: (b, i, 0)),
        scratch_shapes=[
            pltpu.VMEM((pad_rows, tq), I32),
            pltpu.VMEM((pad_rows, tq), I16),
            pltpu.VMEM((pad_rows, tq), I16),
            pltpu.VMEM((N_HEADS, 1, tq), F32),
            pltpu.VMEM((N_HEADS, HEAD_DIM + PACK16, tq), F32),
            pltpu.VMEM((N_HEADS, 1, tq), F32),
            pltpu.VMEM((N_HEADS, 1, tq), F32),
            pltpu.VMEM((N_HEADS, ATTN_TILES * K_TILE, tq), F32),
            pltpu.VMEM((N_HEADS, ATTN_TILES * K_TILE, tq), BF16),
        ],
        compiler_params=_cparams("arbitrary", "arbitrary"),
        name="dsa",
    )(qt, qit, wtt, k, vt, ki)


def _deepnorm_ln(x, h, gate, g, b):
    y = DEEPNORM_ALPHA * x + (1.0 + gate) * h
    mu = jnp.mean(y, axis=-1, keepdims=True)
    yc = y - mu
    var = jnp.mean(yc * yc, axis=-1, keepdims=True)
    return yc * lax.rsqrt(var + LN_EPS) * g + b


def _merge_kernel(x_ref, ys_ref, ya_ref, sg_ref, gt_ref, wglu_ref, bglu_ref, pssm_ref, pattn_ref,
                  wout_ref, lng_ref, lnb_ref, o_ref):
    y = jax.nn.gelu(ys_ref[0].astype(F32))
    glu = jnp.dot(y.astype(BF16), wglu_ref[...], preferred_element_type=F32) + bglu_ref[...]
    y = (y * jax.nn.sigmoid(glu)).astype(BF16)
    sg = sg_ref[0]
    merged = (sg[:, :D_MODEL].astype(F32) * jnp.dot(y, pssm_ref[...], preferred_element_type=F32)
              + sg[:, D_MODEL:].astype(F32) * jnp.dot(ya_ref[0], pattn_ref[...], preferred_element_type=F32))
    h = jnp.dot(merged.astype(BF16), wout_ref[...], preferred_element_type=F32)
    o_ref[0] = _deepnorm_ln(x_ref[0], h, gt_ref[0], lng_ref[...], lnb_ref[...])


def _merge(x, ys, ya, sg, gt, wglu, bglu, pssm, pattn, wout, lng, lnb, tm):
    bsz, seq, d = x.shape
    tok = lambda w: pl.BlockSpec((1, tm, w), lambda b, i: (b, i, 0))
    vec = pl.BlockSpec((1, 1, d), lambda b, i: (b, 0, 0))
    const = lambda a: pl.BlockSpec(a.shape, lambda b, i: (0, 0))
    return pl.pallas_call(
        _merge_kernel,
        out_shape=jax.ShapeDtypeStruct(x.shape, F32),
        grid=(bsz, seq // tm),
        in_specs=[tok(d), tok(SSM_WIDTH), tok(ATTN_WIDTH), tok(2 * D_MODEL), vec,
                  const(wglu), const(bglu), const(pssm), const(pattn), const(wout), const(lng), const(lnb)],
        out_specs=tok(d),
        compiler_params=_cparams("arbitrary", "arbitrary"),
        name="merge",
    )(x, ys, ya, sg, gt, wglu, bglu, pssm, pattn, wout, lng, lnb)


def _ffn_kernel(x_ref, sc_ref, sh_ref, gt_ref, wgu_ref, wd_ref, lng_ref, lnb_ref, o_ref, hid_scr):
    x = x_ref[0]
    u = (x * (1.0 + sc_ref[0]) + sh_ref[0]).astype(BF16)
    for j in range(D_FF // FF_CHUNK):
        a = jnp.dot(u, wgu_ref[:, j * FF_CHUNK:(j + 1) * FF_CHUNK], preferred_element_type=F32)
        b = jnp.dot(u, wgu_ref[:, D_FF + j * FF_CHUNK:D_FF + (j + 1) * FF_CHUNK], preferred_element_type=F32)
        hid_scr[:, j * FF_CHUNK:(j + 1) * FF_CHUNK] = (a * jax.nn.sigmoid(a) * b).astype(BF16)
    f = jnp.dot(hid_scr[...], wd_ref[...], preferred_element_type=F32)
    o_ref[0] = _deepnorm_ln(x, f, gt_ref[0], lng_ref[...], lnb_ref[...])


def _ffn(x, sc, sh, gt, wgu, wd, lng, lnb, tm):
    bsz, seq, d = x.shape
    tok = pl.BlockSpec((1, tm, d), lambda b, i: (b, i, 0))
    vec = pl.BlockSpec((1, 1, d), lambda b, i: (b, 0, 0))
    const = lambda a: pl.BlockSpec(a.shape, lambda b, i: (0, 0))
    return pl.pallas_call(
        _ffn_kernel,
        out_shape=jax.ShapeDtypeStruct(x.shape, F32),
        grid=(bsz, seq // tm),
        in_specs=[tok, vec, vec, vec, const(wgu), const(wd), const(lng), const(lnb)],
        out_specs=tok,
        scratch_shapes=[pltpu.VMEM((tm, D_FF), BF16)],
        compiler_params=_cparams("arbitrary", "arbitrary"),
        name="ffn",
    )(x, sc, sh, gt, wgu, wd, lng, lnb)


def kernel(x, c, w_cond, b_cond, w_in, ssm_lam_re, ssm_lam_im, ssm_log_dt, ssm_b_re, ssm_b_im,
           ssm_c_re, ssm_c_im, ssm_d, ssm_w_glu, ssm_b_glu, p_ssm, p_attn, w_out,
           ln1_g, ln1_b, w_gate_up, w_down, ln2_g, ln2_b):
    bsz, seq, d = x.shape
    assert d == D_MODEL and seq % Q_TILE == 0 and seq % S5_CHUNK == 0
    tm = min(TOKEN_TILE, seq)
    assert seq % tm == 0
    tabs = _rope_tables(seq)
    mod = _cond(c, w_cond, b_cond)
    row = lambda a: a.reshape(1, -1)
    for l in range(DEPTH):
        sh1, sc1, gt1, sh2, sc2, gt2 = [mod[l, :, i * d:(i + 1) * d][:, None, :] for i in range(6)]
        wn, wt = _split_w_in(w_in[l])
        u_ssm, k, ki, sg, qt, vt, qit, wtt = _inproj(x, sc1, sh1, wn, wt, tabs, tm)
        ys = _s5(u_ssm, ssm_lam_re[l], ssm_lam_im[l], ssm_log_dt[l], ssm_b_re[l], ssm_b_im[l],
                 ssm_c_re[l], ssm_c_im[l], ssm_d[l])
        ya = _dsa(qt, qit, wtt, k, vt, ki)
        x = _merge(x, ys, ya, sg, gt1, ssm_w_glu[l].astype(BF16), row(ssm_b_glu[l]),
                   p_ssm[l].astype(BF16), p_attn[l].astype(BF16), w_out[l].astype(BF16),
                   row(ln1_g[l]), row(ln1_b[l]), tm)
        x = _ffn(x, sc2, sh2, gt2, w_gate_up[l].astype(BF16), w_down[l].astype(BF16),
                 row(ln2_g[l]), row(ln2_b[l]), tm)
    return x
```

```python
import functools
import math

import jax
import jax.numpy as jnp
import numpy as np
from jax import lax
from jax.experimental import pallas as pl
from jax.experimental.pallas import tpu as pltpu

F32 = jnp.float32
BF16 = jnp.bfloat16
I32 = jnp.int32
I16 = jnp.int16
HIGHEST = lax.Precision.HIGHEST

D_MODEL = 1024
DEPTH = 2
SSM_WIDTH = 512
SSM_GROUP = 16
SSM_GROUPS = SSM_WIDTH // SSM_GROUP
SSM_STATE = 64
HEAD_DIM = 128
N_HEADS = D_MODEL // HEAD_DIM
N_KV_HEADS = 2
KV_GROUP = N_HEADS // N_KV_HEADS
ATTN_WIDTH = N_HEADS * HEAD_DIM
KV_WIDTH = N_KV_HEADS * HEAD_DIM
IDX_HEADS = 8
IDX_DIM = 64
IDX_WIDTH = IDX_HEADS * IDX_DIM
IDX_SCALE = (IDX_HEADS * IDX_DIM) ** -0.5
TOPK_MAX = 256
ROPE_THETA = 10000.0
D_FF = -(-8 * D_MODEL // (3 * 256)) * 256
DEEPNORM_ALPHA = (2 * DEPTH) ** 0.25
LN_EPS = 1e-5
IN_SIZES = (SSM_WIDTH, ATTN_WIDTH, KV_WIDTH, KV_WIDTH, IDX_WIDTH, IDX_DIM, IDX_HEADS, D_MODEL, D_MODEL)
IN_OFFSETS = [0] + [int(o) for o in np.cumsum(IN_SIZES)]

LANES = 128
SUBLANES = 8
VMEM_LIMIT_BYTES = 56 * 1024 * 1024

COND_TILE = 1536
TOKEN_TILE = 512
S5_CHUNK = LANES
Q_TILE = 256
K_TILE = 256
TILE_GROUP = 4
ATTN_TILES = 2
FF_CHUNK = 256
INT_MIN = -(2 ** 31)
NEG_BIG = -1e30
HALF16 = 2 ** 15
PACK16 = 2 * SUBLANES
Q_SCALE = HEAD_DIM ** -0.5 * math.log2(math.e)


def _cparams(*sem):
    return pltpu.CompilerParams(dimension_semantics=sem, vmem_limit_bytes=VMEM_LIMIT_BYTES)


def _cond_kernel(c_ref, w_ref, b_ref, o_ref):
    c = c_ref[...]
    s = c * jax.nn.sigmoid(c)
    o_ref[0] = jnp.dot(s, w_ref[0], preferred_element_type=F32, precision=HIGHEST) + b_ref[0]


def _cond(c, w_cond, b_cond):
    depth, d, n = w_cond.shape
    bsz = c.shape[0]
    tn = COND_TILE
    assert n % tn == 0
    return pl.pallas_call(
        _cond_kernel,
        out_shape=jax.ShapeDtypeStruct((depth, bsz, n), F32),
        grid=(depth, n // tn),
        in_specs=[
            pl.BlockSpec((bsz, d), lambda l, j: (0, 0)),
            pl.BlockSpec((1, d, tn), lambda l, j: (l, 0, j)),
            pl.BlockSpec((1, 1, tn), lambda l, j: (l, 0, j)),
        ],
        out_specs=pl.BlockSpec((1, bsz, tn), lambda l, j: (l, 0, j)),
        compiler_params=_cparams("arbitrary", "arbitrary"),
        name="cond",
    )(c, w_cond, b_cond.reshape(depth, 1, n))


NAT_SSM = 0
NAT_K = NAT_SSM + SSM_WIDTH
NAT_KI = NAT_K + KV_WIDTH
NAT_G = NAT_KI + LANES
NAT_WIDTH = NAT_G + 2 * D_MODEL
TR_Q = 0
TR_V = TR_Q + ATTN_WIDTH
TR_QI = TR_V + KV_WIDTH
TR_W = TR_QI + IDX_WIDTH
TR_ROWS = -(-(TR_W + IDX_HEADS) // PACK16) * PACK16


def _inproj_kernel(x_ref, sc_ref, sh_ref, wn_ref, wt_ref, cn_ref, sn_ref, cki_ref, ski_ref,
                   ct_ref, st_ref, ct32_ref, st32_ref,
                   ssm_ref, k_ref, ki_ref, sg_ref, qt_ref, vt_ref, qit_ref, wtt_ref):
    u = (x_ref[0] * (1.0 + sc_ref[0]) + sh_ref[0]).astype(BF16)
    zn = jnp.dot(u, wn_ref[...], preferred_element_type=F32)
    zt = lax.dot_general(wt_ref[...], u, (((1,), (1,)), ((), ())),
                         preferred_element_type=F32)

    ssm_ref[0] = zn[:, NAT_SSM:NAT_SSM + SSM_WIDTH].astype(BF16)
    sg_ref[0] = jax.nn.sigmoid(zn[:, NAT_G:NAT_G + 2 * D_MODEL]).astype(BF16)

    cn, sn = cn_ref[...], sn_ref[...]
    for h in range(N_KV_HEADS):
        xh = zn[:, NAT_K + h * HEAD_DIM:NAT_K + (h + 1) * HEAD_DIM]
        rot = pltpu.roll(xh, HEAD_DIM // 2, axis=1)
        k_ref[0, :, h * HEAD_DIM:(h + 1) * HEAD_DIM] = (xh * cn + rot * sn).astype(BF16)

    xk = zn[:, NAT_KI:NAT_KI + LANES]
    lane = lax.broadcasted_iota(I32, xk.shape, 1)
    rot = jnp.where((lane & (IDX_DIM // 2)) == 0,
                    pltpu.roll(xk, LANES - IDX_DIM // 2, axis=1),
                    pltpu.roll(xk, IDX_DIM // 2, axis=1))
    ki_ref[0] = (xk * cki_ref[...] + rot * ski_ref[...])[:, :IDX_DIM].astype(BF16)

    ct, st = ct_ref[...], st_ref[...]
    half = HEAD_DIM // 2
    for h in range(N_HEADS):
        r0 = TR_Q + h * HEAD_DIM
        x1, x2 = zt[r0:r0 + half], zt[r0 + half:r0 + HEAD_DIM]
        qt_ref[0, h * HEAD_DIM:h * HEAD_DIM + half, :] = ((x1 * ct - x2 * st) * Q_SCALE).astype(BF16)
        qt_ref[0, h * HEAD_DIM + half:(h + 1) * HEAD_DIM, :] = ((x2 * ct + x1 * st) * Q_SCALE).astype(BF16)
    ct32, st32 = ct32_ref[...], st32_ref[...]
    half = IDX_DIM // 2
    for h in range(IDX_HEADS):
        r0 = TR_QI + h * IDX_DIM
        x1, x2 = zt[r0:r0 + half], zt[r0 + half:r0 + IDX_DIM]
        qit_ref[0, h * IDX_DIM:h * IDX_DIM + half, :] = (x1 * ct32 - x2 * st32).astype(BF16)
        qit_ref[0, h * IDX_DIM + half:(h + 1) * IDX_DIM, :] = (x2 * ct32 + x1 * st32).astype(BF16)

    vt_ref[0] = zt[TR_V:TR_V + KV_WIDTH].astype(BF16)
    wtt_ref[0] = zt[TR_W:TR_W + IDX_HEADS] * IDX_SCALE


def _rope_tables(seq):
    pos = jnp.arange(seq, dtype=F32)

    def cs(half):
        inv = ROPE_THETA ** (-jnp.arange(half, dtype=F32) / half)
        ang = pos[:, None] * inv[None, :]
        return jnp.cos(ang), jnp.sin(ang)

    c64, s64 = cs(HEAD_DIM // 2)
    c32, s32 = cs(IDX_DIM // 2)
    z = jnp.zeros((seq, LANES - IDX_DIM), F32)
    return dict(
        cn=jnp.concatenate([c64, c64], -1), sn=jnp.concatenate([-s64, s64], -1),
        cki=jnp.concatenate([c32, c32, z], -1), ski=jnp.concatenate([-s32, s32, z], -1),
        ct=c64.T, st=s64.T, ct32=c32.T, st32=s32.T)


def _split_w_in(w_in):
    o = IN_OFFSETS
    ssm, q, k, v, qi, ki, wi, gs, ga = [w_in[:, o[i]:o[i + 1]] for i in range(9)]
    zpad = jnp.zeros((D_MODEL, LANES - IDX_DIM), w_in.dtype)
    wn = jnp.concatenate([ssm, k, ki, zpad, gs, ga], axis=1).astype(BF16)
    wt = jnp.concatenate([q, v, qi, wi, jnp.zeros((D_MODEL, TR_ROWS - TR_W - IDX_HEADS), w_in.dtype)],
                         axis=1).T.astype(BF16)
    return wn, wt


def _inproj(x, sc, sh, wn, wt, tabs, tm):
    bsz, seq, d = x.shape
    grid = (bsz, seq // tm)
    tok = lambda w: pl.BlockSpec((1, tm, w), lambda b, i: (b, i, 0))
    trn = lambda r: pl.BlockSpec((1, r, tm), lambda b, i: (b, 0, i))
    vec = pl.BlockSpec((1, 1, d), lambda b, i: (b, 0, 0))
    const = lambda a: pl.BlockSpec(a.shape, lambda b, i: (0, 0))
    ntab = lambda: pl.BlockSpec((tm, LANES), lambda b, i: (i, 0))
    ttab = lambda r: pl.BlockSpec((r, tm), lambda b, i: (0, i))
    out_shape = (
        jax.ShapeDtypeStruct((bsz, seq, SSM_WIDTH), BF16),
        jax.ShapeDtypeStruct((bsz, seq, KV_WIDTH), BF16),
        jax.ShapeDtypeStruct((bsz, seq, IDX_DIM), BF16),
        jax.ShapeDtypeStruct((bsz, seq, 2 * D_MODEL), BF16),
        jax.ShapeDtypeStruct((bsz, ATTN_WIDTH, seq), BF16),
        jax.ShapeDtypeStruct((bsz, KV_WIDTH, seq), BF16),
        jax.ShapeDtypeStruct((bsz, IDX_WIDTH, seq), BF16),
        jax.ShapeDtypeStruct((bsz, IDX_HEADS, seq), F32),
    )
    return pl.pallas_call(
        _inproj_kernel,
        out_shape=out_shape,
        grid=grid,
        in_specs=[tok(d), vec, vec, const(wn), const(wt), ntab(), ntab(), ntab(), ntab(),
                  ttab(HEAD_DIM // 2), ttab(HEAD_DIM // 2), ttab(IDX_DIM // 2), ttab(IDX_DIM // 2)],
        out_specs=(tok(SSM_WIDTH), tok(KV_WIDTH), tok(IDX_DIM), tok(2 * D_MODEL),
                   trn(ATTN_WIDTH), trn(KV_WIDTH), trn(IDX_WIDTH), trn(IDX_HEADS)),
        compiler_params=_cparams("arbitrary", "arbitrary"),
        name="in_proj",
    )(x, sc, sh, wn, wt, tabs["cn"], tabs["sn"], tabs["cki"], tabs["ski"],
      tabs["ct"], tabs["st"], tabs["ct32"], tabs["st32"])


def _s5_kernel(u_ref, lrr_ref, lir_ref, ldr_ref, lrc_ref, lic_ref, ldc_ref,
               brt_ref, bit_ref, cr_ref, ci_ref, crt_ref, cit_ref, dv_ref,
               y_ref, w_scr, kv_scr, *, bsz):
    t = S5_CHUNK
    p = SSM_STATE
    gsz = SSM_GROUP
    lrr, lir = lrr_ref[0], lir_ref[0]
    dtr = jnp.exp(ldr_ref[0])
    lrc, lic = lrc_ref[0], lic_ref[0]
    dtc = jnp.exp(ldc_ref[0])

    mag = jnp.exp(lrr * dtr)
    ar, ai = mag * jnp.cos(lir * dtr), mag * jnp.sin(lir * dtr)
    den = lrr * lrr + lir * lir
    nr = ar - 1.0
    fr, fi = (nr * lrr + ai * lir) / den, (ai * lrr - nr * lir) / den
    brt, bit = brt_ref[0], bit_ref[0]
    bbr, bbi = fr * brt - fi * bit, fr * bit + fi * brt

    cr, ci = cr_ref[0], ci_ref[0]
    bbr_h, bbi_h = bbr[:, :p], bbi[:, :p]
    cbr = jnp.concatenate([cr[i:i + 1] * bbr_h - ci[i:i + 1] * bbi_h for i in range(gsz)], axis=0)
    cbi = jnp.concatenate([cr[i:i + 1] * bbi_h + ci[i:i + 1] * bbr_h for i in range(gsz)], axis=0)
    lag = lax.broadcasted_iota(I32, (p, t), 1).astype(F32)
    lrc_h, lic_h, dtc_h = lrc[:p], lic[:p], dtc[:p]
    pmag = jnp.exp(lrc_h * dtc_h * lag)
    pth = lic_h * dtc_h * lag
    kv_scr[...] = (jnp.dot(cbr, pmag * jnp.cos(pth), preferred_element_type=F32, precision=HIGHEST)
                   - jnp.dot(cbi, pmag * jnp.sin(pth), preferred_element_type=F32, precision=HIGHEST))

    row = lax.broadcasted_iota(I32, (t, t), 0)
    col = lax.broadcasted_iota(I32, (t, t), 1)
    causal = col >= row

    def toeplitz_cols(i, carry):
        c0 = pl.multiple_of(i * t, t)
        for k in range(gsz):
            vec = kv_scr[pl.ds(i * gsz + k, 1), :]
            blk = pltpu.roll(jnp.broadcast_to(vec, (t, t)), 0, 1, stride=1, stride_axis=0)
            w_scr[k * t:(k + 1) * t, pl.ds(c0, t)] = jnp.where(causal, blk, 0.0).astype(BF16)
        return carry

    lax.fori_loop(0, gsz, toeplitz_cols, 0)

    lo = lax.broadcasted_iota(I32, (t, 2 * p), 1) < p
    rem = (t - 1 - lax.broadcasted_iota(I32, (t, 2 * p), 0)).astype(F32)
    emag = jnp.exp(lrr * dtr * rem)
    eth = lir * dtr * rem
    ecs, esn = emag * jnp.cos(eth), emag * jnp.sin(eth)
    e_same, e_swap = jnp.where(lo, ecs, esn), jnp.where(lo, esn, ecs)
    sgn = jnp.where(lo[:1], -1.0, 1.0)
    wb = jnp.concatenate([e_same * bbr[k:k + 1] + e_swap * (bbi[k:k + 1] * sgn) for k in range(gsz)],
                         axis=0).astype(BF16)

    top = lax.broadcasted_iota(I32, (2 * p, t), 0) < p
    step = (lax.broadcasted_iota(I32, (2 * p, t), 1) + 1).astype(F32)
    cmag = jnp.exp(lrc * dtc * step)
    cth = lic * dtc * step
    mc, ms = cmag * jnp.cos(cth), cmag * jnp.sin(cth)
    crt, cit = crt_ref[0], cit_ref[0]
    wc = jnp.concatenate(
        [jnp.where(top, crt[:, i:i + 1] * mc - cit[:, i:i + 1] * ms,
                   -(crt[:, i:i + 1] * ms + cit[:, i:i + 1] * mc)) for i in range(gsz)],
        axis=1).astype(BF16)

    u = u_ref[0]
    y = jnp.dot(u, w_scr[...], preferred_element_type=F32)
    g = jnp.dot(u, wb, preferred_element_type=F32)
    gr, gi = g[:, :p], g[:, p:]

    m_t = jnp.exp(lrr[:, :p] * dtr[:, :p] * float(t))
    at_r = m_t * jnp.cos(lir[:, :p] * dtr[:, :p] * float(t))
    at_i = m_t * jnp.sin(lir[:, :p] * dtr[:, :p] * float(t))
    nc = u.shape[0] // bsz
    hr = jnp.zeros((bsz, p), F32)
    hi = jnp.zeros((bsz, p), F32)
    hrs, his = [hr], [hi]
    for c in range(1, nc):
        sl = slice((c - 1) * bsz, c * bsz)
        hr, hi = at_r * hr - at_i * hi + gr[sl], at_r * hi + at_i * hr + gi[sl]
        hrs.append(hr)
        his.append(hi)
    hcat = jnp.concatenate([jnp.concatenate(hrs, axis=0), jnp.concatenate(his, axis=0)], axis=1)
    y = y + jnp.dot(hcat.astype(BF16), wc, preferred_element_type=F32)
    y_ref[0] = (y + dv_ref[0] * u.astype(F32)).astype(BF16)


def _s5(u_ssm, lam_re, lam_im, log_dt, b_re, b_im, c_re, c_im, d_skip):
    bsz, seq, _ = u_ssm.shape
    t, g, gs, p = S5_CHUNK, SSM_GROUPS, SSM_GROUP, SSM_STATE
    nc = seq // t
    rows = nc * bsz
    ug = u_ssm.reshape(bsz, nc, t, g, gs).transpose(3, 1, 0, 4, 2).reshape(g, rows, gs * t)

    dup_r = lambda a: jnp.concatenate([a, a], axis=-1)[:, None, :]
    dup_c = lambda a: jnp.concatenate([a, a], axis=-1)[:, :, None]
    ld = jnp.broadcast_to(log_dt[:, None], (g, p))
    brt = jnp.swapaxes(b_re, 1, 2)
    bit = jnp.swapaxes(b_im, 1, 2)
    crt = jnp.swapaxes(c_re, 1, 2)
    cit = jnp.swapaxes(c_im, 1, 2)
    args = (
        ug,
        dup_r(lam_re), dup_r(lam_im), dup_r(ld), dup_c(lam_re), dup_c(lam_im), dup_c(ld),
        jnp.concatenate([brt, brt], -1), jnp.concatenate([bit, bit], -1),
        c_re, c_im, jnp.concatenate([crt, crt], 1), jnp.concatenate([cit, cit], 1),
        jnp.repeat(d_skip.reshape(g, gs), t, axis=1)[:, None, :],
    )
    per_g = lambda a: pl.BlockSpec((1,) + a.shape[1:], lambda i: (i,) + (0,) * (a.ndim - 1))
    yg = pl.pallas_call(
        functools.partial(_s5_kernel, bsz=bsz),
        out_shape=jax.ShapeDtypeStruct((g, rows, gs * t), BF16),
        grid=(g,),
        in_specs=[per_g(a) for a in args],
        out_specs=pl.BlockSpec((1, rows, gs * t), lambda i: (i, 0, 0)),
        scratch_shapes=[pltpu.VMEM((gs * t, gs * t), BF16), pltpu.VMEM((gs * gs, t), F32)],
        compiler_params=_cparams("arbitrary"),
        name="s5",
    )(*args)
    return yg.reshape(g, nc, bsz, gs, t).transpose(2, 1, 4, 0, 3).reshape(bsz, seq, g * gs)


def _dsa_kernel(qt_ref, qit_ref, wt_ref, k_ref, vt_ref, ki_ref, y_ref,
                key_scr, hi_scr, lo_scr, m_scr, acc_scr, sacc_scr, s_scr, lg_scr, p_scr, *, n_sel):
    tq, tk = Q_TILE, K_TILE
    qi = pl.program_id(1)
    nkt = qi + 1
    q0 = qi * tq
    diff = (lax.broadcasted_iota(I32, (tk, tq), 0) - lax.broadcasted_iota(I32, (tk, tq), 1))

    npair = (nkt + 1) // 2
    rows = lambda kt: pl.ds(pl.multiple_of(kt * tk, tk), tk)

    def score_tile(kt, causal_mask):
        k0 = kt * tk
        ki_t = ki_ref[0, rows(kt), :]
        s = jnp.zeros((tk, tq), F32)
        for h in range(IDX_HEADS):
            rel = jnp.dot(ki_t, qit_ref[0, h * IDX_DIM:(h + 1) * IDX_DIM, :], preferred_element_type=F32)
            s = s + jnp.maximum(rel, 0.0) * wt_ref[0, h:h + 1, :]
        bits = lax.bitcast_convert_type(s, I32)
        sign = bits >> 31
        key = ((bits & 0x7FFFFFFF) ^ sign) - sign
        if causal_mask:
            key = jnp.where(diff <= q0 - k0, key, INT_MIN)
        key_scr[rows(kt), :] = key
        hi_scr[rows(kt), :] = (key >> 16).astype(I16)
        lo_scr[rows(kt), :] = ((key & 0xFFFF) - HALF16).astype(I16)

    def score_pair(j, carry):
        score_tile(2 * j, False)
        score_tile(2 * j + 1, False)
        return carry

    lax.fori_loop(0, npair - 1, score_pair, 0)
    score_tile(2 * npair - 2, True)
    score_tile(2 * npair - 1, True)

    def tile_loop(fn, carry):
        def group(g, c):
            for u in range(TILE_GROUP):
                c = fn(g * TILE_GROUP + u, c)
            return c

        def pair(j, c):
            return fn(2 * j + 1, fn(2 * j, c))

        pairs_per_group = TILE_GROUP // 2
        ngrp = npair // pairs_per_group
        carry = lax.fori_loop(0, ngrp, group, carry)
        return lax.fori_loop(ngrp * pairs_per_group, npair, pair, carry)

    def tree_sum(parts):
        while len(parts) > 1:
            parts = [a + b for a, b in zip(parts[::2], parts[1::2])]
        return parts[0]

    def count16(ref, pred, with_tile=False):
        def tile(kt, acc):
            vals = ref[rows(kt), :]
            hit = jnp.where(pred(vals, kt) if with_tile else pred(vals), jnp.int16(1), jnp.int16(0))
            return acc + tree_sum([hit[r:r + PACK16] for r in range(0, tk, PACK16)])
        part = tile_loop(tile, jnp.zeros((PACK16, tq), I16))
        return jnp.sum(part.astype(I32), axis=0, keepdims=True)

    def search16(ref, target):
        def bit_step(i, state):
            c_u, c_cnt = state
            trial = c_u | (jnp.int32(1) << (15 - i))
            trial_s = (trial - HALF16).astype(I16)
            cnt = count16(ref, lambda v: v >= trial_s)
            ok = cnt >= target
            return jnp.where(ok, trial, c_u), jnp.where(ok, cnt, c_cnt)
        zero = jnp.zeros((1, tq), I32)
        return lax.fori_loop(0, 16, bit_step, (zero, zero))

    hi_u, n_hi_ge = search16(hi_scr, n_sel)
    hi_thr = hi_u - HALF16
    hi_thr16 = hi_thr.astype(I16)

    def keep_bucket(hi, kt):
        lo_scr[rows(kt), :] = jnp.where(hi == hi_thr16, lo_scr[rows(kt), :], jnp.int16(-HALF16))
        return hi > hi_thr16

    cnt_above = count16(hi_scr, keep_bucket, with_tile=True)
    lo_thr, n_lo_ge = search16(lo_scr, n_sel - cnt_above)
    thr = jnp.maximum(hi_thr * (2 * HALF16) + lo_thr, INT_MIN + 1)
    in_bucket = jnp.where(lo_thr != 0, n_lo_ge, n_hi_ge - cnt_above)
    cnt_ge = jnp.where(hi_u != 0, cnt_above + in_bucket, 0)

    neg_bits = np.float32(NEG_BIG).view(np.int32)

    def to_bias(kt, carry):
        keys = key_scr[rows(kt), :]
        key_scr[rows(kt), :] = jnp.where(keys > thr, 0, jnp.where(keys == thr, INT_MIN, neg_bits))
        return carry

    tile_loop(to_bias, 0)

    @pl.when(jnp.max(cnt_ge) > n_sel)
    def _():
        lo_thr16 = (lo_thr - HALF16).astype(I16)
        cnt_gt = cnt_above + count16(lo_scr, lambda v: v > lo_thr16)
        need = n_sel - cnt_gt
        sub = lax.broadcasted_iota(I32, (tk, tq), 0)
        no_tie = jnp.int16(HALF16 - 1)

        def tie_index(kt, state):
            seen, cut_tile = state
            idx = (sub + kt * tk).astype(I16)
            at_lo = jnp.where(lo_scr[rows(kt), :] == lo_thr16, idx, no_tie)
            tie = jnp.where(hi_scr[rows(kt), :] == hi_thr16, at_lo, no_tie)
            lo_scr[rows(kt), :] = tie
            hit = jnp.where(tie != no_tie, jnp.int16(1), jnp.int16(0))
            here = tree_sum([hit[r:r + PACK16] for r in range(0, tk, PACK16)])
            after = seen + jnp.sum(here.astype(I32), axis=0, keepdims=True)
            return after, jnp.where((seen <= need) & (after > need), kt, cut_tile)

        zero = jnp.zeros((1, tq), I32)
        _, cut_tile = tile_loop(tie_index, (zero, zero + 2 * npair))

        tile_bits = tk.bit_length() - 1

        def idx_step(i, j_cut):
            trial = j_cut | (jnp.int32(1) << (tile_bits - 1 - i))
            trial16 = trial.astype(I16)
            cnt = count16(lo_scr, lambda v: v < trial16)
            return jnp.where(cnt <= need, trial, j_cut)

        j_cut = lax.fori_loop(0, tile_bits, idx_step, cut_tile << tile_bits)

        def drop(kt, carry):
            bits = key_scr[rows(kt), :]
            key_scr[rows(kt), :] = jnp.where((bits == INT_MIN) & (sub + kt * tk >= j_cut), neg_bits, bits)
            return carry

        tile_loop(drop, 0)

    ta = ATTN_TILES * tk
    nstep = (nkt + ATTN_TILES - 1) // ATTN_TILES
    m_scr[...] = jnp.full(m_scr.shape, NEG_BIG, F32)
    sacc_scr[...] = jnp.full(sacc_scr.shape, NEG_BIG, F32)
    acc_scr[...] = jnp.zeros(acc_scr.shape, F32)
    ones_rows = jnp.ones((PACK16, ta), BF16)
    span = lambda i: pl.ds(pl.multiple_of(i * ta, ta), ta)

    def logits(i):
        for h in range(N_HEADS):
            g = h // KV_GROUP
            lg = (jnp.dot(k_ref[0, span(i), g * HEAD_DIM:(g + 1) * HEAD_DIM],
                          qt_ref[0, h * HEAD_DIM:(h + 1) * HEAD_DIM, :], preferred_element_type=F32)
                  + lax.bitcast_convert_type(key_scr[span(i), :], F32))
            lg_scr[h] = lg
            m_scr[h] = jnp.maximum(m_scr[h], jnp.max(lg, axis=0, keepdims=True))

    def probs():
        for h in range(N_HEADS):
            s = m_scr[h]
            s_scr[h] = s
            p_scr[h] = jnp.exp2(lg_scr[h] - s).astype(BF16)

    def accumulate(i):
        for h in range(N_HEADS):
            g = h // KV_GROUP
            vt_t = jnp.concatenate([vt_ref[0, g * HEAD_DIM:(g + 1) * HEAD_DIM, span(i)], ones_rows], axis=0)
            s_new = s_scr[h]
            alpha = jnp.exp2(sacc_scr[h] - s_new)
            acc_scr[h] = alpha * acc_scr[h] + jnp.dot(vt_t, p_scr[h], preferred_element_type=F32)
            sacc_scr[h] = s_new

    logits(0)

    @pl.when(nstep == 1)
    def _():
        probs()
        accumulate(0)

    @pl.when(nstep > 1)
    def _():
        probs()
        logits(1)

        def step(i, carry):
            accumulate(i - 1)
            probs()
            logits(i + 1)
            return carry

        lax.fori_loop(1, nstep - 1, step, 0)
        accumulate(nstep - 2)
        probs()
        accumulate(nstep - 1)

    for h in range(N_HEADS):
        o = acc_scr[h, :HEAD_DIM] / acc_scr[h, HEAD_DIM:HEAD_DIM + 1]
        y_ref[0, :, h * HEAD_DIM:(h + 1) * HEAD_DIM] = o.T.astype(BF16)


def _dsa(qt, qit, wtt, k, vt, ki):
    bsz, seq, _ = k.shape
    n_sel = min(TOPK_MAX, seq // 4)
    tq = Q_TILE
    assert Q_TILE == K_TILE and seq % (2 * K_TILE) == 0 and seq < HALF16
    kern = functools.partial(_dsa_kernel, n_sel=n_sel)
    return pl.pallas_call(
        kern,
        out_shape=jax.ShapeDtypeStruct((bsz, seq, ATTN_WIDTH), BF16),
        grid=(bsz, seq // tq),
        in_specs=[
            pl.BlockSpec((1, ATTN_WIDTH, tq), lambda b, i: (b, 0, i)),
            pl.BlockSpec((1, IDX_WIDTH, tq), lambda b, i: (b, 0, i)),
            pl.BlockSpec((1, IDX_HEADS, tq), lambda b, i: (b, 0, i)),
            pl.BlockSpec((1, seq, KV_WIDTH), lambda b, i: (b, 0, 0)),
            pl.BlockSpec((1, KV_WIDTH, seq), lambda b, i: (b, 0, 0)),
            pl.BlockSpec((1, seq, IDX_DIM), lambda b, i: (b, 0, 0)),
        ],
        out_specs=pl.BlockSpec((1, tq, ATTN_WIDTH), lambda b, i: (b, i, 0)),
        scratch_shapes=[
            pltpu.VMEM((seq, tq), I32),
            pltpu.VMEM((seq, tq), I16),
            pltpu.VMEM((seq, tq), I16),
            pltpu.VMEM((N_HEADS, 1, tq), F32),
            pltpu.VMEM((N_HEADS, HEAD_DIM + PACK16, tq), F32),
            pltpu.VMEM((N_HEADS, 1, tq), F32),
            pltpu.VMEM((N_HEADS, 1, tq), F32),
            pltpu.VMEM((N_HEADS, ATTN_TILES * K_TILE, tq), F32),
            pltpu.VMEM((N_HEADS, ATTN_TILES * K_TILE, tq), BF16),
        ],
        compiler_params=_cparams("arbitrary", "arbitrary"),
        name="dsa",
    )(qt, qit, wtt, k, vt, ki)


def _deepnorm_ln(x, h, gate, g, b):
    y = DEEPNORM_ALPHA * x + (1.0 + gate) * h
    mu = jnp.mean(y, axis=-1, keepdims=True)
    yc = y - mu
    var = jnp.mean(yc * yc, axis=-1, keepdims=True)
    return yc * lax.rsqrt(var + LN_EPS) * g + b


def _merge_kernel(x_ref, ys_ref, ya_ref, sg_ref, gt_ref, wglu_ref, bglu_ref, pssm_ref, pattn_ref,
                  wout_ref, lng_ref, lnb_ref, o_ref):
    y = jax.nn.gelu(ys_ref[0].astype(F32))
    glu = jnp.dot(y.astype(BF16), wglu_ref[...], preferred_element_type=F32) + bglu_ref[...]
    y = (y * jax.nn.sigmoid(glu)).astype(BF16)
    sg = sg_ref[0]
    merged = (sg[:, :D_MODEL].astype(F32) * jnp.dot(y, pssm_ref[...], preferred_element_type=F32)
              + sg[:, D_MODEL:].astype(F32) * jnp.dot(ya_ref[0], pattn_ref[...], preferred_element_type=F32))
    h = jnp.dot(merged.astype(BF16), wout_ref[...], preferred_element_type=F32)
    o_ref[0] = _deepnorm_ln(x_ref[0], h, gt_ref[0], lng_ref[...], lnb_ref[...])


def _merge(x, ys, ya, sg, gt, wglu, bglu, pssm, pattn, wout, lng, lnb, tm):
    bsz, seq, d = x.shape
    tok = lambda w: pl.BlockSpec((1, tm, w), lambda b, i: (b, i, 0))
    vec = pl.BlockSpec((1, 1, d), lambda b, i: (b, 0, 0))
    const = lambda a: pl.BlockSpec(a.shape, lambda b, i: (0, 0))
    return pl.pallas_call(
        _merge_kernel,
        out_shape=jax.ShapeDtypeStruct(x.shape, F32),
        grid=(bsz, seq // tm),
        in_specs=[tok(d), tok(SSM_WIDTH), tok(ATTN_WIDTH), tok(2 * D_MODEL), vec,
                  const(wglu), const(bglu), const(pssm), const(pattn), const(wout), const(lng), const(lnb)],
        out_specs=tok(d),
        compiler_params=_cparams("arbitrary", "arbitrary"),
        name="merge",
    )(x, ys, ya, sg, gt, wglu, bglu, pssm, pattn, wout, lng, lnb)


def _ffn_kernel(x_ref, sc_ref, sh_ref, gt_ref, wgu_ref, wd_ref, lng_ref, lnb_ref, o_ref, hid_scr):
    x = x_ref[0]
    u = (x * (1.0 + sc_ref[0]) + sh_ref[0]).astype(BF16)
    for j in range(D_FF // FF_CHUNK):
        a = jnp.dot(u, wgu_ref[:, j * FF_CHUNK:(j + 1) * FF_CHUNK], preferred_element_type=F32)
        b = jnp.dot(u, wgu_ref[:, D_FF + j * FF_CHUNK:D_FF + (j + 1) * FF_CHUNK], preferred_element_type=F32)
        hid_scr[:, j * FF_CHUNK:(j + 1) * FF_CHUNK] = (a * jax.nn.sigmoid(a) * b).astype(BF16)
    f = jnp.dot(hid_scr[...], wd_ref[...], preferred_element_type=F32)
    o_ref[0] = _deepnorm_ln(x, f, gt_ref[0], lng_ref[...], lnb_ref[...])


def _ffn(x, sc, sh, gt, wgu, wd, lng, lnb, tm):
    bsz, seq, d = x.shape
    tok = pl.BlockSpec((1, tm, d), lambda b, i: (b, i, 0))
    vec = pl.BlockSpec((1, 1, d), lambda b, i: (b, 0, 0))
    const = lambda a: pl.BlockSpec(a.shape, lambda b, i: (0, 0))
    return pl.pallas_call(
        _ffn_kernel,
        out_shape=jax.ShapeDtypeStruct(x.shape, F32),
        grid=(bsz, seq // tm),
        in_specs=[tok, vec, vec, vec, const(wgu), const(wd), const(lng), const(lnb)],
        out_specs=tok,
        scratch_shapes=[pltpu.VMEM((tm, D_FF), BF16)],
        compiler_params=_cparams("arbitrary", "arbitrary"),
        name="ffn",
    )(x, sc, sh, gt, wgu, wd, lng, lnb)


def kernel(x, c, w_cond, b_cond, w_in, ssm_lam_re, ssm_lam_im, ssm_log_dt, ssm_b_re, ssm_b_im,
           ssm_c_re, ssm_c_im, ssm_d, ssm_w_glu, ssm_b_glu, p_ssm, p_attn, w_out,
           ln1_g, ln1_b, w_gate_up, w_down, ln2_g, ln2_b):
    bsz, seq, d = x.shape
    assert d == D_MODEL and seq % Q_TILE == 0 and seq % S5_CHUNK == 0
    tm = min(TOKEN_TILE, seq)
    assert seq % tm == 0
    tabs = _rope_tables(seq)
    mod = _cond(c, w_cond, b_cond)
    row = lambda a: a.reshape(1, -1)
    for l in range(DEPTH):
        sh1, sc1, gt1, sh2, sc2, gt2 = [mod[l, :, i * d:(i + 1) * d][:, None, :] for i in range(6)]
        wn, wt = _split_w_in(w_in[l])
        u_ssm, k, ki, sg, qt, vt, qit, wtt = _inproj(x, sc1, sh1, wn, wt, tabs, tm)
        ys = _s5(u_ssm, ssm_lam_re[l], ssm_lam_im[l], ssm_log_dt[l], ssm_b_re[l], ssm_b_im[l],
                 ssm_c_re[l], ssm_c_im[l], ssm_d[l])
        ya = _dsa(qt, qit, wtt, k, vt, ki)
        x = _merge(x, ys, ya, sg, gt1, ssm_w_glu[l].astype(BF16), row(ssm_b_glu[l]),
                   p_ssm[l].astype(BF16), p_attn[l].astype(BF16), w_out[l].astype(BF16),
                   row(ln1_g[l]), row(ln1_b[l]), tm)
        x = _ffn(x, sc2, sh2, gt2, w_gate_up[l].astype(BF16), w_down[l].astype(BF16),
                 row(ln2_g[l]), row(ln2_b[l]), tm)
    return x
```

```python
import functools
import math

import jax
import jax.numpy as jnp
import numpy as np
from jax import lax
from jax.experimental import pallas as pl
from jax.experimental.pallas import tpu as pltpu

F32 = jnp.float32
BF16 = jnp.bfloat16
I32 = jnp.int32
I16 = jnp.int16
HIGHEST = lax.Precision.HIGHEST

D_MODEL = 1024
DEPTH = 2
SSM_WIDTH = 512
SSM_GROUP = 16
SSM_GROUPS = SSM_WIDTH // SSM_GROUP
SSM_STATE = 64
HEAD_DIM = 128
N_HEADS = D_MODEL // HEAD_DIM
N_KV_HEADS = 2
KV_GROUP = N_HEADS // N_KV_HEADS
ATTN_WIDTH = N_HEADS * HEAD_DIM
KV_WIDTH = N_KV_HEADS * HEAD_DIM
IDX_HEADS = 8
IDX_DIM = 64
IDX_WIDTH = IDX_HEADS * IDX_DIM
IDX_SCALE = (IDX_HEADS * IDX_DIM) ** -0.5
TOPK_MAX = 256
ROPE_THETA = 10000.0
D_FF = -(-8 * D_MODEL // (3 * 256)) * 256
DEEPNORM_ALPHA = (2 * DEPTH) ** 0.25
LN_EPS = 1e-5
IN_SIZES = (SSM_WIDTH, ATTN_WIDTH, KV_WIDTH, KV_WIDTH, IDX_WIDTH, IDX_DIM, IDX_HEADS, D_MODEL, D_MODEL)
IN_OFFSETS = [0] + [int(o) for o in np.cumsum(IN_SIZES)]

LANES = 128
SUBLANES = 8
VMEM_LIMIT_BYTES = 56 * 1024 * 1024

COND_TILE = 1536
TOKEN_TILE = 512
S5_CHUNK = LANES
Q_TILE = 256
K_TILE = 256
TILE_GROUP = 4
ATTN_TILES = 2
FF_CHUNK = 256
INT_MIN = -(2 ** 31)
NEG_BIG = -1e30
HALF16 = 2 ** 15
PACK16 = 2 * SUBLANES
Q_SCALE = HEAD_DIM ** -0.5 * math.log2(math.e)


def _cparams(*sem):
    return pltpu.CompilerParams(dimension_semantics=sem, vmem_limit_bytes=VMEM_LIMIT_BYTES)


def _cond_kernel(c_ref, w_ref, b_ref, o_ref):
    c = c_ref[...]
    s = c * jax.nn.sigmoid(c)
    o_ref[0] = jnp.dot(s, w_ref[0], preferred_element_type=F32, precision=HIGHEST) + b_ref[0]


def _cond(c, w_cond, b_cond):
    depth, d, n = w_cond.shape
    bsz = c.shape[0]
    tn = COND_TILE
    assert n % tn == 0
    return pl.pallas_call(
        _cond_kernel,
        out_shape=jax.ShapeDtypeStruct((depth, bsz, n), F32),
        grid=(depth, n // tn),
        in_specs=[
            pl.BlockSpec((bsz, d), lambda l, j: (0, 0)),
            pl.BlockSpec((1, d, tn), lambda l, j: (l, 0, j)),
            pl.BlockSpec((1, 1, tn), lambda l, j: (l, 0, j)),
        ],
        out_specs=pl.BlockSpec((1, bsz, tn), lambda l, j: (l, 0, j)),
        compiler_params=_cparams("arbitrary", "arbitrary"),
        name="cond",
    )(c, w_cond, b_cond.reshape(depth, 1, n))


NAT_SSM = 0
NAT_K = NAT_SSM + SSM_WIDTH
NAT_KI = NAT_K + KV_WIDTH
NAT_G = NAT_KI + LANES
NAT_WIDTH = NAT_G + 2 * D_MODEL
TR_Q = 0
TR_V = TR_Q + ATTN_WIDTH
TR_QI = TR_V + KV_WIDTH
TR_W = TR_QI + IDX_WIDTH
TR_ROWS = -(-(TR_W + IDX_HEADS) // PACK16) * PACK16


def _inproj_kernel(x_ref, sc_ref, sh_ref, wn_ref, wt_ref, cn_ref, sn_ref, cki_ref, ski_ref,
                   ct_ref, st_ref, ct32_ref, st32_ref,
                   ssm_ref, k_ref, ki_ref, sg_ref, qt_ref, vt_ref, qit_ref, wtt_ref):
    u = (x_ref[0] * (1.0 + sc_ref[0]) + sh_ref[0]).astype(BF16)
    zn = jnp.dot(u, wn_ref[...], preferred_element_type=F32)
    zt = lax.dot_general(wt_ref[...], u, (((1,), (1,)), ((), ())),
                         preferred_element_type=F32)

    ssm_ref[0] = zn[:, NAT_SSM:NAT_SSM + SSM_WIDTH].astype(BF16)
    sg_ref[0] = jax.nn.sigmoid(zn[:, NAT_G:NAT_G + 2 * D_MODEL]).astype(BF16)

    cn, sn = cn_ref[...], sn_ref[...]
    for h in range(N_KV_HEADS):
        xh = zn[:, NAT_K + h * HEAD_DIM:NAT_K + (h + 1) * HEAD_DIM]
        rot = pltpu.roll(xh, HEAD_DIM // 2, axis=1)
        k_ref[0, :, h * HEAD_DIM:(h + 1) * HEAD_DIM] = (xh * cn + rot * sn).astype(BF16)

    xk = zn[:, NAT_KI:NAT_KI + LANES]
    lane = lax.broadcasted_iota(I32, xk.shape, 1)
    rot = jnp.where((lane & (IDX_DIM // 2)) == 0,
                    pltpu.roll(xk, LANES - IDX_DIM // 2, axis=1),
                    pltpu.roll(xk, IDX_DIM // 2, axis=1))
    ki_ref[0] = (xk * cki_ref[...] + rot * ski_ref[...])[:, :IDX_DIM].astype(BF16)

    ct, st = ct_ref[...], st_ref[...]
    half = HEAD_DIM // 2
    for h in range(N_HEADS):
        r0 = TR_Q + h * HEAD_DIM
        x1, x2 = zt[r0:r0 + half], zt[r0 + half:r0 + HEAD_DIM]
        qt_ref[0, h * HEAD_DIM:h * HEAD_DIM + half, :] = ((x1 * ct - x2 * st) * Q_SCALE).astype(BF16)
        qt_ref[0, h * HEAD_DIM + half:(h + 1) * HEAD_DIM, :] = ((x2 * ct + x1 * st) * Q_SCALE).astype(BF16)
    ct32, st32 = ct32_ref[...], st32_ref[...]
    half = IDX_DIM // 2
    for h in range(IDX_HEADS):
        r0 = TR_QI + h * IDX_DIM
        x1, x2 = zt[r0:r0 + half], zt[r0 + half:r0 + IDX_DIM]
        qit_ref[0, h * IDX_DIM:h * IDX_DIM + half, :] = (x1 * ct32 - x2 * st32).astype(BF16)
        qit_ref[0, h * IDX_DIM + half:(h + 1) * IDX_DIM, :] = (x2 * ct32 + x1 * st32).astype(BF16)

    vt_ref[0] = zt[TR_V:TR_V + KV_WIDTH].astype(BF16)
    wtt_ref[0] = zt[TR_W:TR_W + IDX_HEADS] * IDX_SCALE


def _rope_tables(seq):
    pos = jnp.arange(seq, dtype=F32)

    def cs(half):
        inv = ROPE_THETA ** (-jnp.arange(half, dtype=F32) / half)
        ang = pos[:, None] * inv[None, :]
        return jnp.cos(ang), jnp.sin(ang)

    c64, s64 = cs(HEAD_DIM // 2)
    c32, s32 = cs(IDX_DIM // 2)
    z = jnp.zeros((seq, LANES - IDX_DIM), F32)
    return dict(
        cn=jnp.concatenate([c64, c64], -1), sn=jnp.concatenate([-s64, s64], -1),
        cki=jnp.concatenate([c32, c32, z], -1), ski=jnp.concatenate([-s32, s32, z], -1),
        ct=c64.T, st=s64.T, ct32=c32.T, st32=s32.T)


def _split_w_in(w_in):
    o = IN_OFFSETS
    ssm, q, k, v, qi, ki, wi, gs, ga = [w_in[:, o[i]:o[i + 1]] for i in range(9)]
    zpad = jnp.zeros((D_MODEL, LANES - IDX_DIM), w_in.dtype)
    wn = jnp.concatenate([ssm, k, ki, zpad, gs, ga], axis=1).astype(BF16)
    wt = jnp.concatenate([q, v, qi, wi, jnp.zeros((D_MODEL, TR_ROWS - TR_W - IDX_HEADS), w_in.dtype)],
                         axis=1).T.astype(BF16)
    return wn, wt


def _inproj(x, sc, sh, wn, wt, tabs, tm):
    bsz, seq, d = x.shape
    grid = (bsz, seq // tm)
    tok = lambda w: pl.BlockSpec((1, tm, w), lambda b, i: (b, i, 0))
    trn = lambda r: pl.BlockSpec((1, r, tm), lambda b, i: (b, 0, i))
    vec = pl.BlockSpec((1, 1, d), lambda b, i: (b, 0, 0))
    const = lambda a: pl.BlockSpec(a.shape, lambda b, i: (0, 0))
    ntab = lambda: pl.BlockSpec((tm, LANES), lambda b, i: (i, 0))
    ttab = lambda r: pl.BlockSpec((r, tm), lambda b, i: (0, i))
    out_shape = (
        jax.ShapeDtypeStruct((bsz, seq, SSM_WIDTH), BF16),
        jax.ShapeDtypeStruct((bsz, seq, KV_WIDTH), BF16),
        jax.ShapeDtypeStruct((bsz, seq, IDX_DIM), BF16),
        jax.ShapeDtypeStruct((bsz, seq, 2 * D_MODEL), BF16),
        jax.ShapeDtypeStruct((bsz, ATTN_WIDTH, seq), BF16),
        jax.ShapeDtypeStruct((bsz, KV_WIDTH, seq), BF16),
        jax.ShapeDtypeStruct((bsz, IDX_WIDTH, seq), BF16),
        jax.ShapeDtypeStruct((bsz, IDX_HEADS, seq), F32),
    )
    return pl.pallas_call(
        _inproj_kernel,
        out_shape=out_shape,
        grid=grid,
        in_specs=[tok(d), vec, vec, const(wn), const(wt), ntab(), ntab(), ntab(), ntab(),
                  ttab(HEAD_DIM // 2), ttab(HEAD_DIM // 2), ttab(IDX_DIM // 2), ttab(IDX_DIM // 2)],
        out_specs=(tok(SSM_WIDTH), tok(KV_WIDTH), tok(IDX_DIM), tok(2 * D_MODEL),
                   trn(ATTN_WIDTH), trn(KV_WIDTH), trn(IDX_WIDTH), trn(IDX_HEADS)),
        compiler_params=_cparams("arbitrary", "arbitrary"),
        name="in_proj",
    )(x, sc, sh, wn, wt, tabs["cn"], tabs["sn"], tabs["cki"], tabs["ski"],
      tabs["ct"], tabs["st"], tabs["ct32"], tabs["st32"])


def _s5_kernel(u_ref, lrr_ref, lir_ref, ldr_ref, lrc_ref, lic_ref, ldc_ref,
               brt_ref, bit_ref, cr_ref, ci_ref, crt_ref, cit_ref, dv_ref,
               y_ref, w_scr, kv_scr, *, bsz):
    t = S5_CHUNK
    p = SSM_STATE
    gsz = SSM_GROUP
    lrr, lir = lrr_ref[0], lir_ref[0]
    dtr = jnp.exp(ldr_ref[0])
    lrc, lic = lrc_ref[0], lic_ref[0]
    dtc = jnp.exp(ldc_ref[0])

    mag = jnp.exp(lrr * dtr)
    ar, ai = mag * jnp.cos(lir * dtr), mag * jnp.sin(lir * dtr)
    den = lrr * lrr + lir * lir
    nr = ar - 1.0
    fr, fi = (nr * lrr + ai * lir) / den, (ai * lrr - nr * lir) / den
    brt, bit = brt_ref[0], bit_ref[0]
    bbr, bbi = fr * brt - fi * bit, fr * bit + fi * brt

    cr, ci = cr_ref[0], ci_ref[0]
    bbr_h, bbi_h = bbr[:, :p], bbi[:, :p]
    cbr = jnp.concatenate([cr[i:i + 1] * bbr_h - ci[i:i + 1] * bbi_h for i in range(gsz)], axis=0)
    cbi = jnp.concatenate([cr[i:i + 1] * bbi_h + ci[i:i + 1] * bbr_h for i in range(gsz)], axis=0)
    lag = lax.broadcasted_iota(I32, (p, t), 1).astype(F32)
    lrc_h, lic_h, dtc_h = lrc[:p], lic[:p], dtc[:p]
    pmag = jnp.exp(lrc_h * dtc_h * lag)
    pth = lic_h * dtc_h * lag
    kv_scr[...] = (jnp.dot(cbr, pmag * jnp.cos(pth), preferred_element_type=F32, precision=HIGHEST)
                   - jnp.dot(cbi, pmag * jnp.sin(pth), preferred_element_type=F32, precision=HIGHEST))

    row = lax.broadcasted_iota(I32, (t, t), 0)
    col = lax.broadcasted_iota(I32, (t, t), 1)
    causal = col >= row

    def toeplitz_cols(i, carry):
        c0 = pl.multiple_of(i * t, t)
        for k in range(gsz):
            vec = kv_scr[pl.ds(i * gsz + k, 1), :]
            blk = pltpu.roll(jnp.broadcast_to(vec, (t, t)), 0, 1, stride=1, stride_axis=0)
            w_scr[k * t:(k + 1) * t, pl.ds(c0, t)] = jnp.where(causal, blk, 0.0).astype(BF16)
        return carry

    lax.fori_loop(0, gsz, toeplitz_cols, 0)

    lo = lax.broadcasted_iota(I32, (t, 2 * p), 1) < p
    rem = (t - 1 - lax.broadcasted_iota(I32, (t, 2 * p), 0)).astype(F32)
    emag = jnp.exp(lrr * dtr * rem)
    eth = lir * dtr * rem
    ecs, esn = emag * jnp.cos(eth), emag * jnp.sin(eth)
    e_same, e_swap = jnp.where(lo, ecs, esn), jnp.where(lo, esn, ecs)
    sgn = jnp.where(lo[:1], -1.0, 1.0)
    wb = jnp.concatenate([e_same * bbr[k:k + 1] + e_swap * (bbi[k:k + 1] * sgn) for k in range(gsz)],
                         axis=0).astype(BF16)

    top = lax.broadcasted_iota(I32, (2 * p, t), 0) < p
    step = (lax.broadcasted_iota(I32, (2 * p, t), 1) + 1).astype(F32)
    cmag = jnp.exp(lrc * dtc * step)
    cth = lic * dtc * step
    mc, ms = cmag * jnp.cos(cth), cmag * jnp.sin(cth)
    crt, cit = crt_ref[0], cit_ref[0]
    wc = jnp.concatenate(
        [jnp.where(top, crt[:, i:i + 1] * mc - cit[:, i:i + 1] * ms,
                   -(crt[:, i:i + 1] * ms + cit[:, i:i + 1] * mc)) for i in range(gsz)],
        axis=1).astype(BF16)

    u = u_ref[0]
    y = jnp.dot(u, w_scr[...], preferred_element_type=F32)
    g = jnp.dot(u, wb, preferred_element_type=F32)
    gr, gi = g[:, :p], g[:, p:]

    m_t = jnp.exp(lrr[:, :p] * dtr[:, :p] * float(t))
    at_r = m_t * jnp.cos(lir[:, :p] * dtr[:, :p] * float(t))
    at_i = m_t * jnp.sin(lir[:, :p] * dtr[:, :p] * float(t))
    nc = u.shape[0] // bsz
    hr = jnp.zeros((bsz, p), F32)
    hi = jnp.zeros((bsz, p), F32)
    hrs, his = [hr], [hi]
    for c in range(1, nc):
        sl = slice((c - 1) * bsz, c * bsz)
        hr, hi = at_r * hr - at_i * hi + gr[sl], at_r * hi + at_i * hr + gi[sl]
        hrs.append(hr)
        his.append(hi)
    hcat = jnp.concatenate([jnp.concatenate(hrs, axis=0), jnp.concatenate(his, axis=0)], axis=1)
    y = y + jnp.dot(hcat.astype(BF16), wc, preferred_element_type=F32)
    y_ref[0] = (y + dv_ref[0] * u.astype(F32)).astype(BF16)


def _s5(u_ssm, lam_re, lam_im, log_dt, b_re, b_im, c_re, c_im, d_skip):
    bsz, seq, _ = u_ssm.shape
    t, g, gs, p = S5_CHUNK, SSM_GROUPS, SSM_GROUP, SSM_STATE
    nc = seq // t
    rows = nc * bsz
    ug = u_ssm.reshape(bsz, nc, t, g, gs).transpose(3, 1, 0, 4, 2).reshape(g, rows, gs * t)

    dup_r = lambda a: jnp.concatenate([a, a], axis=-1)[:, None, :]
    dup_c = lambda a: jnp.concatenate([a, a], axis=-1)[:, :, None]
    ld = jnp.broadcast_to(log_dt[:, None], (g, p))
    brt = jnp.swapaxes(b_re, 1, 2)
    bit = jnp.swapaxes(b_im, 1, 2)
    crt = jnp.swapaxes(c_re, 1, 2)
    cit = jnp.swapaxes(c_im, 1, 2)
    args = (
        ug,
        dup_r(lam_re), dup_r(lam_im), dup_r(ld), dup_c(lam_re), dup_c(lam_im), dup_c(ld),
        jnp.concatenate([brt, brt], -1), jnp.concatenate([bit, bit], -1),
        c_re, c_im, jnp.concatenate([crt, crt], 1), jnp.concatenate([cit, cit], 1),
        jnp.repeat(d_skip.reshape(g, gs), t, axis=1)[:, None, :],
    )
    per_g = lambda a: pl.BlockSpec((1,) + a.shape[1:], lambda i: (i,) + (0,) * (a.ndim - 1))
    yg = pl.pallas_call(
        functools.partial(_s5_kernel, bsz=bsz),
        out_shape=jax.ShapeDtypeStruct((g, rows, gs * t), BF16),
        grid=(g,),
        in_specs=[per_g(a) for a in args],
        out_specs=pl.BlockSpec((1, rows, gs * t), lambda i: (i, 0, 0)),
        scratch_shapes=[pltpu.VMEM((gs * t, gs * t), BF16), pltpu.VMEM((gs * gs, t), F32)],
        compiler_params=_cparams("arbitrary"),
        name="s5",
    )(*args)
    return yg.reshape(g, nc, bsz, gs, t).transpose(2, 1, 4, 0, 3).reshape(bsz, seq, g * gs)


def _dsa_kernel(qt_ref, qit_ref, wt_ref, k_ref, vt_ref, ki_ref, y_ref,
                key_scr, hi_scr, lo_scr, m_scr, acc_scr, sacc_scr, s_scr, lg_scr, p_scr, *, n_sel):
    tq, tk = Q_TILE, K_TILE
    qi = pl.program_id(1)
    nkt = qi + 1
    q0 = qi * tq
    diff = (lax.broadcasted_iota(I32, (tk, tq), 0) - lax.broadcasted_iota(I32, (tk, tq), 1))

    npair = (nkt + 1) // 2
    rows = lambda kt: pl.ds(pl.multiple_of(kt * tk, tk), tk)

    def score_tile(kt, causal_mask):
        k0 = kt * tk
        ki_t = ki_ref[0, rows(kt), :]
        s = jnp.zeros((tk, tq), F32)
        for h in range(IDX_HEADS):
            rel = jnp.dot(ki_t, qit_ref[0, h * IDX_DIM:(h + 1) * IDX_DIM, :], preferred_element_type=F32)
            s = s + jnp.maximum(rel, 0.0) * wt_ref[0, h:h + 1, :]
        bits = lax.bitcast_convert_type(s, I32)
        sign = bits >> 31
        key = ((bits & 0x7FFFFFFF) ^ sign) - sign
        if causal_mask:
            key = jnp.where(diff <= q0 - k0, key, INT_MIN)
        key_scr[rows(kt), :] = key
        hi_scr[rows(kt), :] = (key >> 16).astype(I16)
        lo_scr[rows(kt), :] = ((key & 0xFFFF) - HALF16).astype(I16)

    def score_group(g, carry):
        for u in range(TILE_GROUP):
            score_tile(g * TILE_GROUP + u, False)
        return carry

    def score_pair(j, carry):
        score_tile(2 * j, False)
        score_tile(2 * j + 1, False)
        return carry

    full_groups = (npair - 1) // (TILE_GROUP // 2)
    lax.fori_loop(0, full_groups, score_group, 0)
    lax.fori_loop(full_groups * (TILE_GROUP // 2), npair - 1, score_pair, 0)
    score_tile(2 * npair - 2, True)
    score_tile(2 * npair - 1, True)

    def tile_loop(fn, carry):
        def group(g, c):
            for u in range(TILE_GROUP):
                c = fn(g * TILE_GROUP + u, c)
            return c

        def pair(j, c):
            return fn(2 * j + 1, fn(2 * j, c))

        pairs_per_group = TILE_GROUP // 2
        ngrp = npair // pairs_per_group
        carry = lax.fori_loop(0, ngrp, group, carry)
        return lax.fori_loop(ngrp * pairs_per_group, npair, pair, carry)

    def tree_sum(parts):
        while len(parts) > 1:
            parts = [a + b for a, b in zip(parts[::2], parts[1::2])]
        return parts[0]

    def count16(ref, pred, with_tile=False):
        def tile(kt, acc):
            vals = ref[rows(kt), :]
            hit = jnp.where(pred(vals, kt) if with_tile else pred(vals), jnp.int16(1), jnp.int16(0))
            return acc + tree_sum([hit[r:r + PACK16] for r in range(0, tk, PACK16)])
        part = tile_loop(tile, jnp.zeros((PACK16, tq), I16))
        return jnp.sum(part.astype(I32), axis=0, keepdims=True)

    def search16(ref, target):
        def bit_step(i, state):
            c_u, c_cnt = state
            trial = c_u | (jnp.int32(1) << (15 - i))
            trial_s = (trial - HALF16).astype(I16)
            cnt = count16(ref, lambda v: v >= trial_s)
            ok = cnt >= target
            return jnp.where(ok, trial, c_u), jnp.where(ok, cnt, c_cnt)
        zero = jnp.zeros((1, tq), I32)
        return lax.fori_loop(0, 16, bit_step, (zero, zero))

    hi_u, n_hi_ge = search16(hi_scr, n_sel)
    hi_thr = hi_u - HALF16
    hi_thr16 = hi_thr.astype(I16)

    def keep_bucket(hi, kt):
        lo_scr[rows(kt), :] = jnp.where(hi == hi_thr16, lo_scr[rows(kt), :], jnp.int16(-HALF16))
        return hi > hi_thr16

    cnt_above = count16(hi_scr, keep_bucket, with_tile=True)
    lo_thr, n_lo_ge = search16(lo_scr, n_sel - cnt_above)
    thr = jnp.maximum(hi_thr * (2 * HALF16) + lo_thr, INT_MIN + 1)
    in_bucket = jnp.where(lo_thr != 0, n_lo_ge, n_hi_ge - cnt_above)
    cnt_ge = jnp.where(hi_u != 0, cnt_above + in_bucket, 0)

    neg_bits = np.float32(NEG_BIG).view(np.int32)

    def to_bias(kt, carry):
        keys = key_scr[rows(kt), :]
        key_scr[rows(kt), :] = jnp.where(keys > thr, 0, jnp.where(keys == thr, INT_MIN, neg_bits))
        return carry

    tile_loop(to_bias, 0)

    @pl.when(jnp.max(cnt_ge) > n_sel)
    def _():
        lo_thr16 = (lo_thr - HALF16).astype(I16)
        cnt_gt = cnt_above + count16(lo_scr, lambda v: v > lo_thr16)
        need = n_sel - cnt_gt
        sub = lax.broadcasted_iota(I32, (tk, tq), 0)
        no_tie = jnp.int16(HALF16 - 1)

        def tie_index(kt, state):
            seen, cut_tile = state
            idx = (sub + kt * tk).astype(I16)
            at_lo = jnp.where(lo_scr[rows(kt), :] == lo_thr16, idx, no_tie)
            tie = jnp.where(hi_scr[rows(kt), :] == hi_thr16, at_lo, no_tie)
            lo_scr[rows(kt), :] = tie
            hit = jnp.where(tie != no_tie, jnp.int16(1), jnp.int16(0))
            here = tree_sum([hit[r:r + PACK16] for r in range(0, tk, PACK16)])
            after = seen + jnp.sum(here.astype(I32), axis=0, keepdims=True)
            return after, jnp.where((seen <= need) & (after > need), kt, cut_tile)

        zero = jnp.zeros((1, tq), I32)
        _, cut_tile = tile_loop(tie_index, (zero, zero + 2 * npair))

        tile_bits = tk.bit_length() - 1

        def idx_step(i, j_cut):
            trial = j_cut | (jnp.int32(1) << (tile_bits - 1 - i))
            trial16 = trial.astype(I16)
            cnt = count16(lo_scr, lambda v: v < trial16)
            return jnp.where(cnt <= need, trial, j_cut)

        j_cut = lax.fori_loop(0, tile_bits, idx_step, cut_tile << tile_bits)

        def drop(kt, carry):
            bits = key_scr[rows(kt), :]
            key_scr[rows(kt), :] = jnp.where((bits == INT_MIN) & (sub + kt * tk >= j_cut), neg_bits, bits)
            return carry

        tile_loop(drop, 0)

    ta = ATTN_TILES * tk
    nstep = (nkt + ATTN_TILES - 1) // ATTN_TILES
    m_scr[...] = jnp.full(m_scr.shape, NEG_BIG, F32)
    sacc_scr[...] = jnp.full(sacc_scr.shape, NEG_BIG, F32)
    acc_scr[...] = jnp.zeros(acc_scr.shape, F32)
    ones_rows = jnp.ones((PACK16, ta), BF16)
    span = lambda i: pl.ds(pl.multiple_of(i * ta, ta), ta)

    def logits(i):
        for h in range(N_HEADS):
            g = h // KV_GROUP
            lg = (jnp.dot(k_ref[0, span(i), g * HEAD_DIM:(g + 1) * HEAD_DIM],
                          qt_ref[0, h * HEAD_DIM:(h + 1) * HEAD_DIM, :], preferred_element_type=F32)
                  + lax.bitcast_convert_type(key_scr[span(i), :], F32))
            lg_scr[h] = lg
            m_scr[h] = jnp.maximum(m_scr[h], jnp.max(lg, axis=0, keepdims=True))

    def probs():
        for h in range(N_HEADS):
            s = m_scr[h]
            s_scr[h] = s
            p_scr[h] = jnp.exp2(lg_scr[h] - s).astype(BF16)

    def accumulate(i):
        for h in range(N_HEADS):
            g = h // KV_GROUP
            vt_t = jnp.concatenate([vt_ref[0, g * HEAD_DIM:(g + 1) * HEAD_DIM, span(i)], ones_rows], axis=0)
            s_new = s_scr[h]
            alpha = jnp.exp2(sacc_scr[h] - s_new)
            acc_scr[h] = alpha * acc_scr[h] + jnp.dot(vt_t, p_scr[h], preferred_element_type=F32)
            sacc_scr[h] = s_new

    logits(0)

    @pl.when(nstep == 1)
    def _():
        probs()
        accumulate(0)

    @pl.when(nstep > 1)
    def _():
        probs()
        logits(1)

        def step(i, carry):
            accumulate(i - 1)
            probs()
            logits(i + 1)
            return carry

        lax.fori_loop(1, nstep - 1, step, 0)
        accumulate(nstep - 2)
        probs()
        accumulate(nstep - 1)

    for h in range(N_HEADS):
        o = acc_scr[h, :HEAD_DIM] / acc_scr[h, HEAD_DIM:HEAD_DIM + 1]
        y_ref[0, :, h * HEAD_DIM:(h + 1) * HEAD_DIM] = o.T.astype(BF16)


def _dsa(qt, qit, wtt, k, vt, ki):
    bsz, seq, _ = k.shape
    n_sel = min(TOPK_MAX, seq // 4)
    tq = Q_TILE
    assert Q_TILE == K_TILE and seq % (2 * K_TILE) == 0 and seq < HALF16
    kern = functools.partial(_dsa_kernel, n_sel=n_sel)
    return pl.pallas_call(
        kern,
        out_shape=jax.ShapeDtypeStruct((bsz, seq, ATTN_WIDTH), BF16),
        grid=(bsz, seq // tq),
        in_specs=[
            pl.BlockSpec((1, ATTN_WIDTH, tq), lambda b, i: (b, 0, i)),
            pl.BlockSpec((1, IDX_WIDTH, tq), lambda b, i: (b, 0, i)),
            pl.BlockSpec((1, IDX_HEADS, tq), lambda b, i: (b, 0, i)),
            pl.BlockSpec((1, seq, KV_WIDTH), lambda b, i: (b, 0, 0)),
            pl.BlockSpec((1, KV_WIDTH, seq), lambda b, i: (b, 0, 0)),
            pl.BlockSpec((1, seq, IDX_DIM), lambda b, i: (b, 0, 0)),
        ],
        out_specs=pl.BlockSpec((1, tq, ATTN_WIDTH), lambda b, i: (b, i, 0)),
        scratch_shapes=[
            pltpu.VMEM((seq, tq), I32),
            pltpu.VMEM((seq, tq), I16),
            pltpu.VMEM((seq, tq), I16),
            pltpu.VMEM((N_HEADS, 1, tq), F32),
            pltpu.VMEM((N_HEADS, HEAD_DIM + PACK16, tq), F32),
            pltpu.VMEM((N_HEADS, 1, tq), F32),
            pltpu.VMEM((N_HEADS, 1, tq), F32),
            pltpu.VMEM((N_HEADS, ATTN_TILES * K_TILE, tq), F32),
            pltpu.VMEM((N_HEADS, ATTN_TILES * K_TILE, tq), BF16),
        ],
        compiler_params=_cparams("arbitrary", "arbitrary"),
        name="dsa",
    )(qt, qit, wtt, k, vt, ki)


def _deepnorm_ln(x, h, gate, g, b):
    y = DEEPNORM_ALPHA * x + (1.0 + gate) * h
    mu = jnp.mean(y, axis=-1, keepdims=True)
    yc = y - mu
    var = jnp.mean(yc * yc, axis=-1, keepdims=True)
    return yc * lax.rsqrt(var + LN_EPS) * g + b


def _merge_kernel(x_ref, ys_ref, ya_ref, sg_ref, gt_ref, wglu_ref, bglu_ref, pssm_ref, pattn_ref,
                  wout_ref, lng_ref, lnb_ref, o_ref):
    y = jax.nn.gelu(ys_ref[0].astype(F32))
    glu = jnp.dot(y.astype(BF16), wglu_ref[...], preferred_element_type=F32) + bglu_ref[...]
    y = (y * jax.nn.sigmoid(glu)).astype(BF16)
    sg = sg_ref[0]
    merged = (sg[:, :D_MODEL].astype(F32) * jnp.dot(y, pssm_ref[...], preferred_element_type=F32)
              + sg[:, D_MODEL:].astype(F32) * jnp.dot(ya_ref[0], pattn_ref[...], preferred_element_type=F32))
    h = jnp.dot(merged.astype(BF16), wout_ref[...], preferred_element_type=F32)
    o_ref[0] = _deepnorm_ln(x_ref[0], h, gt_ref[0], lng_ref[...], lnb_ref[...])


def _merge(x, ys, ya, sg, gt, wglu, bglu, pssm, pattn, wout, lng, lnb, tm):
    bsz, seq, d = x.shape
    tok = lambda w: pl.BlockSpec((1, tm, w), lambda b, i: (b, i, 0))
    vec = pl.BlockSpec((1, 1, d), lambda b, i: (b, 0, 0))
    const = lambda a: pl.BlockSpec(a.shape, lambda b, i: (0, 0))
    return pl.pallas_call(
        _merge_kernel,
        out_shape=jax.ShapeDtypeStruct(x.shape, F32),
        grid=(bsz, seq // tm),
        in_specs=[tok(d), tok(SSM_WIDTH), tok(ATTN_WIDTH), tok(2 * D_MODEL), vec,
                  const(wglu), const(bglu), const(pssm), const(pattn), const(wout), const(lng), const(lnb)],
        out_specs=tok(d),
        compiler_params=_cparams("arbitrary", "arbitrary"),
        name="merge",
    )(x, ys, ya, sg, gt, wglu, bglu, pssm, pattn, wout, lng, lnb)


def _ffn_kernel(x_ref, sc_ref, sh_ref, gt_ref, wgu_ref, wd_ref, lng_ref, lnb_ref, o_ref, hid_scr):
    x = x_ref[0]
    u = (x * (1.0 + sc_ref[0]) + sh_ref[0]).astype(BF16)
    for j in range(D_FF // FF_CHUNK):
        a = jnp.dot(u, wgu_ref[:, j * FF_CHUNK:(j + 1) * FF_CHUNK], preferred_element_type=F32)
        b = jnp.dot(u, wgu_ref[:, D_FF + j * FF_CHUNK:D_FF + (j + 1) * FF_CHUNK], preferred_element_type=F32)
        hid_scr[:, j * FF_CHUNK:(j + 1) * FF_CHUNK] = (a * jax.nn.sigmoid(a) * b).astype(BF16)
    f = jnp.dot(hid_scr[...], wd_ref[...], preferred_element_type=F32)
    o_ref[0] = _deepnorm_ln(x, f, gt_ref[0], lng_ref[...], lnb_ref[...])


def _ffn(x, sc, sh, gt, wgu, wd, lng, lnb, tm):
    bsz, seq, d = x.shape
    tok = pl.BlockSpec((1, tm, d), lambda b, i: (b, i, 0))
    vec = pl.BlockSpec((1, 1, d), lambda b, i: (b, 0, 0))
    const = lambda a: pl.BlockSpec(a.shape, lambda b, i: (0, 0))
    return pl.pallas_call(
        _ffn_kernel,
        out_shape=jax.ShapeDtypeStruct(x.shape, F32),
        grid=(bsz, seq // tm),
        in_specs=[tok, vec, vec, vec, const(wgu), const(wd), const(lng), const(lnb)],
        out_specs=tok,
        scratch_shapes=[pltpu.VMEM((tm, D_FF), BF16)],
        compiler_params=_cparams("arbitrary", "arbitrary"),
        name="ffn",
    )(x, sc, sh, gt, wgu, wd, lng, lnb)


def kernel(x, c, w_cond, b_cond, w_in, ssm_lam_re, ssm_lam_im, ssm_log_dt, ssm_b_re, ssm_b_im,
           ssm_c_re, ssm_c_im, ssm_d, ssm_w_glu, ssm_b_glu, p_ssm, p_attn, w_out,
           ln1_g, ln1_b, w_gate_up, w_down, ln2_g, ln2_b):
    bsz, seq, d = x.shape
    assert d == D_MODEL and seq % Q_TILE == 0 and seq % S5_CHUNK == 0
    tm = min(TOKEN_TILE, seq)
    assert seq % tm == 0
    tabs = _rope_tables(seq)
    mod = _cond(c, w_cond, b_cond)
    row = lambda a: a.reshape(1, -1)
    for l in range(DEPTH):
        sh1, sc1, gt1, sh2, sc2, gt2 = [mod[l, :, i * d:(i + 1) * d][:, None, :] for i in range(6)]
        wn, wt = _split_w_in(w_in[l])
        u_ssm, k, ki, sg, qt, vt, qit, wtt = _inproj(x, sc1, sh1, wn, wt, tabs, tm)
        ys = _s5(u_ssm, ssm_lam_re[l], ssm_lam_im[l], ssm_log_dt[l], ssm_b_re[l], ssm_b_im[l],
                 ssm_c_re[l], ssm_c_im[l], ssm_d[l])
        ya = _dsa(qt, qit, wtt, k, vt, ki)
        x = _merge(x, ys, ya, sg, gt1, ssm_w_glu[l].astype(BF16), row(ssm_b_glu[l]),
                   p_ssm[l].astype(BF16), p_attn[l].astype(BF16), w_out[l].astype(BF16),
                   row(ln1_g[l]), row(ln1_b[l]), tm)
        x = _ffn(x, sc2, sh2, gt2, w_gate_up[l].astype(BF16), w_down[l].astype(BF16),
                 row(ln2_g[l]), row(ln2_b[l]), tm)
    return x
```

```python
import functools
import math

import jax
import jax.numpy as jnp
import numpy as np
from jax import lax
from jax.experimental import pallas as pl
from jax.experimental.pallas import tpu as pltpu

F32 = jnp.float32
BF16 = jnp.bfloat16
I32 = jnp.int32
I16 = jnp.int16
HIGHEST = lax.Precision.HIGHEST

D_MODEL = 1024
DEPTH = 2
SSM_WIDTH = 512
SSM_GROUP = 16
SSM_GROUPS = SSM_WIDTH // SSM_GROUP
SSM_STATE = 64
HEAD_DIM = 128
N_HEADS = D_MODEL // HEAD_DIM
N_KV_HEADS = 2
KV_GROUP = N_HEADS // N_KV_HEADS
ATTN_WIDTH = N_HEADS * HEAD_DIM
KV_WIDTH = N_KV_HEADS * HEAD_DIM
IDX_HEADS = 8
IDX_DIM = 64
IDX_WIDTH = IDX_HEADS * IDX_DIM
IDX_SCALE = (IDX_HEADS * IDX_DIM) ** -0.5
TOPK_MAX = 256
ROPE_THETA = 10000.0
D_FF = -(-8 * D_MODEL // (3 * 256)) * 256
DEEPNORM_ALPHA = (2 * DEPTH) ** 0.25
LN_EPS = 1e-5
IN_SIZES = (SSM_WIDTH, ATTN_WIDTH, KV_WIDTH, KV_WIDTH, IDX_WIDTH, IDX_DIM, IDX_HEADS, D_MODEL, D_MODEL)
IN_OFFSETS = [0] + [int(o) for o in np.cumsum(IN_SIZES)]

LANES = 128
SUBLANES = 8
VMEM_LIMIT_BYTES = 56 * 1024 * 1024

COND_TILE = 1536
TOKEN_TILE = 512
S5_CHUNK = LANES
Q_TILE = 256
K_TILE = 256
TILE_GROUP = 8
ATTN_TILES = 2
FF_CHUNK = 256
INT_MIN = -(2 ** 31)
NEG_BIG = -1e30
HALF16 = 2 ** 15
PACK16 = 2 * SUBLANES
Q_SCALE = HEAD_DIM ** -0.5 * math.log2(math.e)


def _cparams(*sem):
    return pltpu.CompilerParams(dimension_semantics=sem, vmem_limit_bytes=VMEM_LIMIT_BYTES)


def _cond_kernel(c_ref, w_ref, b_ref, o_ref):
    c = c_ref[...]
    s = c * jax.nn.sigmoid(c)
    o_ref[0] = jnp.dot(s, w_ref[0], preferred_element_type=F32, precision=HIGHEST) + b_ref[0]


def _cond(c, w_cond, b_cond):
    depth, d, n = w_cond.shape
    bsz = c.shape[0]
    tn = COND_TILE
    assert n % tn == 0
    return pl.pallas_call(
        _cond_kernel,
        out_shape=jax.ShapeDtypeStruct((depth, bsz, n), F32),
        grid=(depth, n // tn),
        in_specs=[
            pl.BlockSpec((bsz, d), lambda l, j: (0, 0)),
            pl.BlockSpec((1, d, tn), lambda l, j: (l, 0, j)),
            pl.BlockSpec((1, 1, tn), lambda l, j: (l, 0, j)),
        ],
        out_specs=pl.BlockSpec((1, bsz, tn), lambda l, j: (l, 0, j)),
        compiler_params=_cparams("arbitrary", "arbitrary"),
        name="cond",
    )(c, w_cond, b_cond.reshape(depth, 1, n))


NAT_SSM = 0
NAT_K = NAT_SSM + SSM_WIDTH
NAT_KI = NAT_K + KV_WIDTH
NAT_G = NAT_KI + LANES
NAT_WIDTH = NAT_G + 2 * D_MODEL
TR_Q = 0
TR_V = TR_Q + ATTN_WIDTH
TR_QI = TR_V + KV_WIDTH
TR_W = TR_QI + IDX_WIDTH
TR_ROWS = -(-(TR_W + IDX_HEADS) // PACK16) * PACK16


def _inproj_kernel(x_ref, sc_ref, sh_ref, wn_ref, wt_ref, cn_ref, sn_ref, cki_ref, ski_ref,
                   ct_ref, st_ref, ct32_ref, st32_ref,
                   ssm_ref, k_ref, ki_ref, sg_ref, qt_ref, vt_ref, qit_ref, wtt_ref):
    u = (x_ref[0] * (1.0 + sc_ref[0]) + sh_ref[0]).astype(BF16)
    zn = jnp.dot(u, wn_ref[...], preferred_element_type=F32)
    zt = lax.dot_general(wt_ref[...], u, (((1,), (1,)), ((), ())),
                         preferred_element_type=F32)

    ssm_ref[0] = zn[:, NAT_SSM:NAT_SSM + SSM_WIDTH].astype(BF16)
    sg_ref[0] = jax.nn.sigmoid(zn[:, NAT_G:NAT_G + 2 * D_MODEL]).astype(BF16)

    cn, sn = cn_ref[...], sn_ref[...]
    for h in range(N_KV_HEADS):
        xh = zn[:, NAT_K + h * HEAD_DIM:NAT_K + (h + 1) * HEAD_DIM]
        rot = pltpu.roll(xh, HEAD_DIM // 2, axis=1)
        k_ref[0, :, h * HEAD_DIM:(h + 1) * HEAD_DIM] = (xh * cn + rot * sn).astype(BF16)

    xk = zn[:, NAT_KI:NAT_KI + LANES]
    lane = lax.broadcasted_iota(I32, xk.shape, 1)
    rot = jnp.where((lane & (IDX_DIM // 2)) == 0,
                    pltpu.roll(xk, LANES - IDX_DIM // 2, axis=1),
                    pltpu.roll(xk, IDX_DIM // 2, axis=1))
    ki_ref[0] = (xk * cki_ref[...] + rot * ski_ref[...])[:, :IDX_DIM].astype(BF16)

    ct, st = ct_ref[...], st_ref[...]
    half = HEAD_DIM // 2
    for h in range(N_HEADS):
        r0 = TR_Q + h * HEAD_DIM
        x1, x2 = zt[r0:r0 + half], zt[r0 + half:r0 + HEAD_DIM]
        qt_ref[0, h * HEAD_DIM:h * HEAD_DIM + half, :] = ((x1 * ct - x2 * st) * Q_SCALE).astype(BF16)
        qt_ref[0, h * HEAD_DIM + half:(h + 1) * HEAD_DIM, :] = ((x2 * ct + x1 * st) * Q_SCALE).astype(BF16)
    ct32, st32 = ct32_ref[...], st32_ref[...]
    half = IDX_DIM // 2
    for h in range(IDX_HEADS):
        r0 = TR_QI + h * IDX_DIM
        x1, x2 = zt[r0:r0 + half], zt[r0 + half:r0 + IDX_DIM]
        qit_ref[0, h * IDX_DIM:h * IDX_DIM + half, :] = (x1 * ct32 - x2 * st32).astype(BF16)
        qit_ref[0, h * IDX_DIM + half:(h + 1) * IDX_DIM, :] = (x2 * ct32 + x1 * st32).astype(BF16)

    vt_ref[0] = zt[TR_V:TR_V + KV_WIDTH].astype(BF16)
    wtt_ref[0] = zt[TR_W:TR_W + IDX_HEADS] * IDX_SCALE


def _rope_tables(seq):
    pos = jnp.arange(seq, dtype=F32)

    def cs(half):
        inv = ROPE_THETA ** (-jnp.arange(half, dtype=F32) / half)
        ang = pos[:, None] * inv[None, :]
        return jnp.cos(ang), jnp.sin(ang)

    c64, s64 = cs(HEAD_DIM // 2)
    c32, s32 = cs(IDX_DIM // 2)
    z = jnp.zeros((seq, LANES - IDX_DIM), F32)
    return dict(
        cn=jnp.concatenate([c64, c64], -1), sn=jnp.concatenate([-s64, s64], -1),
        cki=jnp.concatenate([c32, c32, z], -1), ski=jnp.concatenate([-s32, s32, z], -1),
        ct=c64.T, st=s64.T, ct32=c32.T, st32=s32.T)


def _split_w_in(w_in):
    o = IN_OFFSETS
    ssm, q, k, v, qi, ki, wi, gs, ga = [w_in[:, o[i]:o[i + 1]] for i in range(9)]
    zpad = jnp.zeros((D_MODEL, LANES - IDX_DIM), w_in.dtype)
    wn = jnp.concatenate([ssm, k, ki, zpad, gs, ga], axis=1).astype(BF16)
    wt = jnp.concatenate([q, v, qi, wi, jnp.zeros((D_MODEL, TR_ROWS - TR_W - IDX_HEADS), w_in.dtype)],
                         axis=1).T.astype(BF16)
    return wn, wt


def _inproj(x, sc, sh, wn, wt, tabs, tm):
    bsz, seq, d = x.shape
    grid = (bsz, seq // tm)
    tok = lambda w: pl.BlockSpec((1, tm, w), lambda b, i: (b, i, 0))
    trn = lambda r: pl.BlockSpec((1, r, tm), lambda b, i: (b, 0, i))
    vec = pl.BlockSpec((1, 1, d), lambda b, i: (b, 0, 0))
    const = lambda a: pl.BlockSpec(a.shape, lambda b, i: (0, 0))
    ntab = lambda: pl.BlockSpec((tm, LANES), lambda b, i: (i, 0))
    ttab = lambda r: pl.BlockSpec((r, tm), lambda b, i: (0, i))
    out_shape = (
        jax.ShapeDtypeStruct((bsz, seq, SSM_WIDTH), BF16),
        jax.ShapeDtypeStruct((bsz, seq, KV_WIDTH), BF16),
        jax.ShapeDtypeStruct((bsz, seq, IDX_DIM), BF16),
        jax.ShapeDtypeStruct((bsz, seq, 2 * D_MODEL), BF16),
        jax.ShapeDtypeStruct((bsz, ATTN_WIDTH, seq), BF16),
        jax.ShapeDtypeStruct((bsz, KV_WIDTH, seq), BF16),
        jax.ShapeDtypeStruct((bsz, IDX_WIDTH, seq), BF16),
        jax.ShapeDtypeStruct((bsz, IDX_HEADS, seq), F32),
    )
    return pl.pallas_call(
        _inproj_kernel,
        out_shape=out_shape,
        grid=grid,
        in_specs=[tok(d), vec, vec, const(wn), const(wt), ntab(), ntab(), ntab(), ntab(),
                  ttab(HEAD_DIM // 2), ttab(HEAD_DIM // 2), ttab(IDX_DIM // 2), ttab(IDX_DIM // 2)],
        out_specs=(tok(SSM_WIDTH), tok(KV_WIDTH), tok(IDX_DIM), tok(2 * D_MODEL),
                   trn(ATTN_WIDTH), trn(KV_WIDTH), trn(IDX_WIDTH), trn(IDX_HEADS)),
        compiler_params=_cparams("arbitrary", "arbitrary"),
        name="in_proj",
    )(x, sc, sh, wn, wt, tabs["cn"], tabs["sn"], tabs["cki"], tabs["ski"],
      tabs["ct"], tabs["st"], tabs["ct32"], tabs["st32"])


def _s5_kernel(u_ref, lrr_ref, lir_ref, ldr_ref, lrc_ref, lic_ref, ldc_ref,
               brt_ref, bit_ref, cr_ref, ci_ref, crt_ref, cit_ref, dv_ref,
               y_ref, w_scr, kv_scr, *, bsz):
    t = S5_CHUNK
    p = SSM_STATE
    gsz = SSM_GROUP
    lrr, lir = lrr_ref[0], lir_ref[0]
    dtr = jnp.exp(ldr_ref[0])
    lrc, lic = lrc_ref[0], lic_ref[0]
    dtc = jnp.exp(ldc_ref[0])

    mag = jnp.exp(lrr * dtr)
    ar, ai = mag * jnp.cos(lir * dtr), mag * jnp.sin(lir * dtr)
    den = lrr * lrr + lir * lir
    nr = ar - 1.0
    fr, fi = (nr * lrr + ai * lir) / den, (ai * lrr - nr * lir) / den
    brt, bit = brt_ref[0], bit_ref[0]
    bbr, bbi = fr * brt - fi * bit, fr * bit + fi * brt

    cr, ci = cr_ref[0], ci_ref[0]
    bbr_h, bbi_h = bbr[:, :p], bbi[:, :p]
    cbr = jnp.concatenate([cr[i:i + 1] * bbr_h - ci[i:i + 1] * bbi_h for i in range(gsz)], axis=0)
    cbi = jnp.concatenate([cr[i:i + 1] * bbi_h + ci[i:i + 1] * bbr_h for i in range(gsz)], axis=0)
    lag = lax.broadcasted_iota(I32, (p, t), 1).astype(F32)
    lrc_h, lic_h, dtc_h = lrc[:p], lic[:p], dtc[:p]
    pmag = jnp.exp(lrc_h * dtc_h * lag)
    pth = lic_h * dtc_h * lag
    kv_scr[...] = (jnp.dot(cbr, pmag * jnp.cos(pth), preferred_element_type=F32, precision=HIGHEST)
                   - jnp.dot(cbi, pmag * jnp.sin(pth), preferred_element_type=F32, precision=HIGHEST))

    row = lax.broadcasted_iota(I32, (t, t), 0)
    col = lax.broadcasted_iota(I32, (t, t), 1)
    causal = col >= row

    def toeplitz_cols(i, carry):
        c0 = pl.multiple_of(i * t, t)
        for k in range(gsz):
            vec = kv_scr[pl.ds(i * gsz + k, 1), :]
            blk = pltpu.roll(jnp.broadcast_to(vec, (t, t)), 0, 1, stride=1, stride_axis=0)
            w_scr[k * t:(k + 1) * t, pl.ds(c0, t)] = jnp.where(causal, blk, 0.0).astype(BF16)
        return carry

    lax.fori_loop(0, gsz, toeplitz_cols, 0)

    lo = lax.broadcasted_iota(I32, (t, 2 * p), 1) < p
    rem = (t - 1 - lax.broadcasted_iota(I32, (t, 2 * p), 0)).astype(F32)
    emag = jnp.exp(lrr * dtr * rem)
    eth = lir * dtr * rem
    ecs, esn = emag * jnp.cos(eth), emag * jnp.sin(eth)
    e_same, e_swap = jnp.where(lo, ecs, esn), jnp.where(lo, esn, ecs)
    sgn = jnp.where(lo[:1], -1.0, 1.0)
    wb = jnp.concatenate([e_same * bbr[k:k + 1] + e_swap * (bbi[k:k + 1] * sgn) for k in range(gsz)],
                         axis=0).astype(BF16)

    top = lax.broadcasted_iota(I32, (2 * p, t), 0) < p
    step = (lax.broadcasted_iota(I32, (2 * p, t), 1) + 1).astype(F32)
    cmag = jnp.exp(lrc * dtc * step)
    cth = lic * dtc * step
    mc, ms = cmag * jnp.cos(cth), cmag * jnp.sin(cth)
    crt, cit = crt_ref[0], cit_ref[0]
    wc = jnp.concatenate(
        [jnp.where(top, crt[:, i:i + 1] * mc - cit[:, i:i + 1] * ms,
                   -(crt[:, i:i + 1] * ms + cit[:, i:i + 1] * mc)) for i in range(gsz)],
        axis=1).astype(BF16)

    u = u_ref[0]
    y = jnp.dot(u, w_scr[...], preferred_element_type=F32)
    g = jnp.dot(u, wb, preferred_element_type=F32)
    gr, gi = g[:, :p], g[:, p:]

    m_t = jnp.exp(lrr[:, :p] * dtr[:, :p] * float(t))
    at_r = m_t * jnp.cos(lir[:, :p] * dtr[:, :p] * float(t))
    at_i = m_t * jnp.sin(lir[:, :p] * dtr[:, :p] * float(t))
    nc = u.shape[0] // bsz
    hr = jnp.zeros((bsz, p), F32)
    hi = jnp.zeros((bsz, p), F32)
    hrs, his = [hr], [hi]
    for c in range(1, nc):
        sl = slice((c - 1) * bsz, c * bsz)
        hr, hi = at_r * hr - at_i * hi + gr[sl], at_r * hi + at_i * hr + gi[sl]
        hrs.append(hr)
        his.append(hi)
    hcat = jnp.concatenate([jnp.concatenate(hrs, axis=0), jnp.concatenate(his, axis=0)], axis=1)
    y = y + jnp.dot(hcat.astype(BF16), wc, preferred_element_type=F32)
    y_ref[0] = (y + dv_ref[0] * u.astype(F32)).astype(BF16)


def _s5(u_ssm, lam_re, lam_im, log_dt, b_re, b_im, c_re, c_im, d_skip):
    bsz, seq, _ = u_ssm.shape
    t, g, gs, p = S5_CHUNK, SSM_GROUPS, SSM_GROUP, SSM_STATE
    nc = seq // t
    rows = nc * bsz
    ug = u_ssm.reshape(bsz, nc, t, g, gs).transpose(3, 1, 0, 4, 2).reshape(g, rows, gs * t)

    dup_r = lambda a: jnp.concatenate([a, a], axis=-1)[:, None, :]
    dup_c = lambda a: jnp.concatenate([a, a], axis=-1)[:, :, None]
    ld = jnp.broadcast_to(log_dt[:, None], (g, p))
    brt = jnp.swapaxes(b_re, 1, 2)
    bit = jnp.swapaxes(b_im, 1, 2)
    crt = jnp.swapaxes(c_re, 1, 2)
    cit = jnp.swapaxes(c_im, 1, 2)
    args = (
        ug,
        dup_r(lam_re), dup_r(lam_im), dup_r(ld), dup_c(lam_re), dup_c(lam_im), dup_c(ld),
        jnp.concatenate([brt, brt], -1), jnp.concatenate([bit, bit], -1),
        c_re, c_im, jnp.concatenate([crt, crt], 1), jnp.concatenate([cit, cit], 1),
        jnp.repeat(d_skip.reshape(g, gs), t, axis=1)[:, None, :],
    )
    per_g = lambda a: pl.BlockSpec((1,) + a.shape[1:], lambda i: (i,) + (0,) * (a.ndim - 1))
    yg = pl.pallas_call(
        functools.partial(_s5_kernel, bsz=bsz),
        out_shape=jax.ShapeDtypeStruct((g, rows, gs * t), BF16),
        grid=(g,),
        in_specs=[per_g(a) for a in args],
        out_specs=pl.BlockSpec((1, rows, gs * t), lambda i: (i, 0, 0)),
        scratch_shapes=[pltpu.VMEM((gs * t, gs * t), BF16), pltpu.VMEM((gs * gs, t), F32)],
        compiler_params=_cparams("arbitrary"),
        name="s5",
    )(*args)
    return yg.reshape(g, nc, bsz, gs, t).transpose(2, 1, 4, 0, 3).reshape(bsz, seq, g * gs)


def _dsa_kernel(qt_ref, qit_ref, wt_ref, k_ref, vt_ref, ki_ref, y_ref,
                key_scr, hi_scr, lo_scr, m_scr, acc_scr, sacc_scr, s_scr, lg_scr, p_scr, *, n_sel):
    tq, tk = Q_TILE, K_TILE
    qi = pl.program_id(1)
    nkt = qi + 1
    q0 = qi * tq
    diff = (lax.broadcasted_iota(I32, (tk, tq), 0) - lax.broadcasted_iota(I32, (tk, tq), 1))

    npair = (nkt + 1) // 2
    rows = lambda kt: pl.ds(pl.multiple_of(kt * tk, tk), tk)

    def score_tile(kt, causal_mask):
        k0 = kt * tk
        ki_t = ki_ref[0, rows(kt), :]
        s = jnp.zeros((tk, tq), F32)
        for h in range(IDX_HEADS):
            rel = jnp.dot(ki_t, qit_ref[0, h * IDX_DIM:(h + 1) * IDX_DIM, :], preferred_element_type=F32)
            s = s + jnp.maximum(rel, 0.0) * wt_ref[0, h:h + 1, :]
        bits = lax.bitcast_convert_type(s, I32)
        sign = bits >> 31
        key = ((bits & 0x7FFFFFFF) ^ sign) - sign
        if causal_mask:
            key = jnp.where(diff <= q0 - k0, key, INT_MIN)
        key_scr[rows(kt), :] = key
        hi_scr[rows(kt), :] = (key >> 16).astype(I16)
        lo_scr[rows(kt), :] = ((key & 0xFFFF) - HALF16).astype(I16)

    def score_group(g, carry):
        for u in range(TILE_GROUP):
            score_tile(g * TILE_GROUP + u, False)
        return carry

    def score_pair(j, carry):
        score_tile(2 * j, False)
        score_tile(2 * j + 1, False)
        return carry

    full_groups = (npair - 1) // (TILE_GROUP // 2)
    lax.fori_loop(0, full_groups, score_group, 0)
    lax.fori_loop(full_groups * (TILE_GROUP // 2), npair - 1, score_pair, 0)
    score_tile(2 * npair - 2, True)
    score_tile(2 * npair - 1, True)

    def tile_loop(fn, carry):
        def group(g, c):
            for u in range(TILE_GROUP):
                c = fn(g * TILE_GROUP + u, c)
            return c

        def pair(j, c):
            return fn(2 * j + 1, fn(2 * j, c))

        pairs_per_group = TILE_GROUP // 2
        ngrp = npair // pairs_per_group
        carry = lax.fori_loop(0, ngrp, group, carry)
        return lax.fori_loop(ngrp * pairs_per_group, npair, pair, carry)

    def tree_sum(parts):
        while len(parts) > 1:
            parts = [a + b for a, b in zip(parts[::2], parts[1::2])]
        return parts[0]

    def count16(ref, pred, with_tile=False):
        def tile(kt, acc):
            vals = ref[rows(kt), :]
            hit = jnp.where(pred(vals, kt) if with_tile else pred(vals), jnp.int16(1), jnp.int16(0))
            return acc + tree_sum([hit[r:r + PACK16] for r in range(0, tk, PACK16)])
        part = tile_loop(tile, jnp.zeros((PACK16, tq), I16))
        return jnp.sum(part.astype(I32), axis=0, keepdims=True)

    def search16(ref, target):
        def bit_step(i, state):
            c_u, c_cnt = state
            trial = c_u | (jnp.int32(1) << (15 - i))
            trial_s = (trial - HALF16).astype(I16)
            cnt = count16(ref, lambda v: v >= trial_s)
            ok = cnt >= target
            return jnp.where(ok, trial, c_u), jnp.where(ok, cnt, c_cnt)
        zero = jnp.zeros((1, tq), I32)
        return lax.fori_loop(0, 16, bit_step, (zero, zero))

    hi_u, n_hi_ge = search16(hi_scr, n_sel)
    hi_thr = hi_u - HALF16
    hi_thr16 = hi_thr.astype(I16)

    def keep_bucket(hi, kt):
        lo_scr[rows(kt), :] = jnp.where(hi == hi_thr16, lo_scr[rows(kt), :], jnp.int16(-HALF16))
        return hi > hi_thr16

    cnt_above = count16(hi_scr, keep_bucket, with_tile=True)
    lo_thr, n_lo_ge = search16(lo_scr, n_sel - cnt_above)
    thr = jnp.maximum(hi_thr * (2 * HALF16) + lo_thr, INT_MIN + 1)
    in_bucket = jnp.where(lo_thr != 0, n_lo_ge, n_hi_ge - cnt_above)
    cnt_ge = jnp.where(hi_u != 0, cnt_above + in_bucket, 0)

    neg_bits = np.float32(NEG_BIG).view(np.int32)

    def to_bias(kt, carry):
        keys = key_scr[rows(kt), :]
        key_scr[rows(kt), :] = jnp.where(keys > thr, 0, jnp.where(keys == thr, INT_MIN, neg_bits))
        return carry

    tile_loop(to_bias, 0)

    @pl.when(jnp.max(cnt_ge) > n_sel)
    def _():
        lo_thr16 = (lo_thr - HALF16).astype(I16)
        cnt_gt = cnt_above + count16(lo_scr, lambda v: v > lo_thr16)
        need = n_sel - cnt_gt
        sub = lax.broadcasted_iota(I32, (tk, tq), 0)
        no_tie = jnp.int16(HALF16 - 1)

        def tie_index(kt, state):
            seen, cut_tile = state
            idx = (sub + kt * tk).astype(I16)
            at_lo = jnp.where(lo_scr[rows(kt), :] == lo_thr16, idx, no_tie)
            tie = jnp.where(hi_scr[rows(kt), :] == hi_thr16, at_lo, no_tie)
            lo_scr[rows(kt), :] = tie
            hit = jnp.where(tie != no_tie, jnp.int16(1), jnp.int16(0))
            here = tree_sum([hit[r:r + PACK16] for r in range(0, tk, PACK16)])
            after = seen + jnp.sum(here.astype(I32), axis=0, keepdims=True)
            return after, jnp.where((seen <= need) & (after > need), kt, cut_tile)

        zero = jnp.zeros((1, tq), I32)
        _, cut_tile = tile_loop(tie_index, (zero, zero + 2 * npair))

        tile_bits = tk.bit_length() - 1

        def idx_step(i, j_cut):
            trial = j_cut | (jnp.int32(1) << (tile_bits - 1 - i))
            trial16 = trial.astype(I16)
            cnt = count16(lo_scr, lambda v: v < trial16)
            return jnp.where(cnt <= need, trial, j_cut)

        j_cut = lax.fori_loop(0, tile_bits, idx_step, cut_tile << tile_bits)

        def drop(kt, carry):
            bits = key_scr[rows(kt), :]
            key_scr[rows(kt), :] = jnp.where((bits == INT_MIN) & (sub + kt * tk >= j_cut), neg_bits, bits)
            return carry

        tile_loop(drop, 0)

    ta = ATTN_TILES * tk
    nstep = (nkt + ATTN_TILES - 1) // ATTN_TILES
    m_scr[...] = jnp.full(m_scr.shape, NEG_BIG, F32)
    sacc_scr[...] = jnp.full(sacc_scr.shape, NEG_BIG, F32)
    acc_scr[...] = jnp.zeros(acc_scr.shape, F32)
    ones_rows = jnp.ones((PACK16, ta), BF16)
    span = lambda i: pl.ds(pl.multiple_of(i * ta, ta), ta)

    def logits(i):
        for h in range(N_HEADS):
            g = h // KV_GROUP
            lg = (jnp.dot(k_ref[0, span(i), g * HEAD_DIM:(g + 1) * HEAD_DIM],
                          qt_ref[0, h * HEAD_DIM:(h + 1) * HEAD_DIM, :], preferred_element_type=F32)
                  + lax.bitcast_convert_type(key_scr[span(i), :], F32))
            lg_scr[h] = lg
            m_scr[h] = jnp.maximum(m_scr[h], jnp.max(lg, axis=0, keepdims=True))

    def probs():
        for h in range(N_HEADS):
            s = m_scr[h]
            s_scr[h] = s
            p_scr[h] = jnp.exp2(lg_scr[h] - s).astype(BF16)

    def accumulate(i):
        for h in range(N_HEADS):
            g = h // KV_GROUP
            vt_t = jnp.concatenate([vt_ref[0, g * HEAD_DIM:(g + 1) * HEAD_DIM, span(i)], ones_rows], axis=0)
            s_new = s_scr[h]
            alpha = jnp.exp2(sacc_scr[h] - s_new)
            acc_scr[h] = alpha * acc_scr[h] + jnp.dot(vt_t, p_scr[h], preferred_element_type=F32)
            sacc_scr[h] = s_new

    logits(0)

    @pl.when(nstep == 1)
    def _():
        probs()
        accumulate(0)

    @pl.when(nstep > 1)
    def _():
        probs()
        logits(1)

        def step(i, carry):
            accumulate(i - 1)
            probs()
            logits(i + 1)
            return carry

        lax.fori_loop(1, nstep - 1, step, 0)
        accumulate(nstep - 2)
        probs()
        accumulate(nstep - 1)

    for h in range(N_HEADS):
        o = acc_scr[h, :HEAD_DIM] / acc_scr[h, HEAD_DIM:HEAD_DIM + 1]
        y_ref[0, :, h * HEAD_DIM:(h + 1) * HEAD_DIM] = o.T.astype(BF16)


def _dsa(qt, qit, wtt, k, vt, ki):
    bsz, seq, _ = k.shape
    n_sel = min(TOPK_MAX, seq // 4)
    tq = Q_TILE
    assert Q_TILE == K_TILE and seq % (2 * K_TILE) == 0 and seq < HALF16
    kern = functools.partial(_dsa_kernel, n_sel=n_sel)
    return pl.pallas_call(
        kern,
        out_shape=jax.ShapeDtypeStruct((bsz, seq, ATTN_WIDTH), BF16),
        grid=(bsz, seq // tq),
        in_specs=[
            pl.BlockSpec((1, ATTN_WIDTH, tq), lambda b, i: (b, 0, i)),
            pl.BlockSpec((1, IDX_WIDTH, tq), lambda b, i: (b, 0, i)),
            pl.BlockSpec((1, IDX_HEADS, tq), lambda b, i: (b, 0, i)),
            pl.BlockSpec((1, seq, KV_WIDTH), lambda b, i: (b, 0, 0)),
            pl.BlockSpec((1, KV_WIDTH, seq), lambda b, i: (b, 0, 0)),
            pl.BlockSpec((1, seq, IDX_DIM), lambda b, i: (b, 0, 0)),
        ],
        out_specs=pl.BlockSpec((1, tq, ATTN_WIDTH), lambda b, i: (b, i, 0)),
        scratch_shapes=[
            pltpu.VMEM((seq, tq), I32),
            pltpu.VMEM((seq, tq), I16),
            pltpu.VMEM((seq, tq), I16),
            pltpu.VMEM((N_HEADS, 1, tq), F32),
            pltpu.VMEM((N_HEADS, HEAD_DIM + PACK16, tq), F32),
            pltpu.VMEM((N_HEADS, 1, tq), F32),
            pltpu.VMEM((N_HEADS, 1, tq), F32),
            pltpu.VMEM((N_HEADS, ATTN_TILES * K_TILE, tq), F32),
            pltpu.VMEM((N_HEADS, ATTN_TILES * K_TILE, tq), BF16),
        ],
        compiler_params=_cparams("arbitrary", "arbitrary"),
        name="dsa",
    )(qt, qit, wtt, k, vt, ki)


def _deepnorm_ln(x, h, gate, g, b):
    y = DEEPNORM_ALPHA * x + (1.0 + gate) * h
    mu = jnp.mean(y, axis=-1, keepdims=True)
    yc = y - mu
    var = jnp.mean(yc * yc, axis=-1, keepdims=True)
    return yc * lax.rsqrt(var + LN_EPS) * g + b


def _merge_kernel(x_ref, ys_ref, ya_ref, sg_ref, gt_ref, wglu_ref, bglu_ref, pssm_ref, pattn_ref,
                  wout_ref, lng_ref, lnb_ref, o_ref):
    y = jax.nn.gelu(ys_ref[0].astype(F32))
    glu = jnp.dot(y.astype(BF16), wglu_ref[...], preferred_element_type=F32) + bglu_ref[...]
    y = (y * jax.nn.sigmoid(glu)).astype(BF16)
    sg = sg_ref[0]
    merged = (sg[:, :D_MODEL].astype(F32) * jnp.dot(y, pssm_ref[...], preferred_element_type=F32)
              + sg[:, D_MODEL:].astype(F32) * jnp.dot(ya_ref[0], pattn_ref[...], preferred_element_type=F32))
    h = jnp.dot(merged.astype(BF16), wout_ref[...], preferred_element_type=F32)
    o_ref[0] = _deepnorm_ln(x_ref[0], h, gt_ref[0], lng_ref[...], lnb_ref[...])


def _merge(x, ys, ya, sg, gt, wglu, bglu, pssm, pattn, wout, lng, lnb, tm):
    bsz, seq, d = x.shape
    tok = lambda w: pl.BlockSpec((1, tm, w), lambda b, i: (b, i, 0))
    vec = pl.BlockSpec((1, 1, d), lambda b, i: (b, 0, 0))
    const = lambda a: pl.BlockSpec(a.shape, lambda b, i: (0, 0))
    return pl.pallas_call(
        _merge_kernel,
        out_shape=jax.ShapeDtypeStruct(x.shape, F32),
        grid=(bsz, seq // tm),
        in_specs=[tok(d), tok(SSM_WIDTH), tok(ATTN_WIDTH), tok(2 * D_MODEL), vec,
                  const(wglu), const(bglu), const(pssm), const(pattn), const(wout), const(lng), const(lnb)],
        out_specs=tok(d),
        compiler_params=_cparams("arbitrary", "arbitrary"),
        name="merge",
    )(x, ys, ya, sg, gt, wglu, bglu, pssm, pattn, wout, lng, lnb)


def _ffn_kernel(x_ref, sc_ref, sh_ref, gt_ref, wgu_ref, wd_ref, lng_ref, lnb_ref, o_ref, hid_scr):
    x = x_ref[0]
    u = (x * (1.0 + sc_ref[0]) + sh_ref[0]).astype(BF16)
    for j in range(D_FF // FF_CHUNK):
        a = jnp.dot(u, wgu_ref[:, j * FF_CHUNK:(j + 1) * FF_CHUNK], preferred_element_type=F32)
        b = jnp.dot(u, wgu_ref[:, D_FF + j * FF_CHUNK:D_FF + (j + 1) * FF_CHUNK], preferred_element_type=F32)
        hid_scr[:, j * FF_CHUNK:(j + 1) * FF_CHUNK] = (a * jax.nn.sigmoid(a) * b).astype(BF16)
    f = jnp.dot(hid_scr[...], wd_ref[...], preferred_element_type=F32)
    o_ref[0] = _deepnorm_ln(x, f, gt_ref[0], lng_ref[...], lnb_ref[...])


def _ffn(x, sc, sh, gt, wgu, wd, lng, lnb, tm):
    bsz, seq, d = x.shape
    tok = pl.BlockSpec((1, tm, d), lambda b, i: (b, i, 0))
    vec = pl.BlockSpec((1, 1, d), lambda b, i: (b, 0, 0))
    const = lambda a: pl.BlockSpec(a.shape, lambda b, i: (0, 0))
    return pl.pallas_call(
        _ffn_kernel,
        out_shape=jax.ShapeDtypeStruct(x.shape, F32),
        grid=(bsz, seq // tm),
        in_specs=[tok, vec, vec, vec, const(wgu), const(wd), const(lng), const(lnb)],
        out_specs=tok,
        scratch_shapes=[pltpu.VMEM((tm, D_FF), BF16)],
        compiler_params=_cparams("arbitrary", "arbitrary"),
        name="ffn",
    )(x, sc, sh, gt, wgu, wd, lng, lnb)


def kernel(x, c, w_cond, b_cond, w_in, ssm_lam_re, ssm_lam_im, ssm_log_dt, ssm_b_re, ssm_b_im,
           ssm_c_re, ssm_c_im, ssm_d, ssm_w_glu, ssm_b_glu, p_ssm, p_attn, w_out,
           ln1_g, ln1_b, w_gate_up, w_down, ln2_g, ln2_b):
    bsz, seq, d = x.shape
    assert d == D_MODEL and seq % Q_TILE == 0 and seq % S5_CHUNK == 0
    tm = min(TOKEN_TILE, seq)
    assert seq % tm == 0
    tabs = _rope_tables(seq)
    mod = _cond(c, w_cond, b_cond)
    row = lambda a: a.reshape(1, -1)
    for l in range(DEPTH):
        sh1, sc1, gt1, sh2, sc2, gt2 = [mod[l, :, i * d:(i + 1) * d][:, None, :] for i in range(6)]
        wn, wt = _split_w_in(w_in[l])
        u_ssm, k, ki, sg, qt, vt, qit, wtt = _inproj(x, sc1, sh1, wn, wt, tabs, tm)
        ys = _s5(u_ssm, ssm_lam_re[l], ssm_lam_im[l], ssm_log_dt[l], ssm_b_re[l], ssm_b_im[l],
                 ssm_c_re[l], ssm_c_im[l], ssm_d[l])
        ya = _dsa(qt, qit, wtt, k, vt, ki)
        x = _merge(x, ys, ya, sg, gt1, ssm_w_glu[l].astype(BF16), row(ssm_b_glu[l]),
                   p_ssm[l].astype(BF16), p_attn[l].astype(BF16), w_out[l].astype(BF16),
                   row(ln1_g[l]), row(ln1_b[l]), tm)
        x = _ffn(x, sc2, sh2, gt2, w_gate_up[l].astype(BF16), w_down[l].astype(BF16),
                 row(ln2_g[l]), row(ln2_b[l]), tm)
    return x
```

```python
import functools
import math

import jax
import jax.numpy as jnp
import numpy as np
from jax import lax
from jax.experimental import pallas as pl
from jax.experimental.pallas import tpu as pltpu

F32 = jnp.float32
BF16 = jnp.bfloat16
I32 = jnp.int32
I16 = jnp.int16
HIGHEST = lax.Precision.HIGHEST

D_MODEL = 1024
DEPTH = 2
SSM_WIDTH = 512
SSM_GROUP = 16
SSM_GROUPS = SSM_WIDTH // SSM_GROUP
SSM_STATE = 64
HEAD_DIM = 128
N_HEADS = D_MODEL // HEAD_DIM
N_KV_HEADS = 2
KV_GROUP = N_HEADS // N_KV_HEADS
ATTN_WIDTH = N_HEADS * HEAD_DIM
KV_WIDTH = N_KV_HEADS * HEAD_DIM
IDX_HEADS = 8
IDX_DIM = 64
IDX_WIDTH = IDX_HEADS * IDX_DIM
IDX_SCALE = (IDX_HEADS * IDX_DIM) ** -0.5
TOPK_MAX = 256
ROPE_THETA = 10000.0
D_FF = -(-8 * D_MODEL // (3 * 256)) * 256
DEEPNORM_ALPHA = (2 * DEPTH) ** 0.25
LN_EPS = 1e-5
IN_SIZES = (SSM_WIDTH, ATTN_WIDTH, KV_WIDTH, KV_WIDTH, IDX_WIDTH, IDX_DIM, IDX_HEADS, D_MODEL, D_MODEL)
IN_OFFSETS = [0] + [int(o) for o in np.cumsum(IN_SIZES)]

LANES = 128
SUBLANES = 8
VMEM_LIMIT_BYTES = 56 * 1024 * 1024

COND_TILE = 1536
TOKEN_TILE = 512
S5_CHUNK = LANES
Q_TILE = 256
K_TILE = 256
TILE_GROUP = 8
ATTN_TILES = 2
FF_CHUNK = 256
INT_MIN = -(2 ** 31)
NEG_BIG = -1e30
HALF16 = 2 ** 15
PACK16 = 2 * SUBLANES
Q_SCALE = HEAD_DIM ** -0.5 * math.log2(math.e)


def _cparams(*sem):
    return pltpu.CompilerParams(dimension_semantics=sem, vmem_limit_bytes=VMEM_LIMIT_BYTES)


def _cond_kernel(c_ref, w_ref, b_ref, o_ref):
    c = c_ref[...]
    s = c * jax.nn.sigmoid(c)
    o_ref[0] = jnp.dot(s, w_ref[0], preferred_element_type=F32, precision=HIGHEST) + b_ref[0]


def _cond(c, w_cond, b_cond):
    depth, d, n = w_cond.shape
    bsz = c.shape[0]
    tn = COND_TILE
    assert n % tn == 0
    return pl.pallas_call(
        _cond_kernel,
        out_shape=jax.ShapeDtypeStruct((depth, bsz, n), F32),
        grid=(depth, n // tn),
        in_specs=[
            pl.BlockSpec((bsz, d), lambda l, j: (0, 0)),
            pl.BlockSpec((1, d, tn), lambda l, j: (l, 0, j)),
            pl.BlockSpec((1, 1, tn), lambda l, j: (l, 0, j)),
        ],
        out_specs=pl.BlockSpec((1, bsz, tn), lambda l, j: (l, 0, j)),
        compiler_params=_cparams("arbitrary", "arbitrary"),
        name="cond",
    )(c, w_cond, b_cond.reshape(depth, 1, n))


NAT_SSM = 0
NAT_K = NAT_SSM + SSM_WIDTH
NAT_KI = NAT_K + KV_WIDTH
NAT_G = NAT_KI + LANES
NAT_WIDTH = NAT_G + 2 * D_MODEL
TR_Q = 0
TR_V = TR_Q + ATTN_WIDTH
TR_QI = TR_V + KV_WIDTH
TR_W = TR_QI + IDX_WIDTH
TR_ROWS = -(-(TR_W + IDX_HEADS) // PACK16) * PACK16


def _inproj_kernel(x_ref, sc_ref, sh_ref, wn_ref, wt_ref, cn_ref, sn_ref, cki_ref, ski_ref,
                   ct_ref, st_ref, ct32_ref, st32_ref,
                   ssm_ref, k_ref, ki_ref, sg_ref, qt_ref, vt_ref, qit_ref, wtt_ref):
    u = (x_ref[0] * (1.0 + sc_ref[0]) + sh_ref[0]).astype(BF16)
    zn = jnp.dot(u, wn_ref[...], preferred_element_type=F32)
    zt = lax.dot_general(wt_ref[...], u, (((1,), (1,)), ((), ())),
                         preferred_element_type=F32)

    ssm_ref[0] = zn[:, NAT_SSM:NAT_SSM + SSM_WIDTH].astype(BF16)
    sg_ref[0] = jax.nn.sigmoid(zn[:, NAT_G:NAT_G + 2 * D_MODEL]).astype(BF16)

    cn, sn = cn_ref[...], sn_ref[...]
    for h in range(N_KV_HEADS):
        xh = zn[:, NAT_K + h * HEAD_DIM:NAT_K + (h + 1) * HEAD_DIM]
        rot = pltpu.roll(xh, HEAD_DIM // 2, axis=1)
        k_ref[0, :, h * HEAD_DIM:(h + 1) * HEAD_DIM] = (xh * cn + rot * sn).astype(BF16)

    xk = zn[:, NAT_KI:NAT_KI + LANES]
    lane = lax.broadcasted_iota(I32, xk.shape, 1)
    rot = jnp.where((lane & (IDX_DIM // 2)) == 0,
                    pltpu.roll(xk, LANES - IDX_DIM // 2, axis=1),
                    pltpu.roll(xk, IDX_DIM // 2, axis=1))
    ki_ref[0] = (xk * cki_ref[...] + rot * ski_ref[...])[:, :IDX_DIM].astype(BF16)

    ct, st = ct_ref[...], st_ref[...]
    half = HEAD_DIM // 2
    for h in range(N_HEADS):
        r0 = TR_Q + h * HEAD_DIM
        x1, x2 = zt[r0:r0 + half], zt[r0 + half:r0 + HEAD_DIM]
        qt_ref[0, h * HEAD_DIM:h * HEAD_DIM + half, :] = ((x1 * ct - x2 * st) * Q_SCALE).astype(BF16)
        qt_ref[0, h * HEAD_DIM + half:(h + 1) * HEAD_DIM, :] = ((x2 * ct + x1 * st) * Q_SCALE).astype(BF16)
    ct32, st32 = ct32_ref[...], st32_ref[...]
    half = IDX_DIM // 2
    for h in range(IDX_HEADS):
        r0 = TR_QI + h * IDX_DIM
        x1, x2 = zt[r0:r0 + half], zt[r0 + half:r0 + IDX_DIM]
        qit_ref[0, h * IDX_DIM:h * IDX_DIM + half, :] = (x1 * ct32 - x2 * st32).astype(BF16)
        qit_ref[0, h * IDX_DIM + half:(h + 1) * IDX_DIM, :] = (x2 * ct32 + x1 * st32).astype(BF16)

    vt_ref[0] = zt[TR_V:TR_V + KV_WIDTH].astype(BF16)
    wtt_ref[0] = zt[TR_W:TR_W + IDX_HEADS] * IDX_SCALE


def _rope_tables(seq):
    pos = jnp.arange(seq, dtype=F32)

    def cs(half):
        inv = ROPE_THETA ** (-jnp.arange(half, dtype=F32) / half)
        ang = pos[:, None] * inv[None, :]
        return jnp.cos(ang), jnp.sin(ang)

    c64, s64 = cs(HEAD_DIM // 2)
    c32, s32 = cs(IDX_DIM // 2)
    z = jnp.zeros((seq, LANES - IDX_DIM), F32)
    return dict(
        cn=jnp.concatenate([c64, c64], -1), sn=jnp.concatenate([-s64, s64], -1),
        cki=jnp.concatenate([c32, c32, z], -1), ski=jnp.concatenate([-s32, s32, z], -1),
        ct=c64.T, st=s64.T, ct32=c32.T, st32=s32.T)


def _split_w_in(w_in):
    o = IN_OFFSETS
    ssm, q, k, v, qi, ki, wi, gs, ga = [w_in[:, o[i]:o[i + 1]] for i in range(9)]
    zpad = jnp.zeros((D_MODEL, LANES - IDX_DIM), w_in.dtype)
    wn = jnp.concatenate([ssm, k, ki, zpad, gs, ga], axis=1).astype(BF16)
    wt = jnp.concatenate([q, v, qi, wi, jnp.zeros((D_MODEL, TR_ROWS - TR_W - IDX_HEADS), w_in.dtype)],
                         axis=1).T.astype(BF16)
    return wn, wt


def _inproj(x, sc, sh, wn, wt, tabs, tm):
    bsz, seq, d = x.shape
    grid = (bsz, seq // tm)
    tok = lambda w: pl.BlockSpec((1, tm, w), lambda b, i: (b, i, 0))
    trn = lambda r: pl.BlockSpec((1, r, tm), lambda b, i: (b, 0, i))
    vec = pl.BlockSpec((1, 1, d), lambda b, i: (b, 0, 0))
    const = lambda a: pl.BlockSpec(a.shape, lambda b, i: (0, 0))
    ntab = lambda: pl.BlockSpec((tm, LANES), lambda b, i: (i, 0))
    ttab = lambda r: pl.BlockSpec((r, tm), lambda b, i: (0, i))
    out_shape = (
        jax.ShapeDtypeStruct((bsz, seq, SSM_WIDTH), BF16),
        jax.ShapeDtypeStruct((bsz, seq, KV_WIDTH), BF16),
        jax.ShapeDtypeStruct((bsz, seq, IDX_DIM), BF16),
        jax.ShapeDtypeStruct((bsz, seq, 2 * D_MODEL), BF16),
        jax.ShapeDtypeStruct((bsz, ATTN_WIDTH, seq), BF16),
        jax.ShapeDtypeStruct((bsz, KV_WIDTH, seq), BF16),
        jax.ShapeDtypeStruct((bsz, IDX_WIDTH, seq), BF16),
        jax.ShapeDtypeStruct((bsz, IDX_HEADS, seq), F32),
    )
    return pl.pallas_call(
        _inproj_kernel,
        out_shape=out_shape,
        grid=grid,
        in_specs=[tok(d), vec, vec, const(wn), const(wt), ntab(), ntab(), ntab(), ntab(),
                  ttab(HEAD_DIM // 2), ttab(HEAD_DIM // 2), ttab(IDX_DIM // 2), ttab(IDX_DIM // 2)],
        out_specs=(tok(SSM_WIDTH), tok(KV_WIDTH), tok(IDX_DIM), tok(2 * D_MODEL),
                   trn(ATTN_WIDTH), trn(KV_WIDTH), trn(IDX_WIDTH), trn(IDX_HEADS)),
        compiler_params=_cparams("arbitrary", "arbitrary"),
        name="in_proj",
    )(x, sc, sh, wn, wt, tabs["cn"], tabs["sn"], tabs["cki"], tabs["ski"],
      tabs["ct"], tabs["st"], tabs["ct32"], tabs["st32"])


def _s5_kernel(u_ref, lrr_ref, lir_ref, ldr_ref, lrc_ref, lic_ref, ldc_ref,
               brt_ref, bit_ref, cr_ref, ci_ref, crt_ref, cit_ref, dv_ref,
               y_ref, w_scr, kv_scr, *, bsz):
    t = S5_CHUNK
    p = SSM_STATE
    gsz = SSM_GROUP
    lrr, lir = lrr_ref[0], lir_ref[0]
    dtr = jnp.exp(ldr_ref[0])
    lrc, lic = lrc_ref[0], lic_ref[0]
    dtc = jnp.exp(ldc_ref[0])

    mag = jnp.exp(lrr * dtr)
    ar, ai = mag * jnp.cos(lir * dtr), mag * jnp.sin(lir * dtr)
    den = lrr * lrr + lir * lir
    nr = ar - 1.0
    fr, fi = (nr * lrr + ai * lir) / den, (ai * lrr - nr * lir) / den
    brt, bit = brt_ref[0], bit_ref[0]
    bbr, bbi = fr * brt - fi * bit, fr * bit + fi * brt

    cr, ci = cr_ref[0], ci_ref[0]
    bbr_h, bbi_h = bbr[:, :p], bbi[:, :p]
    cbr = jnp.concatenate([cr[i:i + 1] * bbr_h - ci[i:i + 1] * bbi_h for i in range(gsz)], axis=0)
    cbi = jnp.concatenate([cr[i:i + 1] * bbi_h + ci[i:i + 1] * bbr_h for i in range(gsz)], axis=0)
    lag = lax.broadcasted_iota(I32, (p, t), 1).astype(F32)
    lrc_h, lic_h, dtc_h = lrc[:p], lic[:p], dtc[:p]
    pmag = jnp.exp(lrc_h * dtc_h * lag)
    pth = lic_h * dtc_h * lag
    kv_scr[...] = (jnp.dot(cbr, pmag * jnp.cos(pth), preferred_element_type=F32, precision=HIGHEST)
                   - jnp.dot(cbi, pmag * jnp.sin(pth), preferred_element_type=F32, precision=HIGHEST))

    row = lax.broadcasted_iota(I32, (t, t), 0)
    col = lax.broadcasted_iota(I32, (t, t), 1)
    causal = col >= row

    def toeplitz_cols(i, carry):
        c0 = pl.multiple_of(i * t, t)
        for k in range(gsz):
            vec = kv_scr[pl.ds(i * gsz + k, 1), :]
            blk = pltpu.roll(jnp.broadcast_to(vec, (t, t)), 0, 1, stride=1, stride_axis=0)
            w_scr[k * t:(k + 1) * t, pl.ds(c0, t)] = jnp.where(causal, blk, 0.0).astype(BF16)
        return carry

    lax.fori_loop(0, gsz, toeplitz_cols, 0)

    lo = lax.broadcasted_iota(I32, (t, 2 * p), 1) < p
    rem = (t - 1 - lax.broadcasted_iota(I32, (t, 2 * p), 0)).astype(F32)
    emag = jnp.exp(lrr * dtr * rem)
    eth = lir * dtr * rem
    ecs, esn = emag * jnp.cos(eth), emag * jnp.sin(eth)
    e_same, e_swap = jnp.where(lo, ecs, esn), jnp.where(lo, esn, ecs)
    sgn = jnp.where(lo[:1], -1.0, 1.0)
    wb = jnp.concatenate([e_same * bbr[k:k + 1] + e_swap * (bbi[k:k + 1] * sgn) for k in range(gsz)],
                         axis=0).astype(BF16)

    top = lax.broadcasted_iota(I32, (2 * p, t), 0) < p
    step = (lax.broadcasted_iota(I32, (2 * p, t), 1) + 1).astype(F32)
    cmag = jnp.exp(lrc * dtc * step)
    cth = lic * dtc * step
    mc, ms = cmag * jnp.cos(cth), cmag * jnp.sin(cth)
    crt, cit = crt_ref[0], cit_ref[0]
    wc = jnp.concatenate(
        [jnp.where(top, crt[:, i:i + 1] * mc - cit[:, i:i + 1] * ms,
                   -(crt[:, i:i + 1] * ms + cit[:, i:i + 1] * mc)) for i in range(gsz)],
        axis=1).astype(BF16)

    u = u_ref[0]
    y = jnp.dot(u, w_scr[...], preferred_element_type=F32)
    g = jnp.dot(u, wb, preferred_element_type=F32)
    gr, gi = g[:, :p], g[:, p:]

    m_t = jnp.exp(lrr[:, :p] * dtr[:, :p] * float(t))
    at_r = m_t * jnp.cos(lir[:, :p] * dtr[:, :p] * float(t))
    at_i = m_t * jnp.sin(lir[:, :p] * dtr[:, :p] * float(t))
    nc = u.shape[0] // bsz
    hr = jnp.zeros((bsz, p), F32)
    hi = jnp.zeros((bsz, p), F32)
    hrs, his = [hr], [hi]
    for c in range(1, nc):
        sl = slice((c - 1) * bsz, c * bsz)
        hr, hi = at_r * hr - at_i * hi + gr[sl], at_r * hi + at_i * hr + gi[sl]
        hrs.append(hr)
        his.append(hi)
    hcat = jnp.concatenate([jnp.concatenate(hrs, axis=0), jnp.concatenate(his, axis=0)], axis=1)
    y = y + jnp.dot(hcat.astype(BF16), wc, preferred_element_type=F32)
    y_ref[0] = (y + dv_ref[0] * u.astype(F32)).astype(BF16)


def _s5(u_ssm, lam_re, lam_im, log_dt, b_re, b_im, c_re, c_im, d_skip):
    bsz, seq, _ = u_ssm.shape
    t, g, gs, p = S5_CHUNK, SSM_GROUPS, SSM_GROUP, SSM_STATE
    nc = seq // t
    rows = nc * bsz
    ug = u_ssm.reshape(bsz, nc, t, g, gs).transpose(3, 1, 0, 4, 2).reshape(g, rows, gs * t)

    dup_r = lambda a: jnp.concatenate([a, a], axis=-1)[:, None, :]
    dup_c = lambda a: jnp.concatenate([a, a], axis=-1)[:, :, None]
    ld = jnp.broadcast_to(log_dt[:, None], (g, p))
    brt = jnp.swapaxes(b_re, 1, 2)
    bit = jnp.swapaxes(b_im, 1, 2)
    crt = jnp.swapaxes(c_re, 1, 2)
    cit = jnp.swapaxes(c_im, 1, 2)
    args = (
        ug,
        dup_r(lam_re), dup_r(lam_im), dup_r(ld), dup_c(lam_re), dup_c(lam_im), dup_c(ld),
        jnp.concatenate([brt, brt], -1), jnp.concatenate([bit, bit], -1),
        c_re, c_im, jnp.concatenate([crt, crt], 1), jnp.concatenate([cit, cit], 1),
        jnp.repeat(d_skip.reshape(g, gs), t, axis=1)[:, None, :],
    )
    per_g = lambda a: pl.BlockSpec((1,) + a.shape[1:], lambda i: (i,) + (0,) * (a.ndim - 1))
    yg = pl.pallas_call(
        functools.partial(_s5_kernel, bsz=bsz),
        out_shape=jax.ShapeDtypeStruct((g, rows, gs * t), BF16),
        grid=(g,),
        in_specs=[per_g(a) for a in args],
        out_specs=pl.BlockSpec((1, rows, gs * t), lambda i: (i, 0, 0)),
        scratch_shapes=[pltpu.VMEM((gs * t, gs * t), BF16), pltpu.VMEM((gs * gs, t), F32)],
        compiler_params=_cparams("arbitrary"),
        name="s5",
    )(*args)
    return yg.reshape(g, nc, bsz, gs, t).transpose(2, 1, 4, 0, 3).reshape(bsz, seq, g * gs)


def _dsa_kernel(qt_ref, qit_ref, wt_ref, k_ref, vt_ref, ki_ref, y_ref,
                key_scr, hi_scr, lo_scr, m_scr, acc_scr, sacc_scr, s_scr, lg_scr, p_scr, *, n_sel):
    tq, tk = Q_TILE, K_TILE
    qi = pl.program_id(1)
    nkt = qi + 1
    q0 = qi * tq
    diff = (lax.broadcasted_iota(I32, (tk, tq), 0) - lax.broadcasted_iota(I32, (tk, tq), 1))

    npair = (nkt + 1) // 2
    rows = lambda kt: pl.ds(pl.multiple_of(kt * tk, tk), tk)

    def score_tile(kt, causal_mask):
        k0 = kt * tk
        ki_t = ki_ref[0, rows(kt), :]
        s = jnp.zeros((tk, tq), F32)
        for h in range(IDX_HEADS):
            rel = jnp.dot(ki_t, qit_ref[0, h * IDX_DIM:(h + 1) * IDX_DIM, :], preferred_element_type=F32)
            s = s + jnp.maximum(rel, 0.0) * wt_ref[0, h:h + 1, :]
        bits = lax.bitcast_convert_type(s, I32)
        sign = bits >> 31
        key = ((bits & 0x7FFFFFFF) ^ sign) - sign
        if causal_mask:
            key = jnp.where(diff <= q0 - k0, key, INT_MIN)
        key_scr[rows(kt), :] = key
        hi_scr[rows(kt), :] = (key >> 16).astype(I16)
        lo_scr[rows(kt), :] = ((key & 0xFFFF) - HALF16).astype(I16)

    def score_group(g, carry):
        for u in range(TILE_GROUP):
            score_tile(g * TILE_GROUP + u, False)
        return carry

    def score_pair(j, carry):
        score_tile(2 * j, False)
        score_tile(2 * j + 1, False)
        return carry

    full_groups = (npair - 1) // (TILE_GROUP // 2)
    lax.fori_loop(0, full_groups, score_group, 0)
    lax.fori_loop(full_groups * (TILE_GROUP // 2), npair - 1, score_pair, 0)
    score_tile(2 * npair - 2, True)
    score_tile(2 * npair - 1, True)

    def tile_loop(fn, carry):
        def group(g, c):
            for u in range(TILE_GROUP):
                c = fn(g * TILE_GROUP + u, c)
            return c

        def pair(j, c):
            return fn(2 * j + 1, fn(2 * j, c))

        pairs_per_group = TILE_GROUP // 2
        ngrp = npair // pairs_per_group
        carry = lax.fori_loop(0, ngrp, group, carry)
        return lax.fori_loop(ngrp * pairs_per_group, npair, pair, carry)

    def tree_sum(parts):
        while len(parts) > 1:
            parts = [a + b for a, b in zip(parts[::2], parts[1::2])]
        return parts[0]

    def count16(ref, pred, with_tile=False):
        def tile(kt, acc):
            vals = ref[rows(kt), :]
            hit = jnp.where(pred(vals, kt) if with_tile else pred(vals), jnp.int16(1), jnp.int16(0))
            return acc + tree_sum([hit[r:r + PACK16] for r in range(0, tk, PACK16)])
        part = tile_loop(tile, jnp.zeros((PACK16, tq), I16))
        return jnp.sum(part.astype(I32), axis=0, keepdims=True)

    def search16(ref, target):
        def bit_step(i, state):
            c_u, c_cnt = state
            trial = c_u | (jnp.int32(1) << (15 - i))
            trial_s = (trial - HALF16).astype(I16)
            cnt = count16(ref, lambda v: v >= trial_s)
            ok = cnt >= target
            return jnp.where(ok, trial, c_u), jnp.where(ok, cnt, c_cnt)
        zero = jnp.zeros((1, tq), I32)
        return lax.fori_loop(0, 16, bit_step, (zero, zero))

    hi_u, n_hi_ge = search16(hi_scr, n_sel)
    hi_thr = hi_u - HALF16
    hi_thr16 = hi_thr.astype(I16)

    def keep_bucket(hi, kt):
        lo_scr[rows(kt), :] = jnp.where(hi == hi_thr16, lo_scr[rows(kt), :], jnp.int16(-HALF16))
        return hi > hi_thr16

    cnt_above = count16(hi_scr, keep_bucket, with_tile=True)
    lo_thr, n_lo_ge = search16(lo_scr, n_sel - cnt_above)
    thr = jnp.maximum(hi_thr * (2 * HALF16) + lo_thr, INT_MIN + 1)
    in_bucket = jnp.where(lo_thr != 0, n_lo_ge, n_hi_ge - cnt_above)
    cnt_ge = jnp.where(hi_u != 0, cnt_above + in_bucket, 0)

    neg_bits = np.float32(NEG_BIG).view(np.int32)

    def to_bias(kt, carry):
        keys = key_scr[rows(kt), :]
        key_scr[rows(kt), :] = jnp.where(keys > thr, 0, jnp.where(keys == thr, INT_MIN, neg_bits))
        return carry

    tile_loop(to_bias, 0)

    @pl.when(jnp.max(cnt_ge) > n_sel)
    def _():
        lo_thr16 = (lo_thr - HALF16).astype(I16)
        cnt_gt = cnt_above + count16(lo_scr, lambda v: v > lo_thr16)
        need = n_sel - cnt_gt
        sub = lax.broadcasted_iota(I32, (tk, tq), 0)
        no_tie = jnp.int16(HALF16 - 1)

        def tie_index(kt, state):
            seen, cut_tile = state
            idx = (sub + kt * tk).astype(I16)
            at_lo = jnp.where(lo_scr[rows(kt), :] == lo_thr16, idx, no_tie)
            tie = jnp.where(hi_scr[rows(kt), :] == hi_thr16, at_lo, no_tie)
            lo_scr[rows(kt), :] = tie
            hit = jnp.where(tie != no_tie, jnp.int16(1), jnp.int16(0))
            here = tree_sum([hit[r:r + PACK16] for r in range(0, tk, PACK16)])
            after = seen + jnp.sum(here.astype(I32), axis=0, keepdims=True)
            return after, jnp.where((seen <= need) & (after > need), kt, cut_tile)

        zero = jnp.zeros((1, tq), I32)
        _, cut_tile = tile_loop(tie_index, (zero, zero + 2 * npair))

        tile_bits = tk.bit_length() - 1

        def idx_step(i, j_cut):
            trial = j_cut | (jnp.int32(1) << (tile_bits - 1 - i))
            trial16 = trial.astype(I16)
            cnt = count16(lo_scr, lambda v: v < trial16)
            return jnp.where(cnt <= need, trial, j_cut)

        j_cut = lax.fori_loop(0, tile_bits, idx_step, cut_tile << tile_bits)

        def drop(kt, carry):
            bits = key_scr[rows(kt), :]
            key_scr[rows(kt), :] = jnp.where((bits == INT_MIN) & (sub + kt * tk >= j_cut), neg_bits, bits)
            return carry

        tile_loop(drop, 0)

    ta = ATTN_TILES * tk
    nstep = (nkt + ATTN_TILES - 1) // ATTN_TILES
    m_scr[...] = jnp.full(m_scr.shape, NEG_BIG, F32)
    sacc_scr[...] = jnp.full(sacc_scr.shape, NEG_BIG, F32)
    acc_scr[...] = jnp.zeros(acc_scr.shape, F32)
    ones_rows = jnp.ones((PACK16, ta), BF16)
    span = lambda i: pl.ds(pl.multiple_of(i * ta, ta), ta)

    def logits(i):
        for h in range(N_HEADS):
            g = h // KV_GROUP
            lg = (jnp.dot(k_ref[0, span(i), g * HEAD_DIM:(g + 1) * HEAD_DIM],
                          qt_ref[0, h * HEAD_DIM:(h + 1) * HEAD_DIM, :], preferred_element_type=F32)
                  + lax.bitcast_convert_type(key_scr[span(i), :], F32))
            lg_scr[h] = lg
            m_scr[h] = jnp.maximum(m_scr[h], jnp.max(lg, axis=0, keepdims=True))

    def probs():
        for h in range(N_HEADS):
            s = m_scr[h]
            s_scr[h] = s
            p_scr[h] = jnp.exp2(lg_scr[h] - s).astype(BF16)

    def accumulate(i):
        for h in range(N_HEADS):
            g = h // KV_GROUP
            vt_t = jnp.concatenate([vt_ref[0, g * HEAD_DIM:(g + 1) * HEAD_DIM, span(i)], ones_rows], axis=0)
            s_new = s_scr[h]
            alpha = jnp.exp2(sacc_scr[h] - s_new)
            acc_scr[h] = alpha * acc_scr[h] + jnp.dot(vt_t, p_scr[h], preferred_element_type=F32)
            sacc_scr[h] = s_new

    logits(0)

    @pl.when(nstep == 1)
    def _():
        probs()
        accumulate(0)

    @pl.when(nstep > 1)
    def _():
        probs()
        logits(1)

        def step(i, carry):
            accumulate(i - 1)
            probs()
            logits(i + 1)
            return carry

        lax.fori_loop(1, nstep - 1, step, 0)
        accumulate(nstep - 2)
        probs()
        accumulate(nstep - 1)

    for h in range(N_HEADS):
        o = acc_scr[h, :HEAD_DIM] / acc_scr[h, HEAD_DIM:HEAD_DIM + 1]
        y_ref[0, :, h * HEAD_DIM:(h + 1) * HEAD_DIM] = o.T.astype(BF16)


def _dsa(qt, qit, wtt, k, vt, ki):
    bsz, seq, _ = k.shape
    n_sel = min(TOPK_MAX, seq // 4)
    tq = Q_TILE
    assert Q_TILE == K_TILE and seq % (2 * K_TILE) == 0 and seq < HALF16
    kern = functools.partial(_dsa_kernel, n_sel=n_sel)
    return pl.pallas_call(
        kern,
        out_shape=jax.ShapeDtypeStruct((bsz, seq, ATTN_WIDTH), BF16),
        grid=(bsz, seq // tq),
        in_specs=[
            pl.BlockSpec((1, ATTN_WIDTH, tq), lambda b, i: (b, 0, i)),
            pl.BlockSpec((1, IDX_WIDTH, tq), lambda b, i: (b, 0, i)),
            pl.BlockSpec((1, IDX_HEADS, tq), lambda b, i: (b, 0, i)),
            pl.BlockSpec((1, seq, KV_WIDTH), lambda b, i: (b, 0, 0)),
            pl.BlockSpec((1, KV_WIDTH, seq), lambda b, i: (b, 0, 0)),
            pl.BlockSpec((1, seq, IDX_DIM), lambda b, i: (b, 0, 0)),
        ],
        out_specs=pl.BlockSpec((1, tq, ATTN_WIDTH), lambda b, i: (b, i, 0)),
        scratch_shapes=[
            pltpu.VMEM((seq, tq), I32),
            pltpu.VMEM((seq, tq), I16),
            pltpu.VMEM((seq, tq), I16),
            pltpu.VMEM((N_HEADS, 1, tq), F32),
            pltpu.VMEM((N_HEADS, HEAD_DIM + PACK16, tq), F32),
            pltpu.VMEM((N_HEADS, 1, tq), F32),
            pltpu.VMEM((N_HEADS, 1, tq), F32),
            pltpu.VMEM((N_HEADS, ATTN_TILES * K_TILE, tq), F32),
            pltpu.VMEM((N_HEADS, ATTN_TILES * K_TILE, tq), BF16),
        ],
        compiler_params=_cparams("arbitrary", "arbitrary"),
        name="dsa",
    )(qt, qit, wtt, k, vt, ki)


def _deepnorm_ln(x, h, gate, g, b):
    y = DEEPNORM_ALPHA * x + (1.0 + gate) * h
    mu = jnp.mean(y, axis=-1, keepdims=True)
    yc = y - mu
    var = jnp.mean(yc * yc, axis=-1, keepdims=True)
    return yc * lax.rsqrt(var + LN_EPS) * g + b


def _tail_kernel(x_ref, ys_ref, ya_ref, sg_ref, gt1_ref, sc2_ref, sh2_ref, gt2_ref,
                 wglu_ref, bglu_ref, pssm_ref, pattn_ref, wout_ref, ln1g_ref, ln1b_ref,
                 wgu_ref, wd_ref, ln2g_ref, ln2b_ref, o_ref, hid_scr):
    y = jax.nn.gelu(ys_ref[0].astype(F32))
    glu = jnp.dot(y.astype(BF16), wglu_ref[...], preferred_element_type=F32) + bglu_ref[...]
    y = (y * jax.nn.sigmoid(glu)).astype(BF16)
    sg = sg_ref[0]
    merged = (sg[:, :D_MODEL].astype(F32) * jnp.dot(y, pssm_ref[...], preferred_element_type=F32)
              + sg[:, D_MODEL:].astype(F32) * jnp.dot(ya_ref[0], pattn_ref[...], preferred_element_type=F32))
    h = jnp.dot(merged.astype(BF16), wout_ref[...], preferred_element_type=F32)
    x1 = _deepnorm_ln(x_ref[0], h, gt1_ref[0], ln1g_ref[...], ln1b_ref[...])
    u = (x1 * (1.0 + sc2_ref[0]) + sh2_ref[0]).astype(BF16)
    for j in range(D_FF // FF_CHUNK):
        a = jnp.dot(u, wgu_ref[:, j * FF_CHUNK:(j + 1) * FF_CHUNK], preferred_element_type=F32)
        b = jnp.dot(u, wgu_ref[:, D_FF + j * FF_CHUNK:D_FF + (j + 1) * FF_CHUNK], preferred_element_type=F32)
        hid_scr[:, j * FF_CHUNK:(j + 1) * FF_CHUNK] = (a * jax.nn.sigmoid(a) * b).astype(BF16)
    f = jnp.dot(hid_scr[...], wd_ref[...], preferred_element_type=F32)
    o_ref[0] = _deepnorm_ln(x1, f, gt2_ref[0], ln2g_ref[...], ln2b_ref[...])


def _tail(x, ys, ya, sg, gt1, sc2, sh2, gt2, wglu, bglu, pssm, pattn, wout, ln1g, ln1b, wgu, wd, ln2g, ln2b, tm):
    bsz, seq, d = x.shape
    tok = lambda w: pl.BlockSpec((1, tm, w), lambda b, i: (b, i, 0))
    vec = pl.BlockSpec((1, 1, d), lambda b, i: (b, 0, 0))
    const = lambda a: pl.BlockSpec(a.shape, lambda b, i: (0, 0), pipeline_mode=pl.Buffered(1))
    weights = (wglu, bglu, pssm, pattn, wout, ln1g, ln1b, wgu, wd, ln2g, ln2b)
    return pl.pallas_call(
        _tail_kernel,
        out_shape=jax.ShapeDtypeStruct(x.shape, F32),
        grid=(bsz, seq // tm),
        in_specs=[tok(d), tok(SSM_WIDTH), tok(ATTN_WIDTH), tok(2 * D_MODEL), vec, vec, vec, vec]
                 + [const(w) for w in weights],
        out_specs=tok(d),
        scratch_shapes=[pltpu.VMEM((tm, D_FF), BF16)],
        compiler_params=_cparams("arbitrary", "arbitrary"),
        name="tail",
    )(x, ys, ya, sg, gt1, sc2, sh2, gt2, *weights)


def kernel(x, c, w_cond, b_cond, w_in, ssm_lam_re, ssm_lam_im, ssm_log_dt, ssm_b_re, ssm_b_im,
           ssm_c_re, ssm_c_im, ssm_d, ssm_w_glu, ssm_b_glu, p_ssm, p_attn, w_out,
           ln1_g, ln1_b, w_gate_up, w_down, ln2_g, ln2_b):
    bsz, seq, d = x.shape
    assert d == D_MODEL and seq % Q_TILE == 0 and seq % S5_CHUNK == 0
    tm = min(TOKEN_TILE, seq)
    assert seq % tm == 0
    tabs = _rope_tables(seq)
    mod = _cond(c, w_cond, b_cond)
    row = lambda a: a.reshape(1, -1)
    for l in range(DEPTH):
        sh1, sc1, gt1, sh2, sc2, gt2 = [mod[l, :, i * d:(i + 1) * d][:, None, :] for i in range(6)]
        wn, wt = _split_w_in(w_in[l])
        u_ssm, k, ki, sg, qt, vt, qit, wtt = _inproj(x, sc1, sh1, wn, wt, tabs, tm)
        ys = _s5(u_ssm, ssm_lam_re[l], ssm_lam_im[l], ssm_log_dt[l], ssm_b_re[l], ssm_b_im[l],
                 ssm_c_re[l], ssm_c_im[l], ssm_d[l])
        ya = _dsa(qt, qit, wtt, k, vt, ki)
        x = _tail(x, ys, ya, sg, gt1, sc2, sh2, gt2, ssm_w_glu[l].astype(BF16), row(ssm_b_glu[l]),
                  p_ssm[l].astype(BF16), p_attn[l].astype(BF16), w_out[l].astype(BF16),
                  row(ln1_g[l]), row(ln1_b[l]), w_gate_up[l].astype(BF16), w_down[l].astype(BF16),
                  row(ln2_g[l]), row(ln2_b[l]), tm)
    return x
```

```python
import functools
import math

import jax
import jax.numpy as jnp
import numpy as np
from jax import lax
from jax.experimental import pallas as pl
from jax.experimental.pallas import tpu as pltpu

F32 = jnp.float32
BF16 = jnp.bfloat16
I32 = jnp.int32
I16 = jnp.int16
HIGHEST = lax.Precision.HIGHEST

D_MODEL = 1024
DEPTH = 2
SSM_WIDTH = 512
SSM_GROUP = 16
SSM_GROUPS = SSM_WIDTH // SSM_GROUP
SSM_STATE = 64
HEAD_DIM = 128
N_HEADS = D_MODEL // HEAD_DIM
N_KV_HEADS = 2
KV_GROUP = N_HEADS // N_KV_HEADS
ATTN_WIDTH = N_HEADS * HEAD_DIM
KV_WIDTH = N_KV_HEADS * HEAD_DIM
IDX_HEADS = 8
IDX_DIM = 64
IDX_WIDTH = IDX_HEADS * IDX_DIM
IDX_SCALE = (IDX_HEADS * IDX_DIM) ** -0.5
TOPK_MAX = 256
ROPE_THETA = 10000.0
D_FF = -(-8 * D_MODEL // (3 * 256)) * 256
DEEPNORM_ALPHA = (2 * DEPTH) ** 0.25
LN_EPS = 1e-5
IN_SIZES = (SSM_WIDTH, ATTN_WIDTH, KV_WIDTH, KV_WIDTH, IDX_WIDTH, IDX_DIM, IDX_HEADS, D_MODEL, D_MODEL)
IN_OFFSETS = [0] + [int(o) for o in np.cumsum(IN_SIZES)]

LANES = 128
SUBLANES = 8
VMEM_LIMIT_BYTES = 56 * 1024 * 1024

COND_TILE = 1536
TOKEN_TILE = 512
S5_CHUNK = LANES
Q_TILE = 256
K_TILE = 256
TILE_GROUP = 8
ATTN_TILES = 2
FF_CHUNK = 256
INT_MIN = -(2 ** 31)
NEG_BIG = -1e30
HALF16 = 2 ** 15
PACK16 = 2 * SUBLANES
Q_SCALE = HEAD_DIM ** -0.5 * math.log2(math.e)


def _cparams(*sem):
    return pltpu.CompilerParams(dimension_semantics=sem, vmem_limit_bytes=VMEM_LIMIT_BYTES)


def _cond_kernel(c_ref, w_ref, b_ref, o_ref):
    c = c_ref[...]
    s = c * jax.nn.sigmoid(c)
    o_ref[0] = jnp.dot(s, w_ref[0], preferred_element_type=F32, precision=HIGHEST) + b_ref[0]


def _cond(c, w_cond, b_cond):
    depth, d, n = w_cond.shape
    bsz = c.shape[0]
    tn = COND_TILE
    assert n % tn == 0
    return pl.pallas_call(
        _cond_kernel,
        out_shape=jax.ShapeDtypeStruct((depth, bsz, n), F32),
        grid=(depth, n // tn),
        in_specs=[
            pl.BlockSpec((bsz, d), lambda l, j: (0, 0)),
            pl.BlockSpec((1, d, tn), lambda l, j: (l, 0, j)),
            pl.BlockSpec((1, 1, tn), lambda l, j: (l, 0, j)),
        ],
        out_specs=pl.BlockSpec((1, bsz, tn), lambda l, j: (l, 0, j)),
        compiler_params=_cparams("arbitrary", "arbitrary"),
        name="cond",
    )(c, w_cond, b_cond.reshape(depth, 1, n))


NAT_SSM = 0
NAT_K = NAT_SSM + SSM_WIDTH
NAT_KI = NAT_K + KV_WIDTH
NAT_G = NAT_KI + LANES
NAT_WIDTH = NAT_G + 2 * D_MODEL
TR_Q = 0
TR_V = TR_Q + ATTN_WIDTH
TR_QI = TR_V + KV_WIDTH
TR_W = TR_QI + IDX_WIDTH
TR_ROWS = -(-(TR_W + IDX_HEADS) // PACK16) * PACK16


def _inproj_kernel(x_ref, sc_ref, sh_ref, wn_ref, wt_ref, cn_ref, sn_ref, cki_ref, ski_ref,
                   ct_ref, st_ref, ct32_ref, st32_ref,
                   ssm_ref, k_ref, ki_ref, sg_ref, qt_ref, vt_ref, qit_ref, wtt_ref):
    u = (x_ref[0] * (1.0 + sc_ref[0]) + sh_ref[0]).astype(BF16)
    zn = jnp.dot(u, wn_ref[...], preferred_element_type=F32)
    zt = lax.dot_general(wt_ref[...], u, (((1,), (1,)), ((), ())),
                         preferred_element_type=F32)

    ssm_ref[0] = zn[:, NAT_SSM:NAT_SSM + SSM_WIDTH].astype(BF16)
    sg_ref[0] = jax.nn.sigmoid(zn[:, NAT_G:NAT_G + 2 * D_MODEL]).astype(BF16)

    cn, sn = cn_ref[...], sn_ref[...]
    for h in range(N_KV_HEADS):
        xh = zn[:, NAT_K + h * HEAD_DIM:NAT_K + (h + 1) * HEAD_DIM]
        rot = pltpu.roll(xh, HEAD_DIM // 2, axis=1)
        k_ref[0, :, h * HEAD_DIM:(h + 1) * HEAD_DIM] = (xh * cn + rot * sn).astype(BF16)

    xk = zn[:, NAT_KI:NAT_KI + LANES]
    lane = lax.broadcasted_iota(I32, xk.shape, 1)
    rot = jnp.where((lane & (IDX_DIM // 2)) == 0,
                    pltpu.roll(xk, LANES - IDX_DIM // 2, axis=1),
                    pltpu.roll(xk, IDX_DIM // 2, axis=1))
    ki_ref[0] = (xk * cki_ref[...] + rot * ski_ref[...])[:, :IDX_DIM].astype(BF16)

    ct, st = ct_ref[...], st_ref[...]
    half = HEAD_DIM // 2
    for h in range(N_HEADS):
        r0 = TR_Q + h * HEAD_DIM
        x1, x2 = zt[r0:r0 + half], zt[r0 + half:r0 + HEAD_DIM]
        qt_ref[0, h * HEAD_DIM:h * HEAD_DIM + half, :] = ((x1 * ct - x2 * st) * Q_SCALE).astype(BF16)
        qt_ref[0, h * HEAD_DIM + half:(h + 1) * HEAD_DIM, :] = ((x2 * ct + x1 * st) * Q_SCALE).astype(BF16)
    ct32, st32 = ct32_ref[...], st32_ref[...]
    half = IDX_DIM // 2
    for h in range(IDX_HEADS):
        r0 = TR_QI + h * IDX_DIM
        x1, x2 = zt[r0:r0 + half], zt[r0 + half:r0 + IDX_DIM]
        qit_ref[0, h * IDX_DIM:h * IDX_DIM + half, :] = (x1 * ct32 - x2 * st32).astype(BF16)
        qit_ref[0, h * IDX_DIM + half:(h + 1) * IDX_DIM, :] = (x2 * ct32 + x1 * st32).astype(BF16)

    vt_ref[0] = zt[TR_V:TR_V + KV_WIDTH].astype(BF16)
    wtt_ref[0] = zt[TR_W:TR_W + IDX_HEADS] * IDX_SCALE


def _rope_tables(seq):
    pos = jnp.arange(seq, dtype=F32)

    def cs(half):
        inv = ROPE_THETA ** (-jnp.arange(half, dtype=F32) / half)
        ang = pos[:, None] * inv[None, :]
        return jnp.cos(ang), jnp.sin(ang)

    c64, s64 = cs(HEAD_DIM // 2)
    c32, s32 = cs(IDX_DIM // 2)
    z = jnp.zeros((seq, LANES - IDX_DIM), F32)
    return dict(
        cn=jnp.concatenate([c64, c64], -1), sn=jnp.concatenate([-s64, s64], -1),
        cki=jnp.concatenate([c32, c32, z], -1), ski=jnp.concatenate([-s32, s32, z], -1),
        ct=c64.T, st=s64.T, ct32=c32.T, st32=s32.T)


def _split_w_in(w_in):
    o = IN_OFFSETS
    ssm, q, k, v, qi, ki, wi, gs, ga = [w_in[:, o[i]:o[i + 1]] for i in range(9)]
    zpad = jnp.zeros((D_MODEL, LANES - IDX_DIM), w_in.dtype)
    wn = jnp.concatenate([ssm, k, ki, zpad, gs, ga], axis=1).astype(BF16)
    wt = jnp.concatenate([q, v, qi, wi, jnp.zeros((D_MODEL, TR_ROWS - TR_W - IDX_HEADS), w_in.dtype)],
                         axis=1).T.astype(BF16)
    return wn, wt


def _inproj(x, sc, sh, wn, wt, tabs, tm):
    bsz, seq, d = x.shape
    grid = (bsz, seq // tm)
    tok = lambda w: pl.BlockSpec((1, tm, w), lambda b, i: (b, i, 0))
    trn = lambda r: pl.BlockSpec((1, r, tm), lambda b, i: (b, 0, i))
    vec = pl.BlockSpec((1, 1, d), lambda b, i: (b, 0, 0))
    const = lambda a: pl.BlockSpec(a.shape, lambda b, i: (0, 0))
    ntab = lambda: pl.BlockSpec((tm, LANES), lambda b, i: (i, 0))
    ttab = lambda r: pl.BlockSpec((r, tm), lambda b, i: (0, i))
    out_shape = (
        jax.ShapeDtypeStruct((bsz, seq, SSM_WIDTH), BF16),
        jax.ShapeDtypeStruct((bsz, seq, KV_WIDTH), BF16),
        jax.ShapeDtypeStruct((bsz, seq, IDX_DIM), BF16),
        jax.ShapeDtypeStruct((bsz, seq, 2 * D_MODEL), BF16),
        jax.ShapeDtypeStruct((bsz, ATTN_WIDTH, seq), BF16),
        jax.ShapeDtypeStruct((bsz, KV_WIDTH, seq), BF16),
        jax.ShapeDtypeStruct((bsz, IDX_WIDTH, seq), BF16),
        jax.ShapeDtypeStruct((bsz, IDX_HEADS, seq), F32),
    )
    return pl.pallas_call(
        _inproj_kernel,
        out_shape=out_shape,
        grid=grid,
        in_specs=[tok(d), vec, vec, const(wn), const(wt), ntab(), ntab(), ntab(), ntab(),
                  ttab(HEAD_DIM // 2), ttab(HEAD_DIM // 2), ttab(IDX_DIM // 2), ttab(IDX_DIM // 2)],
        out_specs=(tok(SSM_WIDTH), tok(KV_WIDTH), tok(IDX_DIM), tok(2 * D_MODEL),
                   trn(ATTN_WIDTH), trn(KV_WIDTH), trn(IDX_WIDTH), trn(IDX_HEADS)),
        compiler_params=_cparams("arbitrary", "arbitrary"),
        name="in_proj",
    )(x, sc, sh, wn, wt, tabs["cn"], tabs["sn"], tabs["cki"], tabs["ski"],
      tabs["ct"], tabs["st"], tabs["ct32"], tabs["st32"])


def _s5_kernel(u_ref, lrr_ref, lir_ref, ldr_ref, lrc_ref, lic_ref, ldc_ref,
               brt_ref, bit_ref, cr_ref, ci_ref, crt_ref, cit_ref, dv_ref,
               y_ref, w_scr, kv_scr, *, bsz):
    t = S5_CHUNK
    p = SSM_STATE
    gsz = SSM_GROUP
    lrr, lir = lrr_ref[0], lir_ref[0]
    dtr = jnp.exp(ldr_ref[0])
    lrc, lic = lrc_ref[0], lic_ref[0]
    dtc = jnp.exp(ldc_ref[0])

    mag = jnp.exp(lrr * dtr)
    ar, ai = mag * jnp.cos(lir * dtr), mag * jnp.sin(lir * dtr)
    den = lrr * lrr + lir * lir
    nr = ar - 1.0
    fr, fi = (nr * lrr + ai * lir) / den, (ai * lrr - nr * lir) / den
    brt, bit = brt_ref[0], bit_ref[0]
    bbr, bbi = fr * brt - fi * bit, fr * bit + fi * brt

    cr, ci = cr_ref[0], ci_ref[0]
    bbr_h, bbi_h = bbr[:, :p], bbi[:, :p]
    cbr = jnp.concatenate([cr[i:i + 1] * bbr_h - ci[i:i + 1] * bbi_h for i in range(gsz)], axis=0)
    cbi = jnp.concatenate([cr[i:i + 1] * bbi_h + ci[i:i + 1] * bbr_h for i in range(gsz)], axis=0)
    lag = lax.broadcasted_iota(I32, (p, t), 1).astype(F32)
    lrc_h, lic_h, dtc_h = lrc[:p], lic[:p], dtc[:p]
    pmag = jnp.exp(lrc_h * dtc_h * lag)
    pth = lic_h * dtc_h * lag
    kv_scr[...] = (jnp.dot(cbr, pmag * jnp.cos(pth), preferred_element_type=F32, precision=HIGHEST)
                   - jnp.dot(cbi, pmag * jnp.sin(pth), preferred_element_type=F32, precision=HIGHEST))

    row = lax.broadcasted_iota(I32, (t, t), 0)
    col = lax.broadcasted_iota(I32, (t, t), 1)
    causal = col >= row

    def toeplitz_cols(i, carry):
        c0 = pl.multiple_of(i * t, t)
        for k in range(gsz):
            vec = kv_scr[pl.ds(i * gsz + k, 1), :]
            blk = pltpu.roll(jnp.broadcast_to(vec, (t, t)), 0, 1, stride=1, stride_axis=0)
            w_scr[k * t:(k + 1) * t, pl.ds(c0, t)] = jnp.where(causal, blk, 0.0).astype(BF16)
        return carry

    lax.fori_loop(0, gsz, toeplitz_cols, 0)

    lo = lax.broadcasted_iota(I32, (t, 2 * p), 1) < p
    rem = (t - 1 - lax.broadcasted_iota(I32, (t, 2 * p), 0)).astype(F32)
    emag = jnp.exp(lrr * dtr * rem)
    eth = lir * dtr * rem
    ecs, esn = emag * jnp.cos(eth), emag * jnp.sin(eth)
    e_same, e_swap = jnp.where(lo, ecs, esn), jnp.where(lo, esn, ecs)
    sgn = jnp.where(lo[:1], -1.0, 1.0)
    wb = jnp.concatenate([e_same * bbr[k:k + 1] + e_swap * (bbi[k:k + 1] * sgn) for k in range(gsz)],
                         axis=0).astype(BF16)

    top = lax.broadcasted_iota(I32, (2 * p, t), 0) < p
    step = (lax.broadcasted_iota(I32, (2 * p, t), 1) + 1).astype(F32)
    cmag = jnp.exp(lrc * dtc * step)
    cth = lic * dtc * step
    mc, ms = cmag * jnp.cos(cth), cmag * jnp.sin(cth)
    crt, cit = crt_ref[0], cit_ref[0]
    wc = jnp.concatenate(
        [jnp.where(top, crt[:, i:i + 1] * mc - cit[:, i:i + 1] * ms,
                   -(crt[:, i:i + 1] * ms + cit[:, i:i + 1] * mc)) for i in range(gsz)],
        axis=1).astype(BF16)

    u = u_ref[0]
    y = jnp.dot(u, w_scr[...], preferred_element_type=F32)
    g = jnp.dot(u, wb, preferred_element_type=F32)
    gr, gi = g[:, :p], g[:, p:]

    m_t = jnp.exp(lrr[:, :p] * dtr[:, :p] * float(t))
    at_r = m_t * jnp.cos(lir[:, :p] * dtr[:, :p] * float(t))
    at_i = m_t * jnp.sin(lir[:, :p] * dtr[:, :p] * float(t))
    nc = u.shape[0] // bsz
    hr = jnp.zeros((bsz, p), F32)
    hi = jnp.zeros((bsz, p), F32)
    hrs, his = [hr], [hi]
    for c in range(1, nc):
        sl = slice((c - 1) * bsz, c * bsz)
        hr, hi = at_r * hr - at_i * hi + gr[sl], at_r * hi + at_i * hr + gi[sl]
        hrs.append(hr)
        his.append(hi)
    hcat = jnp.concatenate([jnp.concatenate(hrs, axis=0), jnp.concatenate(his, axis=0)], axis=1)
    y = y + jnp.dot(hcat.astype(BF16), wc, preferred_element_type=F32)
    y_ref[0] = (y + dv_ref[0] * u.astype(F32)).astype(BF16)


def _s5(u_ssm, lam_re, lam_im, log_dt, b_re, b_im, c_re, c_im, d_skip):
    bsz, seq, _ = u_ssm.shape
    t, g, gs, p = S5_CHUNK, SSM_GROUPS, SSM_GROUP, SSM_STATE
    nc = seq // t
    rows = nc * bsz
    ug = u_ssm.reshape(bsz, nc, t, g, gs).transpose(3, 1, 0, 4, 2).reshape(g, rows, gs * t)

    dup_r = lambda a: jnp.concatenate([a, a], axis=-1)[:, None, :]
    dup_c = lambda a: jnp.concatenate([a, a], axis=-1)[:, :, None]
    ld = jnp.broadcast_to(log_dt[:, None], (g, p))
    brt = jnp.swapaxes(b_re, 1, 2)
    bit = jnp.swapaxes(b_im, 1, 2)
    crt = jnp.swapaxes(c_re, 1, 2)
    cit = jnp.swapaxes(c_im, 1, 2)
    args = (
        ug,
        dup_r(lam_re), dup_r(lam_im), dup_r(ld), dup_c(lam_re), dup_c(lam_im), dup_c(ld),
        jnp.concatenate([brt, brt], -1), jnp.concatenate([bit, bit], -1),
        c_re, c_im, jnp.concatenate([crt, crt], 1), jnp.concatenate([cit, cit], 1),
        jnp.repeat(d_skip.reshape(g, gs), t, axis=1)[:, None, :],
    )
    per_g = lambda a: pl.BlockSpec((1,) + a.shape[1:], lambda i: (i,) + (0,) * (a.ndim - 1))
    yg = pl.pallas_call(
        functools.partial(_s5_kernel, bsz=bsz),
        out_shape=jax.ShapeDtypeStruct((g, rows, gs * t), BF16),
        grid=(g,),
        in_specs=[per_g(a) for a in args],
        out_specs=pl.BlockSpec((1, rows, gs * t), lambda i: (i, 0, 0)),
        scratch_shapes=[pltpu.VMEM((gs * t, gs * t), BF16), pltpu.VMEM((gs * gs, t), F32)],
        compiler_params=_cparams("arbitrary"),
        name="s5",
    )(*args)
    return yg.reshape(g, nc, bsz, gs, t).transpose(2, 1, 4, 0, 3).reshape(bsz, seq, g * gs)


def _dsa_kernel(qt_ref, qit_ref, wt_ref, k_ref, vt_ref, ki_ref, y_ref,
                key_scr, hi_scr, lo_scr, m_scr, acc_scr, sacc_scr, s_scr, lg_scr, p_scr, *, n_sel):
    tq, tk = Q_TILE, K_TILE
    qi = pl.program_id(1)
    nkt = qi + 1
    q0 = qi * tq
    diff = (lax.broadcasted_iota(I32, (tk, tq), 0) - lax.broadcasted_iota(I32, (tk, tq), 1))

    npair = (nkt + 1) // 2
    rows = lambda kt: pl.ds(pl.multiple_of(kt * tk, tk), tk)

    def score_tile(kt, causal_mask):
        k0 = kt * tk
        ki_t = ki_ref[0, rows(kt), :]
        s = jnp.zeros((tk, tq), F32)
        for h in range(IDX_HEADS):
            rel = jnp.dot(ki_t, qit_ref[0, h * IDX_DIM:(h + 1) * IDX_DIM, :], preferred_element_type=F32)
            s = s + jnp.maximum(rel, 0.0) * wt_ref[0, h:h + 1, :]
        bits = lax.bitcast_convert_type(s, I32)
        sign = bits >> 31
        key = ((bits & 0x7FFFFFFF) ^ sign) - sign
        if causal_mask:
            key = jnp.where(diff <= q0 - k0, key, INT_MIN)
        key_scr[rows(kt), :] = key
        hi_scr[rows(kt), :] = (key >> 16).astype(I16)
        lo_scr[rows(kt), :] = ((key & 0xFFFF) - HALF16).astype(I16)

    def score_group(g, carry):
        for u in range(TILE_GROUP):
            score_tile(g * TILE_GROUP + u, False)
        return carry

    def score_pair(j, carry):
        score_tile(2 * j, False)
        score_tile(2 * j + 1, False)
        return carry

    full_groups = (npair - 1) // (TILE_GROUP // 2)
    lax.fori_loop(0, full_groups, score_group, 0)
    lax.fori_loop(full_groups * (TILE_GROUP // 2), npair - 1, score_pair, 0)
    score_tile(2 * npair - 2, True)
    score_tile(2 * npair - 1, True)

    def tile_loop(fn, carry):
        def group(g, c):
            for u in range(TILE_GROUP):
                c = fn(g * TILE_GROUP + u, c)
            return c

        def pair(j, c):
            return fn(2 * j + 1, fn(2 * j, c))

        pairs_per_group = TILE_GROUP // 2
        ngrp = npair // pairs_per_group
        carry = lax.fori_loop(0, ngrp, group, carry)
        return lax.fori_loop(ngrp * pairs_per_group, npair, pair, carry)

    def tree_sum(parts):
        while len(parts) > 1:
            parts = [a + b for a, b in zip(parts[::2], parts[1::2])]
        return parts[0]

    def count16(ref, pred, with_tile=False):
        def tile(kt, acc):
            vals = ref[rows(kt), :]
            hit = jnp.where(pred(vals, kt) if with_tile else pred(vals), jnp.int16(1), jnp.int16(0))
            return acc + tree_sum([hit[r:r + PACK16] for r in range(0, tk, PACK16)])
        part = tile_loop(tile, jnp.zeros((PACK16, tq), I16))
        return jnp.sum(part.astype(I32), axis=0, keepdims=True)

    def search16(ref, target):
        def bit_step(i, state):
            c_u, c_cnt = state
            trial = c_u | (jnp.int32(1) << (15 - i))
            trial_s = (trial - HALF16).astype(I16)
            cnt = count16(ref, lambda v: v >= trial_s)
            ok = cnt >= target
            return jnp.where(ok, trial, c_u), jnp.where(ok, cnt, c_cnt)
        zero = jnp.zeros((1, tq), I32)
        return lax.fori_loop(0, 16, bit_step, (zero, zero))

    hi_u, n_hi_ge = search16(hi_scr, n_sel)
    hi_thr = hi_u - HALF16
    hi_thr16 = hi_thr.astype(I16)

    def keep_bucket(hi, kt):
        lo_scr[rows(kt), :] = jnp.where(hi == hi_thr16, lo_scr[rows(kt), :], jnp.int16(-HALF16))
        return hi > hi_thr16

    cnt_above = count16(hi_scr, keep_bucket, with_tile=True)
    lo_thr, n_lo_ge = search16(lo_scr, n_sel - cnt_above)
    thr = jnp.maximum(hi_thr * (2 * HALF16) + lo_thr, INT_MIN + 1)
    in_bucket = jnp.where(lo_thr != 0, n_lo_ge, n_hi_ge - cnt_above)
    cnt_ge = jnp.where(hi_u != 0, cnt_above + in_bucket, 0)

    neg_bits = np.float32(NEG_BIG).view(np.int32)

    def to_bias(kt, carry):
        keys = key_scr[rows(kt), :]
        key_scr[rows(kt), :] = jnp.where(keys > thr, 0, jnp.where(keys == thr, INT_MIN, neg_bits))
        return carry

    tile_loop(to_bias, 0)

    @pl.when(jnp.max(cnt_ge) > n_sel)
    def _():
        lo_thr16 = (lo_thr - HALF16).astype(I16)
        cnt_gt = cnt_above + count16(lo_scr, lambda v: v > lo_thr16)
        need = n_sel - cnt_gt
        sub = lax.broadcasted_iota(I32, (tk, tq), 0)
        no_tie = jnp.int16(HALF16 - 1)

        def tie_index(kt, state):
            seen, cut_tile = state
            idx = (sub + kt * tk).astype(I16)
            at_lo = jnp.where(lo_scr[rows(kt), :] == lo_thr16, idx, no_tie)
            tie = jnp.where(hi_scr[rows(kt), :] == hi_thr16, at_lo, no_tie)
            lo_scr[rows(kt), :] = tie
            hit = jnp.where(tie != no_tie, jnp.int16(1), jnp.int16(0))
            here = tree_sum([hit[r:r + PACK16] for r in range(0, tk, PACK16)])
            after = seen + jnp.sum(here.astype(I32), axis=0, keepdims=True)
            return after, jnp.where((seen <= need) & (after > need), kt, cut_tile)

        zero = jnp.zeros((1, tq), I32)
        _, cut_tile = tile_loop(tie_index, (zero, zero + 2 * npair))

        tile_bits = tk.bit_length() - 1

        def idx_step(i, j_cut):
            trial = j_cut | (jnp.int32(1) << (tile_bits - 1 - i))
            trial16 = trial.astype(I16)
            cnt = count16(lo_scr, lambda v: v < trial16)
            return jnp.where(cnt <= need, trial, j_cut)

        j_cut = lax.fori_loop(0, tile_bits, idx_step, cut_tile << tile_bits)

        def drop(kt, carry):
            bits = key_scr[rows(kt), :]
            key_scr[rows(kt), :] = jnp.where((bits == INT_MIN) & (sub + kt * tk >= j_cut), neg_bits, bits)
            return carry

        tile_loop(drop, 0)

    ta = ATTN_TILES * tk
    nstep = (nkt + ATTN_TILES - 1) // ATTN_TILES
    m_scr[...] = jnp.full(m_scr.shape, NEG_BIG, F32)
    sacc_scr[...] = jnp.full(sacc_scr.shape, NEG_BIG, F32)
    acc_scr[...] = jnp.zeros(acc_scr.shape, F32)
    ones_rows = jnp.ones((PACK16, ta), BF16)
    span = lambda i: pl.ds(pl.multiple_of(i * ta, ta), ta)

    def logits(i, heads=range(N_HEADS)):
        for h in heads:
            g = h // KV_GROUP
            lg = (jnp.dot(k_ref[0, span(i), g * HEAD_DIM:(g + 1) * HEAD_DIM],
                          qt_ref[0, h * HEAD_DIM:(h + 1) * HEAD_DIM, :], preferred_element_type=F32)
                  + lax.bitcast_convert_type(key_scr[span(i), :], F32))
            lg_scr[h] = lg
            m_scr[h] = jnp.maximum(m_scr[h], jnp.max(lg, axis=0, keepdims=True))

    def probs(heads=range(N_HEADS)):
        for h in heads:
            s = m_scr[h]
            s_scr[h] = s
            p_scr[h] = jnp.exp2(lg_scr[h] - s).astype(BF16)

    def accumulate(i, heads=range(N_HEADS)):
        for h in heads:
            g = h // KV_GROUP
            vt_t = jnp.concatenate([vt_ref[0, g * HEAD_DIM:(g + 1) * HEAD_DIM, span(i)], ones_rows], axis=0)
            s_new = s_scr[h]
            alpha = jnp.exp2(sacc_scr[h] - s_new)
            acc_scr[h] = alpha * acc_scr[h] + jnp.dot(vt_t, p_scr[h], preferred_element_type=F32)
            sacc_scr[h] = s_new

    logits(0)

    @pl.when(nstep == 1)
    def _():
        probs()
        accumulate(0)

    @pl.when(nstep > 1)
    def _():
        probs()
        logits(1)

        def step(i, carry):
            for g in range(N_KV_HEADS):
                hs = range(g * KV_GROUP, (g + 1) * KV_GROUP)
                accumulate(i - 1, hs)
                probs(hs)
                logits(i + 1, hs)
            return carry

        lax.fori_loop(1, nstep - 1, step, 0)
        accumulate(nstep - 2)
        probs()
        accumulate(nstep - 1)

    for h in range(N_HEADS):
        o = acc_scr[h, :HEAD_DIM] / acc_scr[h, HEAD_DIM:HEAD_DIM + 1]
        y_ref[0, :, h * HEAD_DIM:(h + 1) * HEAD_DIM] = o.T.astype(BF16)


def _dsa(qt, qit, wtt, k, vt, ki):
    bsz, seq, _ = k.shape
    n_sel = min(TOPK_MAX, seq // 4)
    tq = Q_TILE
    assert Q_TILE == K_TILE and seq % (2 * K_TILE) == 0 and seq < HALF16
    kern = functools.partial(_dsa_kernel, n_sel=n_sel)
    return pl.pallas_call(
        kern,
        out_shape=jax.ShapeDtypeStruct((bsz, seq, ATTN_WIDTH), BF16),
        grid=(bsz, seq // tq),
        in_specs=[
            pl.BlockSpec((1, ATTN_WIDTH, tq), lambda b, i: (b, 0, i)),
            pl.BlockSpec((1, IDX_WIDTH, tq), lambda b, i: (b, 0, i)),
            pl.BlockSpec((1, IDX_HEADS, tq), lambda b, i: (b, 0, i)),
            pl.BlockSpec((1, seq, KV_WIDTH), lambda b, i: (b, 0, 0)),
            pl.BlockSpec((1, KV_WIDTH, seq), lambda b, i: (b, 0, 0)),
            pl.BlockSpec((1, seq, IDX_DIM), lambda b, i: (b, 0, 0)),
        ],
        out_specs=pl.BlockSpec((1, tq, ATTN_WIDTH), lambda b, i: (b, i, 0)),
        scratch_shapes=[
            pltpu.VMEM((seq, tq), I32),
            pltpu.VMEM((seq, tq), I16),
            pltpu.VMEM((seq, tq), I16),
            pltpu.VMEM((N_HEADS, 1, tq), F32),
            pltpu.VMEM((N_HEADS, HEAD_DIM + PACK16, tq), F32),
            pltpu.VMEM((N_HEADS, 1, tq), F32),
            pltpu.VMEM((N_HEADS, 1, tq), F32),
            pltpu.VMEM((N_HEADS, ATTN_TILES * K_TILE, tq), F32),
            pltpu.VMEM((N_HEADS, ATTN_TILES * K_TILE, tq), BF16),
        ],
        compiler_params=_cparams("arbitrary", "arbitrary"),
        name="dsa",
    )(qt, qit, wtt, k, vt, ki)


def _deepnorm_ln(x, h, gate, g, b):
    y = DEEPNORM_ALPHA * x + (1.0 + gate) * h
    mu = jnp.mean(y, axis=-1, keepdims=True)
    yc = y - mu
    var = jnp.mean(yc * yc, axis=-1, keepdims=True)
    return yc * lax.rsqrt(var + LN_EPS) * g + b


def _tail_kernel(x_ref, ys_ref, ya_ref, sg_ref, gt1_ref, sc2_ref, sh2_ref, gt2_ref,
                 wglu_ref, bglu_ref, pssm_ref, pattn_ref, wout_ref, ln1g_ref, ln1b_ref,
                 wgu_ref, wd_ref, ln2g_ref, ln2b_ref, o_ref, hid_scr):
    y = jax.nn.gelu(ys_ref[0].astype(F32))
    glu = jnp.dot(y.astype(BF16), wglu_ref[...], preferred_element_type=F32) + bglu_ref[...]
    y = (y * jax.nn.sigmoid(glu)).astype(BF16)
    sg = sg_ref[0]
    merged = (sg[:, :D_MODEL].astype(F32) * jnp.dot(y, pssm_ref[...], preferred_element_type=F32)
              + sg[:, D_MODEL:].astype(F32) * jnp.dot(ya_ref[0], pattn_ref[...], preferred_element_type=F32))
    h = jnp.dot(merged.astype(BF16), wout_ref[...], preferred_element_type=F32)
    x1 = _deepnorm_ln(x_ref[0], h, gt1_ref[0], ln1g_ref[...], ln1b_ref[...])
    u = (x1 * (1.0 + sc2_ref[0]) + sh2_ref[0]).astype(BF16)
    for j in range(D_FF // FF_CHUNK):
        a = jnp.dot(u, wgu_ref[:, j * FF_CHUNK:(j + 1) * FF_CHUNK], preferred_element_type=F32)
        b = jnp.dot(u, wgu_ref[:, D_FF + j * FF_CHUNK:D_FF + (j + 1) * FF_CHUNK], preferred_element_type=F32)
        hid_scr[:, j * FF_CHUNK:(j + 1) * FF_CHUNK] = (a * jax.nn.sigmoid(a) * b).astype(BF16)
    f = jnp.dot(hid_scr[...], wd_ref[...], preferred_element_type=F32)
    o_ref[0] = _deepnorm_ln(x1, f, gt2_ref[0], ln2g_ref[...], ln2b_ref[...])


def _tail(x, ys, ya, sg, gt1, sc2, sh2, gt2, wglu, bglu, pssm, pattn, wout, ln1g, ln1b, wgu, wd, ln2g, ln2b, tm):
    bsz, seq, d = x.shape
    tok = lambda w: pl.BlockSpec((1, tm, w), lambda b, i: (b, i, 0))
    vec = pl.BlockSpec((1, 1, d), lambda b, i: (b, 0, 0))
    const = lambda a: pl.BlockSpec(a.shape, lambda b, i: (0, 0), pipeline_mode=pl.Buffered(1))
    weights = (wglu, bglu, pssm, pattn, wout, ln1g, ln1b, wgu, wd, ln2g, ln2b)
    return pl.pallas_call(
        _tail_kernel,
        out_shape=jax.ShapeDtypeStruct(x.shape, F32),
        grid=(bsz, seq // tm),
        in_specs=[tok(d), tok(SSM_WIDTH), tok(ATTN_WIDTH), tok(2 * D_MODEL), vec, vec, vec, vec]
                 + [const(w) for w in weights],
        out_specs=tok(d),
        scratch_shapes=[pltpu.VMEM((tm, D_FF), BF16)],
        compiler_params=_cparams("arbitrary", "arbitrary"),
        name="tail",
    )(x, ys, ya, sg, gt1, sc2, sh2, gt2, *weights)


def kernel(x, c, w_cond, b_cond, w_in, ssm_lam_re, ssm_lam_im, ssm_log_dt, ssm_b_re, ssm_b_im,
           ssm_c_re, ssm_c_im, ssm_d, ssm_w_glu, ssm_b_glu, p_ssm, p_attn, w_out,
           ln1_g, ln1_b, w_gate_up, w_down, ln2_g, ln2_b):
    bsz, seq, d = x.shape
    assert d == D_MODEL and seq % Q_TILE == 0 and seq % S5_CHUNK == 0
    tm = min(TOKEN_TILE, seq)
    assert seq % tm == 0
    tabs = _rope_tables(seq)
    mod = _cond(c, w_cond, b_cond)
    row = lambda a: a.reshape(1, -1)
    for l in range(DEPTH):
        sh1, sc1, gt1, sh2, sc2, gt2 = [mod[l, :, i * d:(i + 1) * d][:, None, :] for i in range(6)]
        wn, wt = _split_w_in(w_in[l])
        u_ssm, k, ki, sg, qt, vt, qit, wtt = _inproj(x, sc1, sh1, wn, wt, tabs, tm)
        ys = _s5(u_ssm, ssm_lam_re[l], ssm_lam_im[l], ssm_log_dt[l], ssm_b_re[l], ssm_b_im[l],
                 ssm_c_re[l], ssm_c_im[l], ssm_d[l])
        ya = _dsa(qt, qit, wtt, k, vt, ki)
        x = _tail(x, ys, ya, sg, gt1, sc2, sh2, gt2, ssm_w_glu[l].astype(BF16), row(ssm_b_glu[l]),
                  p_ssm[l].astype(BF16), p_attn[l].astype(BF16), w_out[l].astype(BF16),
                  row(ln1_g[l]), row(ln1_b[l]), w_gate_up[l].astype(BF16), w_down[l].astype(BF16),
                  row(ln2_g[l]), row(ln2_b[l]), tm)
    return x
```

```python
import functools
import math

import jax
import jax.numpy as jnp
import numpy as np
from jax import lax
from jax.experimental import pallas as pl
from jax.experimental.pallas import tpu as pltpu

F32 = jnp.float32
BF16 = jnp.bfloat16
I32 = jnp.int32
I16 = jnp.int16
HIGHEST = lax.Precision.HIGHEST

D_MODEL = 1024
DEPTH = 2
SSM_WIDTH = 512
SSM_GROUP = 16
SSM_GROUPS = SSM_WIDTH // SSM_GROUP
SSM_STATE = 64
HEAD_DIM = 128
N_HEADS = D_MODEL // HEAD_DIM
N_KV_HEADS = 2
KV_GROUP = N_HEADS // N_KV_HEADS
ATTN_WIDTH = N_HEADS * HEAD_DIM
KV_WIDTH = N_KV_HEADS * HEAD_DIM
IDX_HEADS = 8
IDX_DIM = 64
IDX_WIDTH = IDX_HEADS * IDX_DIM
IDX_SCALE = (IDX_HEADS * IDX_DIM) ** -0.5
TOPK_MAX = 256
ROPE_THETA = 10000.0
D_FF = -(-8 * D_MODEL // (3 * 256)) * 256
DEEPNORM_ALPHA = (2 * DEPTH) ** 0.25
LN_EPS = 1e-5
IN_SIZES = (SSM_WIDTH, ATTN_WIDTH, KV_WIDTH, KV_WIDTH, IDX_WIDTH, IDX_DIM, IDX_HEADS, D_MODEL, D_MODEL)
IN_OFFSETS = [0] + [int(o) for o in np.cumsum(IN_SIZES)]

LANES = 128
SUBLANES = 8
VMEM_LIMIT_BYTES = 56 * 1024 * 1024

COND_TILE = 1536
TOKEN_TILE = 512
S5_CHUNK = LANES
Q_TILE = 256
K_TILE = 256
TILE_GROUP = 8
ATTN_TILES = 2
FF_CHUNK = 256
INT_MIN = -(2 ** 31)
NEG_BIG = -1e30
HALF16 = 2 ** 15
PACK16 = 2 * SUBLANES
Q_SCALE = HEAD_DIM ** -0.5 * math.log2(math.e)


def _cparams(*sem):
    return pltpu.CompilerParams(dimension_semantics=sem, vmem_limit_bytes=VMEM_LIMIT_BYTES)


def _cond_kernel(c_ref, w_ref, b_ref, o_ref):
    c = c_ref[...]
    s = c * jax.nn.sigmoid(c)
    o_ref[0] = jnp.dot(s, w_ref[0], preferred_element_type=F32, precision=HIGHEST) + b_ref[0]


def _cond(c, w_cond, b_cond):
    depth, d, n = w_cond.shape
    bsz = c.shape[0]
    tn = COND_TILE
    assert n % tn == 0
    return pl.pallas_call(
        _cond_kernel,
        out_shape=jax.ShapeDtypeStruct((depth, bsz, n), F32),
        grid=(depth, n // tn),
        in_specs=[
            pl.BlockSpec((bsz, d), lambda l, j: (0, 0)),
            pl.BlockSpec((1, d, tn), lambda l, j: (l, 0, j)),
            pl.BlockSpec((1, 1, tn), lambda l, j: (l, 0, j)),
        ],
        out_specs=pl.BlockSpec((1, bsz, tn), lambda l, j: (l, 0, j)),
        compiler_params=_cparams("arbitrary", "arbitrary"),
        name="cond",
    )(c, w_cond, b_cond.reshape(depth, 1, n))


NAT_SSM = 0
NAT_K = NAT_SSM + SSM_WIDTH
NAT_KI = NAT_K + KV_WIDTH
NAT_G = NAT_KI + LANES
NAT_WIDTH = NAT_G + 2 * D_MODEL
TR_Q = 0
TR_V = TR_Q + ATTN_WIDTH
TR_QI = TR_V + KV_WIDTH
TR_W = TR_QI + IDX_WIDTH
TR_ROWS = -(-(TR_W + IDX_HEADS) // PACK16) * PACK16


def _inproj_kernel(x_ref, sc_ref, sh_ref, wn_ref, wt_ref, cn_ref, sn_ref, cki_ref, ski_ref,
                   ct_ref, st_ref, ct32_ref, st32_ref,
                   ssm_ref, k_ref, ki_ref, sg_ref, qt_ref, vt_ref, qit_ref, wtt_ref):
    u = (x_ref[0] * (1.0 + sc_ref[0]) + sh_ref[0]).astype(BF16)
    zn = jnp.dot(u, wn_ref[...], preferred_element_type=F32)
    zt = lax.dot_general(wt_ref[...], u, (((1,), (1,)), ((), ())),
                         preferred_element_type=F32)

    ssm_ref[0] = zn[:, NAT_SSM:NAT_SSM + SSM_WIDTH].astype(BF16)
    sg_ref[0] = jax.nn.sigmoid(zn[:, NAT_G:NAT_G + 2 * D_MODEL]).astype(BF16)

    cn, sn = cn_ref[...], sn_ref[...]
    for h in range(N_KV_HEADS):
        xh = zn[:, NAT_K + h * HEAD_DIM:NAT_K + (h + 1) * HEAD_DIM]
        rot = pltpu.roll(xh, HEAD_DIM // 2, axis=1)
        k_ref[0, :, h * HEAD_DIM:(h + 1) * HEAD_DIM] = (xh * cn + rot * sn).astype(BF16)

    xk = zn[:, NAT_KI:NAT_KI + LANES]
    lane = lax.broadcasted_iota(I32, xk.shape, 1)
    rot = jnp.where((lane & (IDX_DIM // 2)) == 0,
                    pltpu.roll(xk, LANES - IDX_DIM // 2, axis=1),
                    pltpu.roll(xk, IDX_DIM // 2, axis=1))
    ki_ref[0] = (xk * cki_ref[...] + rot * ski_ref[...])[:, :IDX_DIM].astype(BF16)

    ct, st = ct_ref[...], st_ref[...]
    half = HEAD_DIM // 2
    for h in range(N_HEADS):
        r0 = TR_Q + h * HEAD_DIM
        x1, x2 = zt[r0:r0 + half], zt[r0 + half:r0 + HEAD_DIM]
        qt_ref[0, h * HEAD_DIM:h * HEAD_DIM + half, :] = ((x1 * ct - x2 * st) * Q_SCALE).astype(BF16)
        qt_ref[0, h * HEAD_DIM + half:(h + 1) * HEAD_DIM, :] = ((x2 * ct + x1 * st) * Q_SCALE).astype(BF16)
    ct32, st32 = ct32_ref[...], st32_ref[...]
    half = IDX_DIM // 2
    for h in range(IDX_HEADS):
        r0 = TR_QI + h * IDX_DIM
        x1, x2 = zt[r0:r0 + half], zt[r0 + half:r0 + IDX_DIM]
        qit_ref[0, h * IDX_DIM:h * IDX_DIM + half, :] = (x1 * ct32 - x2 * st32).astype(BF16)
        qit_ref[0, h * IDX_DIM + half:(h + 1) * IDX_DIM, :] = (x2 * ct32 + x1 * st32).astype(BF16)

    vt_ref[0] = zt[TR_V:TR_V + KV_WIDTH].astype(BF16)
    wtt_ref[0] = zt[TR_W:TR_W + IDX_HEADS] * IDX_SCALE


def _rope_tables(seq):
    pos = jnp.arange(seq, dtype=F32)

    def cs(half):
        inv = ROPE_THETA ** (-jnp.arange(half, dtype=F32) / half)
        ang = pos[:, None] * inv[None, :]
        return jnp.cos(ang), jnp.sin(ang)

    c64, s64 = cs(HEAD_DIM // 2)
    c32, s32 = cs(IDX_DIM // 2)
    z = jnp.zeros((seq, LANES - IDX_DIM), F32)
    return dict(
        cn=jnp.concatenate([c64, c64], -1), sn=jnp.concatenate([-s64, s64], -1),
        cki=jnp.concatenate([c32, c32, z], -1), ski=jnp.concatenate([-s32, s32, z], -1),
        ct=c64.T, st=s64.T, ct32=c32.T, st32=s32.T)


def _split_w_in(w_in):
    o = IN_OFFSETS
    ssm, q, k, v, qi, ki, wi, gs, ga = [w_in[:, o[i]:o[i + 1]] for i in range(9)]
    zpad = jnp.zeros((D_MODEL, LANES - IDX_DIM), w_in.dtype)
    wn = jnp.concatenate([ssm, k, ki, zpad, gs, ga], axis=1).astype(BF16)
    wt = jnp.concatenate([q, v, qi, wi, jnp.zeros((D_MODEL, TR_ROWS - TR_W - IDX_HEADS), w_in.dtype)],
                         axis=1).T.astype(BF16)
    return wn, wt


def _inproj(x, sc, sh, wn, wt, tabs, tm):
    bsz, seq, d = x.shape
    grid = (bsz, seq // tm)
    tok = lambda w: pl.BlockSpec((1, tm, w), lambda b, i: (b, i, 0))
    trn = lambda r: pl.BlockSpec((1, r, tm), lambda b, i: (b, 0, i))
    vec = pl.BlockSpec((1, 1, d), lambda b, i: (b, 0, 0))
    const = lambda a: pl.BlockSpec(a.shape, lambda b, i: (0, 0))
    ntab = lambda: pl.BlockSpec((tm, LANES), lambda b, i: (i, 0))
    ttab = lambda r: pl.BlockSpec((r, tm), lambda b, i: (0, i))
    out_shape = (
        jax.ShapeDtypeStruct((bsz, seq, SSM_WIDTH), BF16),
        jax.ShapeDtypeStruct((bsz, seq, KV_WIDTH), BF16),
        jax.ShapeDtypeStruct((bsz, seq, IDX_DIM), BF16),
        jax.ShapeDtypeStruct((bsz, seq, 2 * D_MODEL), BF16),
        jax.ShapeDtypeStruct((bsz, ATTN_WIDTH, seq), BF16),
        jax.ShapeDtypeStruct((bsz, KV_WIDTH, seq), BF16),
        jax.ShapeDtypeStruct((bsz, IDX_WIDTH, seq), BF16),
        jax.ShapeDtypeStruct((bsz, IDX_HEADS, seq), F32),
    )
    return pl.pallas_call(
        _inproj_kernel,
        out_shape=out_shape,
        grid=grid,
        in_specs=[tok(d), vec, vec, const(wn), const(wt), ntab(), ntab(), ntab(), ntab(),
                  ttab(HEAD_DIM // 2), ttab(HEAD_DIM // 2), ttab(IDX_DIM // 2), ttab(IDX_DIM // 2)],
        out_specs=(tok(SSM_WIDTH), tok(KV_WIDTH), tok(IDX_DIM), tok(2 * D_MODEL),
                   trn(ATTN_WIDTH), trn(KV_WIDTH), trn(IDX_WIDTH), trn(IDX_HEADS)),
        compiler_params=_cparams("arbitrary", "arbitrary"),
        name="in_proj",
    )(x, sc, sh, wn, wt, tabs["cn"], tabs["sn"], tabs["cki"], tabs["ski"],
      tabs["ct"], tabs["st"], tabs["ct32"], tabs["st32"])


def _s5_kernel(u_ref, lrr_ref, lir_ref, ldr_ref, lrc_ref, lic_ref, ldc_ref,
               brt_ref, bit_ref, cr_ref, ci_ref, crt_ref, cit_ref, dv_ref,
               y_ref, w_scr, kv_scr, *, bsz):
    t = S5_CHUNK
    p = SSM_STATE
    gsz = SSM_GROUP
    lrr, lir = lrr_ref[0], lir_ref[0]
    dtr = jnp.exp(ldr_ref[0])
    lrc, lic = lrc_ref[0], lic_ref[0]
    dtc = jnp.exp(ldc_ref[0])

    mag = jnp.exp(lrr * dtr)
    ar, ai = mag * jnp.cos(lir * dtr), mag * jnp.sin(lir * dtr)
    den = lrr * lrr + lir * lir
    nr = ar - 1.0
    fr, fi = (nr * lrr + ai * lir) / den, (ai * lrr - nr * lir) / den
    brt, bit = brt_ref[0], bit_ref[0]
    bbr, bbi = fr * brt - fi * bit, fr * bit + fi * brt

    cr, ci = cr_ref[0], ci_ref[0]
    bbr_h, bbi_h = bbr[:, :p], bbi[:, :p]
    cbr = jnp.concatenate([cr[i:i + 1] * bbr_h - ci[i:i + 1] * bbi_h for i in range(gsz)], axis=0)
    cbi = jnp.concatenate([cr[i:i + 1] * bbi_h + ci[i:i + 1] * bbr_h for i in range(gsz)], axis=0)
    lag = lax.broadcasted_iota(I32, (p, t), 1).astype(F32)
    lrc_h, lic_h, dtc_h = lrc[:p], lic[:p], dtc[:p]
    pmag = jnp.exp(lrc_h * dtc_h * lag)
    pth = lic_h * dtc_h * lag
    kv_scr[...] = (jnp.dot(cbr, pmag * jnp.cos(pth), preferred_element_type=F32, precision=HIGHEST)
                   - jnp.dot(cbi, pmag * jnp.sin(pth), preferred_element_type=F32, precision=HIGHEST))

    row = lax.broadcasted_iota(I32, (t, t), 0)
    col = lax.broadcasted_iota(I32, (t, t), 1)
    causal = col >= row

    def toeplitz_cols(i, carry):
        c0 = pl.multiple_of(i * t, t)
        for k in range(gsz):
            vec = kv_scr[pl.ds(i * gsz + k, 1), :]
            blk = pltpu.roll(jnp.broadcast_to(vec, (t, t)), 0, 1, stride=1, stride_axis=0)
            w_scr[k * t:(k + 1) * t, pl.ds(c0, t)] = jnp.where(causal, blk, 0.0).astype(BF16)
        return carry

    lax.fori_loop(0, gsz, toeplitz_cols, 0)

    lo = lax.broadcasted_iota(I32, (t, 2 * p), 1) < p
    rem = (t - 1 - lax.broadcasted_iota(I32, (t, 2 * p), 0)).astype(F32)
    emag = jnp.exp(lrr * dtr * rem)
    eth = lir * dtr * rem
    ecs, esn = emag * jnp.cos(eth), emag * jnp.sin(eth)
    e_same, e_swap = jnp.where(lo, ecs, esn), jnp.where(lo, esn, ecs)
    sgn = jnp.where(lo[:1], -1.0, 1.0)
    wb = jnp.concatenate([e_same * bbr[k:k + 1] + e_swap * (bbi[k:k + 1] * sgn) for k in range(gsz)],
                         axis=0).astype(BF16)

    top = lax.broadcasted_iota(I32, (2 * p, t), 0) < p
    step = (lax.broadcasted_iota(I32, (2 * p, t), 1) + 1).astype(F32)
    cmag = jnp.exp(lrc * dtc * step)
    cth = lic * dtc * step
    mc, ms = cmag * jnp.cos(cth), cmag * jnp.sin(cth)
    crt, cit = crt_ref[0], cit_ref[0]
    wc = jnp.concatenate(
        [jnp.where(top, crt[:, i:i + 1] * mc - cit[:, i:i + 1] * ms,
                   -(crt[:, i:i + 1] * ms + cit[:, i:i + 1] * mc)) for i in range(gsz)],
        axis=1).astype(BF16)

    u = u_ref[0]
    y = jnp.dot(u, w_scr[...], preferred_element_type=F32)
    g = jnp.dot(u, wb, preferred_element_type=F32)
    gr, gi = g[:, :p], g[:, p:]

    m_t = jnp.exp(lrr[:, :p] * dtr[:, :p] * float(t))
    at_r = m_t * jnp.cos(lir[:, :p] * dtr[:, :p] * float(t))
    at_i = m_t * jnp.sin(lir[:, :p] * dtr[:, :p] * float(t))
    nc = u.shape[0] // bsz
    hr = jnp.zeros((bsz, p), F32)
    hi = jnp.zeros((bsz, p), F32)
    hrs, his = [hr], [hi]
    for c in range(1, nc):
        sl = slice((c - 1) * bsz, c * bsz)
        hr, hi = at_r * hr - at_i * hi + gr[sl], at_r * hi + at_i * hr + gi[sl]
        hrs.append(hr)
        his.append(hi)
    hcat = jnp.concatenate([jnp.concatenate(hrs, axis=0), jnp.concatenate(his, axis=0)], axis=1)
    y = y + jnp.dot(hcat.astype(BF16), wc, preferred_element_type=F32)
    y_ref[0] = (y + dv_ref[0] * u.astype(F32)).astype(BF16)


def _s5(u_ssm, lam_re, lam_im, log_dt, b_re, b_im, c_re, c_im, d_skip):
    bsz, seq, _ = u_ssm.shape
    t, g, gs, p = S5_CHUNK, SSM_GROUPS, SSM_GROUP, SSM_STATE
    nc = seq // t
    rows = nc * bsz
    ug = u_ssm.reshape(bsz, nc, t, g, gs).transpose(3, 1, 0, 4, 2).reshape(g, rows, gs * t)

    dup_r = lambda a: jnp.concatenate([a, a], axis=-1)[:, None, :]
    dup_c = lambda a: jnp.concatenate([a, a], axis=-1)[:, :, None]
    ld = jnp.broadcast_to(log_dt[:, None], (g, p))
    brt = jnp.swapaxes(b_re, 1, 2)
    bit = jnp.swapaxes(b_im, 1, 2)
    crt = jnp.swapaxes(c_re, 1, 2)
    cit = jnp.swapaxes(c_im, 1, 2)
    args = (
        ug,
        dup_r(lam_re), dup_r(lam_im), dup_r(ld), dup_c(lam_re), dup_c(lam_im), dup_c(ld),
        jnp.concatenate([brt, brt], -1), jnp.concatenate([bit, bit], -1),
        c_re, c_im, jnp.concatenate([crt, crt], 1), jnp.concatenate([cit, cit], 1),
        jnp.repeat(d_skip.reshape(g, gs), t, axis=1)[:, None, :],
    )
    per_g = lambda a: pl.BlockSpec((1,) + a.shape[1:], lambda i: (i,) + (0,) * (a.ndim - 1))
    yg = pl.pallas_call(
        functools.partial(_s5_kernel, bsz=bsz),
        out_shape=jax.ShapeDtypeStruct((g, rows, gs * t), BF16),
        grid=(g,),
        in_specs=[per_g(a) for a in args],
        out_specs=pl.BlockSpec((1, rows, gs * t), lambda i: (i, 0, 0)),
        scratch_shapes=[pltpu.VMEM((gs * t, gs * t), BF16), pltpu.VMEM((gs * gs, t), F32)],
        compiler_params=_cparams("arbitrary"),
        name="s5",
    )(*args)
    return yg.reshape(g, nc, bsz, gs, t).transpose(2, 1, 4, 0, 3).reshape(bsz, seq, g * gs)


def _dsa_kernel(qt_ref, qit_ref, wt_ref, k_ref, vt_ref, ki_ref, y_ref,
                key_scr, hi_scr, lo_scr, m_scr, acc_scr, sacc_scr, s_scr, lg_scr, p_scr, *, n_sel):
    tq, tk = Q_TILE, K_TILE
    qi = pl.program_id(1)
    nkt = qi + 1
    q0 = qi * tq
    diff = (lax.broadcasted_iota(I32, (tk, tq), 0) - lax.broadcasted_iota(I32, (tk, tq), 1))

    npair = (nkt + 1) // 2
    rows = lambda kt: pl.ds(pl.multiple_of(kt * tk, tk), tk)

    def score_tile(kt, causal_mask):
        k0 = kt * tk
        ki_t = ki_ref[0, rows(kt), :]
        s = jnp.zeros((tk, tq), F32)
        for h in range(IDX_HEADS):
            rel = jnp.dot(ki_t, qit_ref[0, h * IDX_DIM:(h + 1) * IDX_DIM, :], preferred_element_type=F32)
            s = s + jnp.maximum(rel, 0.0) * wt_ref[0, h:h + 1, :]
        bits = lax.bitcast_convert_type(s, I32)
        sign = bits >> 31
        key = ((bits & 0x7FFFFFFF) ^ sign) - sign
        if causal_mask:
            key = jnp.where(diff <= q0 - k0, key, INT_MIN)
        key_scr[rows(kt), :] = key
        hi_scr[rows(kt), :] = (key >> 16).astype(I16)
        lo_scr[rows(kt), :] = ((key & 0xFFFF) - HALF16).astype(I16)

    def score_group(g, carry):
        for u in range(TILE_GROUP):
            score_tile(g * TILE_GROUP + u, False)
        return carry

    def score_pair(j, carry):
        score_tile(2 * j, False)
        score_tile(2 * j + 1, False)
        return carry

    full_groups = (npair - 1) // (TILE_GROUP // 2)
    lax.fori_loop(0, full_groups, score_group, 0)
    lax.fori_loop(full_groups * (TILE_GROUP // 2), npair - 1, score_pair, 0)
    score_tile(2 * npair - 2, True)
    score_tile(2 * npair - 1, True)

    def tile_loop(fn, carry):
        def group(g, c):
            for u in range(TILE_GROUP):
                c = fn(g * TILE_GROUP + u, c)
            return c

        def pair(j, c):
            return fn(2 * j + 1, fn(2 * j, c))

        pairs_per_group = TILE_GROUP // 2
        ngrp = npair // pairs_per_group
        carry = lax.fori_loop(0, ngrp, group, carry)
        return lax.fori_loop(ngrp * pairs_per_group, npair, pair, carry)

    def tree_sum(parts):
        while len(parts) > 1:
            parts = [a + b for a, b in zip(parts[::2], parts[1::2])]
        return parts[0]

    def count16(ref, pred, with_tile=False):
        def tile(kt, acc):
            vals = ref[rows(kt), :]
            hit = jnp.where(pred(vals, kt) if with_tile else pred(vals), jnp.int16(1), jnp.int16(0))
            return acc + tree_sum([hit[r:r + PACK16] for r in range(0, tk, PACK16)])
        part = tile_loop(tile, jnp.zeros((PACK16, tq), I16))
        return jnp.sum(part.astype(I32), axis=0, keepdims=True)

    def search16(ref, target):
        def bit_step(i, state):
            c_u, c_cnt = state
            trial = c_u | (jnp.int32(1) << (15 - i))
            trial_s = (trial - HALF16).astype(I16)
            cnt = count16(ref, lambda v: v >= trial_s)
            ok = cnt >= target
            return jnp.where(ok, trial, c_u), jnp.where(ok, cnt, c_cnt)
        zero = jnp.zeros((1, tq), I32)
        return lax.fori_loop(0, 16, bit_step, (zero, zero))

    hi_u, n_hi_ge = search16(hi_scr, n_sel)
    hi_thr = hi_u - HALF16
    hi_thr16 = hi_thr.astype(I16)

    def keep_bucket(hi, kt):
        lo_scr[rows(kt), :] = jnp.where(hi == hi_thr16, lo_scr[rows(kt), :], jnp.int16(-HALF16))
        return hi > hi_thr16

    cnt_above = count16(hi_scr, keep_bucket, with_tile=True)
    lo_thr, n_lo_ge = search16(lo_scr, n_sel - cnt_above)
    thr = jnp.maximum(hi_thr * (2 * HALF16) + lo_thr, INT_MIN + 1)
    in_bucket = jnp.where(lo_thr != 0, n_lo_ge, n_hi_ge - cnt_above)
    cnt_ge = jnp.where(hi_u != 0, cnt_above + in_bucket, 0)

    neg_bits = np.float32(NEG_BIG).view(np.int32)

    def to_bias(kt, carry):
        keys = key_scr[rows(kt), :]
        key_scr[rows(kt), :] = jnp.where(keys > thr, 0, jnp.where(keys == thr, INT_MIN, neg_bits))
        return carry

    tile_loop(to_bias, 0)

    @pl.when(jnp.max(cnt_ge) > n_sel)
    def _():
        lo_thr16 = (lo_thr - HALF16).astype(I16)
        cnt_gt = cnt_above + count16(lo_scr, lambda v: v > lo_thr16)
        need = n_sel - cnt_gt
        sub = lax.broadcasted_iota(I32, (tk, tq), 0)
        no_tie = jnp.int16(HALF16 - 1)

        def tie_index(kt, state):
            seen, cut_tile = state
            idx = (sub + kt * tk).astype(I16)
            at_lo = jnp.where(lo_scr[rows(kt), :] == lo_thr16, idx, no_tie)
            tie = jnp.where(hi_scr[rows(kt), :] == hi_thr16, at_lo, no_tie)
            lo_scr[rows(kt), :] = tie
            hit = jnp.where(tie != no_tie, jnp.int16(1), jnp.int16(0))
            here = tree_sum([hit[r:r + PACK16] for r in range(0, tk, PACK16)])
            after = seen + jnp.sum(here.astype(I32), axis=0, keepdims=True)
            return after, jnp.where((seen <= need) & (after > need), kt, cut_tile)

        zero = jnp.zeros((1, tq), I32)
        _, cut_tile = tile_loop(tie_index, (zero, zero + 2 * npair))

        tile_bits = tk.bit_length() - 1

        def idx_step(i, j_cut):
            trial = j_cut | (jnp.int32(1) << (tile_bits - 1 - i))
            trial16 = trial.astype(I16)
            cnt = count16(lo_scr, lambda v: v < trial16)
            return jnp.where(cnt <= need, trial, j_cut)

        j_cut = lax.fori_loop(0, tile_bits, idx_step, cut_tile << tile_bits)

        def drop(kt, carry):
            bits = key_scr[rows(kt), :]
            key_scr[rows(kt), :] = jnp.where((bits == INT_MIN) & (sub + kt * tk >= j_cut), neg_bits, bits)
            return carry

        tile_loop(drop, 0)

    ta = ATTN_TILES * tk
    nstep = (nkt + ATTN_TILES - 1) // ATTN_TILES
    m_scr[...] = jnp.full(m_scr.shape, NEG_BIG, F32)
    sacc_scr[...] = jnp.full(sacc_scr.shape, NEG_BIG, F32)
    acc_scr[...] = jnp.zeros(acc_scr.shape, F32)
    ones_rows = jnp.ones((PACK16, ta), BF16)
    span = lambda i: pl.ds(pl.multiple_of(i * ta, ta), ta)

    def logits(i, heads=range(N_HEADS)):
        for h in heads:
            g = h // KV_GROUP
            lg = (jnp.dot(k_ref[0, span(i), g * HEAD_DIM:(g + 1) * HEAD_DIM],
                          qt_ref[0, h * HEAD_DIM:(h + 1) * HEAD_DIM, :], preferred_element_type=F32)
                  + lax.bitcast_convert_type(key_scr[span(i), :], F32))
            lg_scr[h] = lg
            m_scr[h] = jnp.maximum(m_scr[h], jnp.max(lg, axis=0, keepdims=True))

    def probs(heads=range(N_HEADS)):
        for h in heads:
            s = m_scr[h]
            s_scr[h] = s
            p_scr[h] = jnp.exp2(lg_scr[h] - s).astype(BF16)

    def accumulate(i, heads=range(N_HEADS)):
        for h in heads:
            g = h // KV_GROUP
            vt_t = jnp.concatenate([vt_ref[0, g * HEAD_DIM:(g + 1) * HEAD_DIM, span(i)], ones_rows], axis=0)
            s_new = s_scr[h]
            alpha = jnp.exp2(sacc_scr[h] - s_new)
            acc_scr[h] = alpha * acc_scr[h] + jnp.dot(vt_t, p_scr[h], preferred_element_type=F32)
            sacc_scr[h] = s_new

    logits(0)

    @pl.when(nstep == 1)
    def _():
        probs()
        accumulate(0)

    @pl.when(nstep > 1)
    def _():
        groups = [range(g * KV_GROUP, (g + 1) * KV_GROUP) for g in range(N_KV_HEADS)]
        for hs in groups:
            probs(hs)
            logits(1, hs)

        def step(i, carry):
            for hs in groups:
                accumulate(i - 1, hs)
                probs(hs)
                logits(i + 1, hs)
            return carry

        lax.fori_loop(1, nstep - 1, step, 0)
        for hs in groups:
            accumulate(nstep - 2, hs)
            probs(hs)
            accumulate(nstep - 1, hs)

    for h in range(N_HEADS):
        o = acc_scr[h, :HEAD_DIM] / acc_scr[h, HEAD_DIM:HEAD_DIM + 1]
        y_ref[0, :, h * HEAD_DIM:(h + 1) * HEAD_DIM] = o.T.astype(BF16)


def _dsa(qt, qit, wtt, k, vt, ki):
    bsz, seq, _ = k.shape
    n_sel = min(TOPK_MAX, seq // 4)
    tq = Q_TILE
    assert Q_TILE == K_TILE and seq % (2 * K_TILE) == 0 and seq < HALF16
    kern = functools.partial(_dsa_kernel, n_sel=n_sel)
    return pl.pallas_call(
        kern,
        out_shape=jax.ShapeDtypeStruct((bsz, seq, ATTN_WIDTH), BF16),
        grid=(bsz, seq // tq),
        in_specs=[
            pl.BlockSpec((1, ATTN_WIDTH, tq), lambda b, i: (b, 0, i)),
            pl.BlockSpec((1, IDX_WIDTH, tq), lambda b, i: (b, 0, i)),
            pl.BlockSpec((1, IDX_HEADS, tq), lambda b, i: (b, 0, i)),
            pl.BlockSpec((1, seq, KV_WIDTH), lambda b, i: (b, 0, 0)),
            pl.BlockSpec((1, KV_WIDTH, seq), lambda b, i: (b, 0, 0)),
            pl.BlockSpec((1, seq, IDX_DIM), lambda b, i: (b, 0, 0)),
        ],
        out_specs=pl.BlockSpec((1, tq, ATTN_WIDTH), lambda b, i: (b, i, 0)),
        scratch_shapes=[
            pltpu.VMEM((seq, tq), I32),
            pltpu.VMEM((seq, tq), I16),
            pltpu.VMEM((seq, tq), I16),
            pltpu.VMEM((N_HEADS, 1, tq), F32),
            pltpu.VMEM((N_HEADS, HEAD_DIM + PACK16, tq), F32),
            pltpu.VMEM((N_HEADS, 1, tq), F32),
            pltpu.VMEM((N_HEADS, 1, tq), F32),
            pltpu.VMEM((N_HEADS, ATTN_TILES * K_TILE, tq), F32),
            pltpu.VMEM((N_HEADS, ATTN_TILES * K_TILE, tq), BF16),
        ],
        compiler_params=_cparams("arbitrary", "arbitrary"),
        name="dsa",
    )(qt, qit, wtt, k, vt, ki)


def _deepnorm_ln(x, h, gate, g, b):
    y = DEEPNORM_ALPHA * x + (1.0 + gate) * h
    mu = jnp.mean(y, axis=-1, keepdims=True)
    yc = y - mu
    var = jnp.mean(yc * yc, axis=-1, keepdims=True)
    return yc * lax.rsqrt(var + LN_EPS) * g + b


def _tail_kernel(x_ref, ys_ref, ya_ref, sg_ref, gt1_ref, sc2_ref, sh2_ref, gt2_ref,
                 wglu_ref, bglu_ref, pssm_ref, pattn_ref, wout_ref, ln1g_ref, ln1b_ref,
                 wgu_ref, wd_ref, ln2g_ref, ln2b_ref, o_ref, hid_scr):
    y = jax.nn.gelu(ys_ref[0].astype(F32))
    glu = jnp.dot(y.astype(BF16), wglu_ref[...], preferred_element_type=F32) + bglu_ref[...]
    y = (y * jax.nn.sigmoid(glu)).astype(BF16)
    sg = sg_ref[0]
    merged = (sg[:, :D_MODEL].astype(F32) * jnp.dot(y, pssm_ref[...], preferred_element_type=F32)
              + sg[:, D_MODEL:].astype(F32) * jnp.dot(ya_ref[0], pattn_ref[...], preferred_element_type=F32))
    h = jnp.dot(merged.astype(BF16), wout_ref[...], preferred_element_type=F32)
    x1 = _deepnorm_ln(x_ref[0], h, gt1_ref[0], ln1g_ref[...], ln1b_ref[...])
    u = (x1 * (1.0 + sc2_ref[0]) + sh2_ref[0]).astype(BF16)
    for j in range(D_FF // FF_CHUNK):
        a = jnp.dot(u, wgu_ref[:, j * FF_CHUNK:(j + 1) * FF_CHUNK], preferred_element_type=F32)
        b = jnp.dot(u, wgu_ref[:, D_FF + j * FF_CHUNK:D_FF + (j + 1) * FF_CHUNK], preferred_element_type=F32)
        hid_scr[:, j * FF_CHUNK:(j + 1) * FF_CHUNK] = (a * jax.nn.sigmoid(a) * b).astype(BF16)
    f = jnp.dot(hid_scr[...], wd_ref[...], preferred_element_type=F32)
    o_ref[0] = _deepnorm_ln(x1, f, gt2_ref[0], ln2g_ref[...], ln2b_ref[...])


def _tail(x, ys, ya, sg, gt1, sc2, sh2, gt2, wglu, bglu, pssm, pattn, wout, ln1g, ln1b, wgu, wd, ln2g, ln2b, tm):
    bsz, seq, d = x.shape
    tok = lambda w: pl.BlockSpec((1, tm, w), lambda b, i: (b, i, 0))
    vec = pl.BlockSpec((1, 1, d), lambda b, i: (b, 0, 0))
    const = lambda a: pl.BlockSpec(a.shape, lambda b, i: (0, 0), pipeline_mode=pl.Buffered(1))
    weights = (wglu, bglu, pssm, pattn, wout, ln1g, ln1b, wgu, wd, ln2g, ln2b)
    return pl.pallas_call(
        _tail_kernel,
        out_shape=jax.ShapeDtypeStruct(x.shape, F32),
        grid=(bsz, seq // tm),
        in_specs=[tok(d), tok(SSM_WIDTH), tok(ATTN_WIDTH), tok(2 * D_MODEL), vec, vec, vec, vec]
                 + [const(w) for w in weights],
        out_specs=tok(d),
        scratch_shapes=[pltpu.VMEM((tm, D_FF), BF16)],
        compiler_params=_cparams("arbitrary", "arbitrary"),
        name="tail",
    )(x, ys, ya, sg, gt1, sc2, sh2, gt2, *weights)


def kernel(x, c, w_cond, b_cond, w_in, ssm_lam_re, ssm_lam_im, ssm_log_dt, ssm_b_re, ssm_b_im,
           ssm_c_re, ssm_c_im, ssm_d, ssm_w_glu, ssm_b_glu, p_ssm, p_attn, w_out,
           ln1_g, ln1_b, w_gate_up, w_down, ln2_g, ln2_b):
    bsz, seq, d = x.shape
    assert d == D_MODEL and seq % Q_TILE == 0 and seq % S5_CHUNK == 0
    tm = min(TOKEN_TILE, seq)
    assert seq % tm == 0
    tabs = _rope_tables(seq)
    mod = _cond(c, w_cond, b_cond)
    row = lambda a: a.reshape(1, -1)
    for l in range(DEPTH):
        sh1, sc1, gt1, sh2, sc2, gt2 = [mod[l, :, i * d:(i + 1) * d][:, None, :] for i in range(6)]
        wn, wt = _split_w_in(w_in[l])
        u_ssm, k, ki, sg, qt, vt, qit, wtt = _inproj(x, sc1, sh1, wn, wt, tabs, tm)
        ys = _s5(u_ssm, ssm_lam_re[l], ssm_lam_im[l], ssm_log_dt[l], ssm_b_re[l], ssm_b_im[l],
                 ssm_c_re[l], ssm_c_im[l], ssm_d[l])
        ya = _dsa(qt, qit, wtt, k, vt, ki)
        x = _tail(x, ys, ya, sg, gt1, sc2, sh2, gt2, ssm_w_glu[l].astype(BF16), row(ssm_b_glu[l]),
                  p_ssm[l].astype(BF16), p_attn[l].astype(BF16), w_out[l].astype(BF16),
                  row(ln1_g[l]), row(ln1_b[l]), w_gate_up[l].astype(BF16), w_down[l].astype(BF16),
                  row(ln2_g[l]), row(ln2_b[l]), tm)
    return x
```
